```python
import jax, jax.numpy as jnp
from jax import lax
import numpy as np

D_MODEL = 1024
BATCH = 1
SEQ = 16384
DEPTH = 4

GRID_W = 64
CTX_LEN = 256
N_MIXERS = 3
N_FNET_LAYERS = (DEPTH + 2) // 3
N_ATTN_LAYERS = (DEPTH + 1) // 3
N_POOL_LAYERS = DEPTH // 3
N_DENSE_LAYERS = (DEPTH + 1) // 2
N_MOE_LAYERS = DEPTH // 2

FNET_GROUPS = 4
FNET_GROUP_DIM = D_MODEL // FNET_GROUPS

N_HEADS = 16
N_KV_HEADS = 4
HEAD_DIM = D_MODEL // N_HEADS
Q_PER_KV = N_HEADS // N_KV_HEADS
Q_COLS = N_HEADS * HEAD_DIM
KV_COLS = N_KV_HEADS * HEAD_DIM
WINDOW = 128
ATTN_BLOCK = 128
ROPE_BASE = 10000.0

POOL_WINDOWS = (2, 4, 8, 16)
POOL_GROUPS = len(POOL_WINDOWS)
POOL_GROUP_DIM = D_MODEL // POOL_GROUPS

D_FF = 3584
N_EXPERTS = 8
TOP_K = 2

NORM_EPS = 1e-6
NEG_INF = -1e30

kernel_name = "hybrid_fnet_swa_pool_moe_dit"


def rms_norm(x, gain):
    xf = x.astype(jnp.float32)
    y = xf * lax.rsqrt(jnp.mean(xf * xf, axis=-1, keepdims=True) + NORM_EPS)
    return (y * gain.astype(jnp.float32)).astype(x.dtype)


def ada_params(cond, w, b):
    m = jax.nn.silu(cond) @ w + b
    return [t[..., None, :] for t in jnp.split(m, 6, axis=-1)]


def modulate(h, shift, scale):
    return h * (1 + scale) + shift


def fourier_mix(h, w):
    b, l, d = h.shape
    hg = h.astype(jnp.float32).reshape(b, l, FNET_GROUPS, FNET_GROUP_DIM)
    f = jnp.fft.fft2(hg, axes=(1, 3), norm="ortho").real
    return f.reshape(b, l, d).astype(h.dtype) @ w


def axial_rope_tables(length):
    rows = length // GRID_W
    row_pos = jnp.repeat(jnp.arange(rows, dtype=jnp.float32), GRID_W)
    col_pos = jnp.tile(jnp.arange(GRID_W, dtype=jnp.float32), rows)
    axis_dim = HEAD_DIM // 2
    inv_freq = ROPE_BASE ** (-jnp.arange(0, axis_dim, 2, dtype=jnp.float32) / axis_dim)
    ang_r = row_pos[:, None] * inv_freq[None, :]
    ang_c = col_pos[:, None] * inv_freq[None, :]
    return jnp.cos(ang_r), jnp.sin(ang_r), jnp.cos(ang_c), jnp.sin(ang_c)


def rope_part(xp, cos, sin):
    x1, x2 = jnp.split(xp, 2, axis=-1)
    cs = cos[None, :, None, :]
    sn = sin[None, :, None, :]
    return jnp.concatenate([x1 * cs - x2 * sn, x1 * sn + x2 * cs], axis=-1)


def apply_axial_rope(x, cos_r, sin_r, cos_c, sin_c):
    xf = x.astype(jnp.float32)
    half = HEAD_DIM // 2
    out = jnp.concatenate([rope_part(xf[..., :half], cos_r, sin_r),
                           rope_part(xf[..., half:], cos_c, sin_c)], axis=-1)
    return out.astype(x.dtype)


def q_heads(q_flat, gain):
    b, l, _ = q_flat.shape
    return rms_norm(q_flat.reshape(b, l, N_HEADS, HEAD_DIM), gain)


def kv_heads(kv_flat, gain):
    b, l, _ = kv_flat.shape
    k, v = jnp.split(kv_flat, 2, axis=-1)
    k = rms_norm(k.reshape(b, l, N_KV_HEADS, HEAD_DIM), gain)
    return k, v.reshape(b, l, N_KV_HEADS, HEAD_DIM)


def latent_window_attention(q, k, v, kc, vc, sink):
    b, l, _, _ = q.shape
    nb = l // ATTN_BLOCK
    scale = HEAD_DIM ** -0.5
    qb = q.astype(jnp.float32).reshape(b, nb, ATTN_BLOCK, N_KV_HEADS, Q_PER_KV, HEAD_DIM) * scale

    def band(t):
        tp = jnp.pad(t.astype(jnp.float32), ((0, 0), (ATTN_BLOCK, ATTN_BLOCK), (0, 0), (0, 0)))
        tp = tp.reshape(b, nb + 2, ATTN_BLOCK, N_KV_HEADS, HEAD_DIM)
        return jnp.concatenate([tp[:, :-2], tp[:, 1:-1], tp[:, 2:]], axis=2)

    kb = band(k)
    vb = band(v)
    blk = jnp.arange(nb)[:, None, None]
    q_pos = blk * ATTN_BLOCK + jnp.arange(ATTN_BLOCK)[None, :, None]
    k_pos = (blk - 1) * ATTN_BLOCK + jnp.arange(3 * ATTN_BLOCK)[None, None, :]
    valid = (jnp.abs(q_pos - k_pos) <= WINDOW) & (k_pos >= 0) & (k_pos < l)

    s_loc = jnp.einsum('bnqkgd,bnskd->bnkgqs', qb, kb)
    s_loc = jnp.where(valid[None, :, None, None], s_loc, NEG_INF)
    kcf = kc.astype(jnp.float32)
    vcf = vc.astype(jnp.float32)
    s_ctx = jnp.einsum('bnqkgd,bckd->bnkgqc', qb, kcf)
    sk = sink.astype(jnp.float32).reshape(N_KV_HEADS, Q_PER_KV)[None, None, :, :, None, None]
    m = jnp.maximum(jnp.maximum(s_loc.max(-1, keepdims=True), s_ctx.max(-1, keepdims=True)), sk)
    p_loc = jnp.exp(s_loc - m)
    p_ctx = jnp.exp(s_ctx - m)
    denom = p_loc.sum(-1, keepdims=True) + p_ctx.sum(-1, keepdims=True) + jnp.exp(sk - m)
    o = (jnp.einsum('bnkgqs,bnskd->bnkgqd', p_loc, vb)
         + jnp.einsum('bnkgqc,bckd->bnkgqd', p_ctx, vcf)) / denom
    o = o.transpose(0, 1, 4, 2, 3, 5).reshape(b, l, N_HEADS * HEAD_DIM)
    return o.astype(q.dtype)


def context_attention(qc, kc, vc, sink):
    b, cl, _, _ = qc.shape
    qg = qc.astype(jnp.float32).reshape(b, cl, N_KV_HEADS, Q_PER_KV, HEAD_DIM) * HEAD_DIM ** -0.5
    s = jnp.einsum('bqkgd,bckd->bkgqc', qg, kc.astype(jnp.float32))
    sk = jnp.broadcast_to(sink.astype(jnp.float32).reshape(N_KV_HEADS, Q_PER_KV)[None, :, :, None, None],
                          s.shape[:-1] + (1,))
    p = jax.nn.softmax(jnp.concatenate([s, sk], axis=-1), axis=-1)[..., :-1]
    o = jnp.einsum('bkgqc,bckd->bqkgd', p, vc.astype(jnp.float32))
    return o.reshape(b, cl, N_HEADS * HEAD_DIM).astype(qc.dtype)


def attention_mixer(h, hc, wqkv, wo, q_gain, k_gain, sink, rope, ctx_queries):
    cos_r, sin_r, cos_c, sin_c = rope
    qkv = h @ wqkv
    q = apply_axial_rope(q_heads(qkv[..., :Q_COLS], q_gain), cos_r, sin_r, cos_c, sin_c)
    k, v = kv_heads(qkv[..., Q_COLS:], k_gain)
    k = apply_axial_rope(k, cos_r, sin_r, cos_c, sin_c)
    if ctx_queries:
        qkv_c = hc @ wqkv
        kc, vc = kv_heads(qkv_c[..., Q_COLS:], k_gain)
        yc = context_attention(q_heads(qkv_c[..., :Q_COLS], q_gain), kc, vc, sink) @ wo
    else:
        kc, vc = kv_heads(hc @ wqkv[:, Q_COLS:], k_gain)
        yc = None
    y = latent_window_attention(q, k, v, kc, vc, sink) @ wo
    return y, yc


def multiscale_pool(h, w, scale):
    b, l, d = h.shape
    hg = h.astype(jnp.float32).reshape(b, l, POOL_GROUPS, POOL_GROUP_DIM)
    cs = jnp.pad(lax.cumsum(hg, axis=1), ((0, 0), (1, 0), (0, 0), (0, 0)))
    t = jnp.arange(l)[:, None]
    win = jnp.array(POOL_WINDOWS, dtype=jnp.int32)[None, :]
    lo = jnp.clip(t - win // 2, 0, l - 1)
    hi = jnp.clip(t + win - win // 2 - 1, 0, l - 1)
    g = jnp.arange(POOL_GROUPS)[None, :]
    win_sum = cs[:, hi + 1, g, :] - cs[:, lo, g, :]
    count = (hi - lo + 1).astype(jnp.float32)[None, :, :, None]
    pooled = (win_sum / count - hg).astype(h.dtype)
    y = jnp.einsum('blgc,gcd->blgd', pooled, w).reshape(b, l, d)
    return y * scale


def swiglu(h, wi, wo):
    gate, up = jnp.split(h @ wi, 2, axis=-1)
    return (jax.nn.silu(gate) * up) @ wo


def moe_swiglu(h, router, wi, wo):
    logits = (h @ router).astype(jnp.float32)
    top_val, top_idx = lax.top_k(logits, TOP_K)
    top_w = jax.nn.softmax(top_val, axis=-1)
    combine = jnp.sum(top_w[..., None] * jax.nn.one_hot(top_idx, N_EXPERTS, dtype=jnp.float32), axis=-2)
    out = jnp.zeros(h.shape, jnp.float32)
    for e in range(N_EXPERTS):
        out = out + combine[..., e:e + 1] * swiglu(h, wi[e], wo[e]).astype(jnp.float32)
    return out.astype(h.dtype)


def setup_inputs(seed: int = 0) -> dict:
    key = jax.random.key(seed)
    ks = iter(jax.random.split(key, 32))

    def nrm(shape, s):
        return jax.random.normal(next(ks), shape, jnp.float32) * s

    D = D_MODEL
    return {
        "x": nrm((BATCH, SEQ, D), 1.0),
        "c": nrm((BATCH, D), 1.0),
        "ctx": nrm((BATCH, CTX_LEN, D), 1.0),
        "c_ctx": nrm((D,), 1.0),
        "ada_w": nrm((DEPTH, D, 6 * D), 0.5 * D ** -0.5),
        "ada_b": nrm((DEPTH, 6 * D), 0.02),
        "norm_mix": 1.0 + nrm((DEPTH, D), 0.02),
        "norm_ffn": 1.0 + nrm((DEPTH, D), 0.02),
        "fnet_w": nrm((N_FNET_LAYERS, D, D), D ** -0.5),
        "attn_wqkv": nrm((N_ATTN_LAYERS, D, Q_COLS + 2 * KV_COLS), D ** -0.5),
        "attn_q_gain": 1.0 + nrm((N_ATTN_LAYERS, HEAD_DIM), 0.02),
        "attn_k_gain": 1.0 + nrm((N_ATTN_LAYERS, HEAD_DIM), 0.02),
        "attn_sink": nrm((N_ATTN_LAYERS, N_HEADS), 0.5),
        "attn_wo": nrm((N_ATTN_LAYERS, Q_COLS, D), Q_COLS ** -0.5),
        "pool_w": nrm((N_POOL_LAYERS, POOL_GROUPS, POOL_GROUP_DIM, POOL_GROUP_DIM), POOL_GROUP_DIM ** -0.5),
        "pool_scale": 1.0 + nrm((N_POOL_LAYERS, D), 0.1),
        "ffn_wi": nrm((N_DENSE_LAYERS, D, 2 * D_FF), D ** -0.5),
        "ffn_wo": nrm((N_DENSE_LAYERS, D_FF, D), D_FF ** -0.5),
        "moe_router": nrm((N_MOE_LAYERS, D, N_EXPERTS), D ** -0.5),
        "moe_wi": nrm((N_MOE_LAYERS, N_EXPERTS, D, 2 * D_FF), D ** -0.5),
        "moe_wo": nrm((N_MOE_LAYERS, N_EXPERTS, D_FF, D), D_FF ** -0.5),
    }


def reference(x, c, ctx, c_ctx, ada_w, ada_b, norm_mix, norm_ffn, fnet_w, attn_wqkv, attn_q_gain,
              attn_k_gain, attn_sink, attn_wo, pool_w, pool_scale, ffn_wi, ffn_wo, moe_router,
              moe_wi, moe_wo):
    rope = axial_rope_tables(x.shape[1])
    attn_layers = [i for i in range(DEPTH) if i % N_MIXERS == 1]
    last_ctx_read = attn_layers[-1] if attn_layers else -1

    for i in range(DEPTH):
        mixer = i % N_MIXERS
        j = i // N_MIXERS
        f = i // 2
        ctx_full = i < last_ctx_read
        ctx_live = i <= last_ctx_read

        sh1, sc1, g1, sh2, sc2, g2 = ada_params(c, ada_w[i], ada_b[i])
        h = modulate(rms_norm(x, norm_mix[i]), sh1, sc1)
        if ctx_live:
            csh1, csc1, cg1, csh2, csc2, cg2 = ada_params(c_ctx, ada_w[i], ada_b[i])
            hc = modulate(rms_norm(ctx, norm_mix[i]), csh1, csc1)

        yc = None
        if mixer == 0:
            y = fourier_mix(h, fnet_w[j])
            if ctx_full:
                yc = fourier_mix(hc, fnet_w[j])
        elif mixer == 1:
            y, yc = attention_mixer(h, hc, attn_wqkv[j], attn_wo[j], attn_q_gain[j], attn_k_gain[j],
                                    attn_sink[j], rope, ctx_full)
        else:
            y = multiscale_pool(h, pool_w[j], pool_scale[j])
            if ctx_full:
                yc = multiscale_pool(hc, pool_w[j], pool_scale[j])

        x = x + g1 * y
        h2 = modulate(rms_norm(x, norm_ffn[i]), sh2, sc2)
        if i % 2 == 0:
            x = x + g2 * swiglu(h2, ffn_wi[f], ffn_wo[f])
        else:
            x = x + g2 * moe_swiglu(h2, moe_router[f], moe_wi[f], moe_wo[f])

        if ctx_full:
            ctx = ctx + cg1 * yc
            hc2 = modulate(rms_norm(ctx, norm_ffn[i]), csh2, csc2)
            if i % 2 == 0:
                ctx = ctx + cg2 * swiglu(hc2, ffn_wi[f], ffn_wo[f])
            else:
                ctx = ctx + cg2 * moe_swiglu(hc2, moe_router[f], moe_wi[f], moe_wo[f])
    return x
```

```python
import functools
import math

import numpy as np
import jax
import jax.numpy as jnp
from jax import lax
from jax.experimental import pallas as pl
from jax.experimental.pallas import tpu as pltpu

F32 = jnp.float32
BF16 = jnp.bfloat16

D_MODEL = 1024
GRID_W = 64
N_MIXERS = 3
FNET_GROUPS = 4
FNET_GROUP_DIM = D_MODEL // FNET_GROUPS
N_HEADS = 16
N_KV_HEADS = 4
HEAD_DIM = 64
Q_COLS = N_HEADS * HEAD_DIM
KV_COLS = N_KV_HEADS * HEAD_DIM
WINDOW = 128
ATTN_BLOCK = 128
ROPE_BASE = 10000.0
POOL_WINDOWS = (2, 4, 8, 16)
POOL_GROUP_DIM = D_MODEL // len(POOL_WINDOWS)
POOL_HALO = 8
D_FF = 3584
N_EXPERTS = 8
NORM_EPS = 1e-6
NEG_INF = -1e30

LANES = 128
SUBLANES = 8
VMEM_LIMIT = 56 * 1024 * 1024

ROW_TILE = 512
FF_CHUNK = 512
MOE_TILE = 512
ROUTE_TILE = 256


def _cparams(sem, vmem=None):
    return pltpu.CompilerParams(dimension_semantics=sem, vmem_limit_bytes=vmem)


def _mod_norm(x, mods):
    ms = jnp.mean(x * x, axis=-1, keepdims=True)
    y = x * lax.rsqrt(ms + NORM_EPS) * mods[0:1, :]
    return y * (1.0 + mods[2:3, :]) + mods[1:2, :]


def _dot(a, b):
    return jnp.dot(a, b, preferred_element_type=F32)


def _ada_body(cc_ref, w_ref, b_ref, o_ref):
    a = cc_ref[...]
    a = a / (1.0 + jnp.exp(-a))
    o_ref[0] = jnp.dot(a, w_ref[0], precision=lax.Precision.HIGHEST,
                       preferred_element_type=F32) + b_ref[0]


def _ada_call(cc, ada_w, ada_b):
    depth, d, n6 = ada_w.shape
    tn = n6 // 4
    return pl.pallas_call(
        _ada_body,
        grid=(depth, n6 // tn),
        in_specs=[pl.BlockSpec((SUBLANES, d), lambda l, j: (0, 0)),
                  pl.BlockSpec((1, d, tn), lambda l, j: (l, 0, j)),
                  pl.BlockSpec((1, 1, tn), lambda l, j: (l, 0, j))],
        out_specs=pl.BlockSpec((1, SUBLANES, tn), lambda l, j: (l, 0, j)),
        out_shape=jax.ShapeDtypeStruct((depth, SUBLANES, n6), F32),
        compiler_params=_cparams(("arbitrary", "arbitrary"), VMEM_LIMIT),
        name="ada",
    )(cc, ada_w, ada_b.reshape(depth, 1, n6))


def _cos_sin(n, period):
    k = np.arange(n, dtype=np.float64)
    ang = 2.0 * np.pi * np.outer(k, k) / period
    return np.cos(ang).astype(np.float32), np.sin(ang).astype(np.float32)


def _channel_table():
    c, s = _cos_sin(FNET_GROUP_DIM, FNET_GROUP_DIM)
    return (jnp.concatenate([jnp.asarray(c), jnp.asarray(s)], axis=0)
            * (FNET_GROUP_DIM ** -0.5)).astype(BF16)


def _sequence_tables(n):
    c, s = _cos_sin(n, n)
    c, s = jnp.asarray(c), jnp.asarray(s)
    eye = jnp.eye(SUBLANES, dtype=F32)
    rows = n * SUBLANES
    f = jnp.stack([c, -s]) * (1.0 / n)
    ka = (f[:, None, :, :, None] * eye[None, :, None, None, :]).reshape(2 * rows, rows)
    ck = (c[:, None, :, None] * eye[None, :, None, :]).reshape(rows, rows)
    sk = (s[:, None, :, None] * eye[None, :, None, :]).reshape(rows, rows)
    cs = jnp.concatenate([ck, sk], axis=1)
    tc, ts = _cos_sin(n, n * n)
    oct_ = n // SUBLANES
    tc = jnp.broadcast_to(jnp.asarray(tc).reshape(oct_, rows, 1), (oct_, rows, LANES))
    ts = jnp.broadcast_to(jnp.asarray(ts).reshape(oct_, rows, 1), (oct_, rows, LANES))
    return ka.astype(BF16), cs.astype(BF16), tc, ts


def _fft_a_body(x_ref, mods_ref, ka_ref, tc_ref, ts_ref, zr_ref, zi_ref, h_scr):
    c = pl.program_id(1)
    rows = h_scr.shape[1]
    w = zr_ref.shape[-1]

    @pl.when(c == 0)
    def _():
        x = x_ref[...].reshape(rows, D_MODEL)
        h = _mod_norm(x, mods_ref[...]).astype(BF16)
        for k in range(D_MODEL // w):
            h_scr[k] = h[:, k * w:(k + 1) * w]

    hc = h_scr[c]
    z = _dot(ka_ref[...], hc)
    zr, zi = z[:rows], z[rows:]
    tc, ts = tc_ref[0], ts_ref[0]
    for t in range(w // LANES):
        sl = slice(t * LANES, (t + 1) * LANES)
        a, b = zr[:, sl], zi[:, sl]
        zr_ref[:, :, sl] = (a * tc + b * ts).reshape(SUBLANES, rows // SUBLANES, LANES)
        zi_ref[:, :, sl] = (b * tc - a * ts).reshape(SUBLANES, rows // SUBLANES, LANES)


def _fft_b_body(zr_ref, zi_ref, x_ref, mods_ref, cs_ref, ch_ref, w_ref, o_ref, acc):
    g = pl.program_id(1)
    rows = acc.shape[0]
    gw = zr_ref.shape[-1]
    zr = zr_ref[...].reshape(rows, gw).astype(BF16)
    zi = zi_ref[...].reshape(rows, gw).astype(BF16)
    cs = cs_ref[...]
    xr = _dot(cs, jnp.concatenate([zr, zi], axis=0))
    xi = _dot(cs, jnp.concatenate([zi, -zr], axis=0))
    y = _dot(jnp.concatenate([xr, xi], axis=1).astype(BF16), ch_ref[...])
    part = _dot(y.astype(BF16), w_ref[...])

    @pl.when(g == 0)
    def _():
        acc[...] = part

    @pl.when(g != 0)
    def _():
        acc[...] += part

    @pl.when(g == pl.num_programs(1) - 1)
    def _():
        x = x_ref[...].reshape(rows, D_MODEL)
        o_ref[...] = (x + mods_ref[3:4, :] * acc[...]).reshape(o_ref.shape)


def _fourier_layer(x, mods, w_bf16, seq_tables, ch):
    L = x.shape[0]
    n = math.isqrt(L)
    rows = n * SUBLANES
    oct_ = n // SUBLANES
    ka, cs, tc, ts = seq_tables
    x3 = x.reshape(n, n, D_MODEL)
    gw = FNET_GROUP_DIM
    ng = D_MODEL // gw
    zshape = jax.ShapeDtypeStruct((n, n, D_MODEL), F32)
    zr, zi = pl.pallas_call(
        _fft_a_body,
        grid=(oct_, ng),
        in_specs=[pl.BlockSpec((n, SUBLANES, D_MODEL), lambda o, c: (0, o, 0)),
                  pl.BlockSpec((SUBLANES, D_MODEL), lambda o, c: (0, 0)),
                  pl.BlockSpec((2 * rows, rows), lambda o, c: (0, 0)),
                  pl.BlockSpec((1, rows, LANES), lambda o, c: (o, 0, 0)),
                  pl.BlockSpec((1, rows, LANES), lambda o, c: (o, 0, 0))],
        out_specs=[pl.BlockSpec((SUBLANES, n, gw), lambda o, c: (o, 0, c)),
                   pl.BlockSpec((SUBLANES, n, gw), lambda o, c: (o, 0, c))],
        out_shape=[zshape, zshape],
        scratch_shapes=[pltpu.VMEM((ng, rows, gw), BF16)],
        compiler_params=_cparams(("arbitrary", "arbitrary"), VMEM_LIMIT),
        name="fft_a",
    )(x3, mods, ka, tc, ts)
    out = pl.pallas_call(
        _fft_b_body,
        grid=(oct_, ng),
        in_specs=[pl.BlockSpec((n, SUBLANES, gw), lambda p, g: (0, p, g)),
                  pl.BlockSpec((n, SUBLANES, gw), lambda p, g: (0, p, g)),
                  pl.BlockSpec((n, SUBLANES, D_MODEL), lambda p, g: (0, p, 0)),
                  pl.BlockSpec((SUBLANES, D_MODEL), lambda p, g: (0, 0)),
                  pl.BlockSpec((rows, 2 * rows), lambda p, g: (0, 0)),
                  pl.BlockSpec((2 * gw, gw), lambda p, g: (0, 0)),
                  pl.BlockSpec((gw, D_MODEL), lambda p, g: (g, 0))],
        out_specs=pl.BlockSpec((n, SUBLANES, D_MODEL), lambda p, g: (0, p, 0)),
        out_shape=jax.ShapeDtypeStruct((n, n, D_MODEL), F32),
        scratch_shapes=[pltpu.VMEM((rows, D_MODEL), F32)],
        compiler_params=_cparams(("arbitrary", "arbitrary"), VMEM_LIMIT),
        name="fft_b",
    )(zr, zi, x3, mods, cs, ch, w_bf16)
    return out.reshape(L, D_MODEL)


def _ctx_fourier_body(x_ref, mods_ref, f_ref, ch_ref, w_ref, o_ref):
    x = x_ref[...]
    n = x.shape[0]
    h = _mod_norm(x, mods_ref[...]).astype(BF16)
    g = _dot(f_ref[...], h)
    gr, gi = g[:n], g[n:]
    gw = FNET_GROUP_DIM
    ys = []
    for k in range(D_MODEL // gw):
        sl = slice(k * gw, (k + 1) * gw)
        ys.append(_dot(jnp.concatenate([gr[:, sl], gi[:, sl]], axis=1).astype(BF16), ch_ref[...]))
    y = jnp.concatenate(ys, axis=1).astype(BF16)
    o_ref[...] = x + mods_ref[3:4, :] * _dot(y, w_ref[...])


def _ctx_fourier_layer(ctx, mods, w_bf16, ch):
    n = ctx.shape[0]
    c, s = _cos_sin(n, n)
    f = (jnp.concatenate([jnp.asarray(c), -jnp.asarray(s)], axis=0) * (n ** -0.5)).astype(BF16)
    return pl.pallas_call(
        _ctx_fourier_body,
        out_shape=jax.ShapeDtypeStruct(ctx.shape, F32),
        compiler_params=pltpu.CompilerParams(vmem_limit_bytes=VMEM_LIMIT),
        name="ctx_fourier",
    )(ctx, mods, f, ch, w_bf16)


def _ffn_body(x_ref, mods_ref, wg_ref, wu_ref, wo_ref, o_ref, h_scr, acc):
    j = pl.program_id(1)

    @pl.when(j == 0)
    def _():
        h_scr[...] = _mod_norm(x_ref[...], mods_ref[...]).astype(BF16)

    h = h_scr[...]
    gate = _dot(h, wg_ref[...])
    up = _dot(h, wu_ref[...])
    act = (gate / (1.0 + jnp.exp(-gate)) * up).astype(BF16)
    part = _dot(act, wo_ref[...])

    @pl.when(j == 0)
    def _():
        acc[...] = part

    @pl.when(j != 0)
    def _():
        acc[...] += part

    @pl.when(j == pl.num_programs(1) - 1)
    def _():
        o_ref[...] = x_ref[...] + mods_ref[3:4, :] * acc[...]


def _ffn_layer(x, mods, wi_bf16, wo_bf16):
    n = x.shape[0]
    tm = min(ROW_TILE, n)
    nj = D_FF // FF_CHUNK
    return pl.pallas_call(
        _ffn_body,
        grid=(n // tm, nj),
        in_specs=[pl.BlockSpec((tm, D_MODEL), lambda i, j: (i, 0)),
                  pl.BlockSpec((SUBLANES, D_MODEL), lambda i, j: (0, 0)),
                  pl.BlockSpec((D_MODEL, FF_CHUNK), lambda i, j: (0, j)),
                  pl.BlockSpec((D_MODEL, FF_CHUNK), lambda i, j: (0, nj + j)),
                  pl.BlockSpec((FF_CHUNK, D_MODEL), lambda i, j: (j, 0))],
        out_specs=pl.BlockSpec((tm, D_MODEL), lambda i, j: (i, 0)),
        out_shape=jax.ShapeDtypeStruct(x.shape, F32),
        scratch_shapes=[pltpu.VMEM((tm, D_MODEL), BF16), pltpu.VMEM((tm, D_MODEL), F32)],
        compiler_params=_cparams(("arbitrary", "arbitrary"), VMEM_LIMIT),
        name="ffn",
    )(x, mods, wi_bf16, wi_bf16, wo_bf16)


def _qkv_body(x_ref, mods_ref, w_ref, bd_ref, gains_ref, cos_ref, sin_ref, q_ref, k_ref, v_ref):
    h = _mod_norm(x_ref[...], mods_ref[...]).astype(BF16)
    qkv = _dot(h, w_ref[...])
    bd = bd_ref[...]
    cos, sin = cos_ref[...], sin_ref[...]
    lane = lax.broadcasted_iota(jnp.int32, (1, LANES), 1)
    first_half = (lane % (HEAD_DIM // 2)) < (HEAD_DIM // 4)
    low_head = lane < HEAD_DIM

    def norm_rope(a, gain):
        sq = a * a
        hi = sq.astype(BF16)
        lo = (sq - hi.astype(F32)).astype(BF16)
        ms = _dot(hi, bd) + _dot(lo, bd)
        an = a * lax.rsqrt(ms + NORM_EPS) * gain
        partner = jnp.where(first_half,
                            pltpu.roll(an, LANES - HEAD_DIM // 4, 1),
                            pltpu.roll(an, HEAD_DIM // 4, 1))
        return an * cos + partner * sin

    def dup_heads(a):
        r = pltpu.roll(a, HEAD_DIM, 1)
        return jnp.where(low_head, a, r), jnp.where(low_head, r, a)

    nq = Q_COLS // LANES
    for t in range(nq):
        a = norm_rope(qkv[:, t * LANES:(t + 1) * LANES], gains_ref[0:1, :])
        q_ref[:, t * LANES:(t + 1) * LANES] = a.astype(BF16)
    for t in range(KV_COLS // LANES):
        kt = norm_rope(qkv[:, Q_COLS + t * LANES:Q_COLS + (t + 1) * LANES], gains_ref[1:2, :])
        k0, k1 = dup_heads(kt)
        k_ref[2 * t] = k0.astype(BF16)
        k_ref[2 * t + 1] = k1.astype(BF16)
        v0, v1 = dup_heads(qkv[:, Q_COLS + KV_COLS + t * LANES:Q_COLS + KV_COLS + (t + 1) * LANES])
        v_ref[2 * t] = v0.astype(BF16)
        v_ref[2 * t + 1] = v1.astype(BF16)


def _qkv_call(x, mods, wqkv_bf16, bd, gains, cos_t, sin_t):
    n = x.shape[0]
    tm = min(ROW_TILE, n)
    ncol = Q_COLS + 2 * KV_COLS
    return pl.pallas_call(
        _qkv_body,
        grid=(n // tm,),
        in_specs=[pl.BlockSpec((tm, D_MODEL), lambda i: (i, 0)),
                  pl.BlockSpec((SUBLANES, D_MODEL), lambda i: (0, 0)),
                  pl.BlockSpec((D_MODEL, ncol), lambda i: (0, 0)),
                  pl.BlockSpec((LANES, LANES), lambda i: (0, 0)),
                  pl.BlockSpec((SUBLANES, LANES), lambda i: (0, 0)),
                  pl.BlockSpec((tm, LANES), lambda i: (i, 0)),
                  pl.BlockSpec((tm, LANES), lambda i: (i, 0))],
        out_specs=[pl.BlockSpec((tm, Q_COLS), lambda i: (i, 0)),
                   pl.BlockSpec((N_KV_HEADS, tm, LANES), lambda i: (0, i, 0)),
                   pl.BlockSpec((N_KV_HEADS, tm, LANES), lambda i: (0, i, 0))],
        out_shape=[jax.ShapeDtypeStruct((n, Q_COLS), BF16),
                   jax.ShapeDtypeStruct((N_KV_HEADS, n, LANES), BF16),
                   jax.ShapeDtypeStruct((N_KV_HEADS, n, LANES), BF16)],
        compiler_params=_cparams(("arbitrary",), VMEM_LIMIT),
        name="qkv",
    )(x, mods, wqkv_bf16, bd, gains, cos_t, sin_t)


def _attn_body(sink_ref, q_ref, kp_ref, kc_ref, kn_ref, vp_ref, vc_ref, vn_ref,
               kx_ref, vx_ref, o_ref):
    b = pl.program_id(0)
    nb = pl.num_programs(0)
    blk = ATTN_BLOCK
    lane = lax.broadcasted_iota(jnp.int32, (1, LANES), 1)
    low_head = lane < HEAD_DIM
    qi = lax.broadcasted_iota(jnp.int32, (blk, 3 * blk), 0)
    kj = lax.broadcasted_iota(jnp.int32, (blk, 3 * blk), 1)
    valid = (kj >= qi) & (kj <= qi + 2 * WINDOW)
    valid = valid & ((kj >= blk) | (b > 0)) & ((kj < 2 * blk) | (b < nb - 1))
    per_kv = N_HEADS // N_KV_HEADS
    for g in range(N_KV_HEADS):
        kb = jnp.concatenate([kp_ref[g], kc_ref[g], kn_ref[g], kx_ref[g]], axis=0)
        vb = jnp.concatenate([vp_ref[g], vc_ref[g], vn_ref[g], vx_ref[g]], axis=0)
        parts = []
        for p in range(per_kv // 2):
            t = g * (per_kv // 2) + p
            qt = q_ref[:, t * LANES:(t + 1) * LANES]
            zero = jnp.zeros_like(qt)
            parts += [jnp.where(low_head, qt, zero), jnp.where(low_head, zero, qt)]
        lhs = jnp.concatenate(parts, axis=0)
        s = lax.dot_general(lhs, kb, (((1,), (1,)), ((), ())), preferred_element_type=F32)
        outs = []
        for hh in range(per_kv):
            sink = sink_ref[g * per_kv + hh]
            sh = s[hh * blk:(hh + 1) * blk]
            s_loc = jnp.where(valid, sh[:, :3 * blk], NEG_INF)
            s_ctx = sh[:, 3 * blk:]
            m = jnp.maximum(jnp.maximum(jnp.max(s_loc, axis=-1, keepdims=True),
                                        jnp.max(s_ctx, axis=-1, keepdims=True)), sink)
            p_loc = jnp.exp(s_loc - m)
            p_ctx = jnp.exp(s_ctx - m)
            den = (jnp.sum(p_loc, axis=-1, keepdims=True) + jnp.sum(p_ctx, axis=-1, keepdims=True)
                   + jnp.exp(sink - m))
            pr = jnp.concatenate([p_loc, p_ctx], axis=1).astype(BF16)
            outs.append(_dot(pr, vb) / den)
        for p in range(per_kv // 2):
            t = g * (per_kv // 2) + p
            o_ref[:, t * LANES:(t + 1) * LANES] = jnp.where(
                low_head, outs[2 * p], outs[2 * p + 1]).astype(BF16)


def _attn_call(sink, q, kd, vd, kx, vx):
    n = q.shape[0]
    blk = ATTN_BLOCK
    nb = n // blk
    nctx = kx.shape[1]
    kv_spec = lambda f: pl.BlockSpec((N_KV_HEADS, blk, LANES), f)
    prev = lambda b: (0, jnp.maximum(b - 1, 0), 0)
    cur = lambda b: (0, b, 0)
    nxt = lambda b: (0, jnp.minimum(b + 1, nb - 1), 0)
    ctx_spec = pl.BlockSpec((N_KV_HEADS, nctx, LANES), lambda b: (0, 0, 0))
    return pl.pallas_call(
        _attn_body,
        grid=(nb,),
        in_specs=[pl.BlockSpec(memory_space=pltpu.SMEM),
                  pl.BlockSpec((blk, Q_COLS), lambda b: (b, 0)),
                  kv_spec(prev), kv_spec(cur), kv_spec(nxt),
                  kv_spec(prev), kv_spec(cur), kv_spec(nxt),
                  ctx_spec, ctx_spec],
        out_specs=pl.BlockSpec((blk, Q_COLS), lambda b: (b, 0)),
        out_shape=jax.ShapeDtypeStruct((n, Q_COLS), BF16),
        compiler_params=_cparams(("arbitrary",), VMEM_LIMIT),
        name="attn",
    )(sink, q, kd, kd, kd, vd, vd, vd, kx, vx)


def _proj_body(a_ref, x_ref, mods_ref, w_ref, o_ref):
    o_ref[...] = x_ref[...] + mods_ref[3:4, :] * _dot(a_ref[...], w_ref[...])


def _proj_call(a, x, mods, w_bf16):
    n = x.shape[0]
    tm = min(ROW_TILE, n)
    return pl.pallas_call(
        _proj_body,
        grid=(n // tm,),
        in_specs=[pl.BlockSpec((tm, a.shape[1]), lambda i: (i, 0)),
                  pl.BlockSpec((tm, D_MODEL), lambda i: (i, 0)),
                  pl.BlockSpec((SUBLANES, D_MODEL), lambda i: (0, 0)),
                  pl.BlockSpec(w_bf16.shape, lambda i: (0, 0))],
        out_specs=pl.BlockSpec((tm, D_MODEL), lambda i: (i, 0)),
        out_shape=jax.ShapeDtypeStruct(x.shape, F32),
        compiler_params=_cparams(("arbitrary",), VMEM_LIMIT),
        name="proj",
    )(a, x, mods, w_bf16)


def _rope_lane_tables(length):
    rows = length // GRID_W
    row_pos = jnp.repeat(jnp.arange(rows, dtype=F32), GRID_W)
    col_pos = jnp.tile(jnp.arange(GRID_W, dtype=F32), rows)
    axis_dim = HEAD_DIM // 2
    inv_freq = ROPE_BASE ** (-jnp.arange(0, axis_dim, 2, dtype=F32) / axis_dim)
    ang_r = row_pos[:, None] * inv_freq[None, :]
    ang_c = col_pos[:, None] * inv_freq[None, :]
    cos_h = jnp.concatenate([jnp.cos(ang_r), jnp.cos(ang_r), jnp.cos(ang_c), jnp.cos(ang_c)], axis=1)
    sin_h = jnp.concatenate([-jnp.sin(ang_r), jnp.sin(ang_r), -jnp.sin(ang_c), jnp.sin(ang_c)], axis=1)
    reps = LANES // HEAD_DIM
    return jnp.tile(cos_h, (1, reps)), jnp.tile(sin_h, (1, reps))


def _attention_layer(x, ctx, mods_x, mods_c, wqkv, wo, q_gain, k_gain, sink):
    L = x.shape[0]
    nctx = ctx.shape[0]
    wqkv_b = wqkv.astype(BF16)
    head = np.arange(LANES) // HEAD_DIM
    bd = jnp.asarray((head[:, None] == head[None, :]).astype(np.float32) / HEAD_DIM).astype(BF16)
    reps = LANES // HEAD_DIM
    gains = jnp.zeros((SUBLANES, LANES), F32)
    gains = gains.at[0].set(jnp.tile(q_gain, reps) * (HEAD_DIM ** -0.5))
    gains = gains.at[1].set(jnp.tile(k_gain, reps))
    cos_t, sin_t = _rope_lane_tables(L)
    q, kd, vd = _qkv_call(x, mods_x, wqkv_b, bd, gains, cos_t, sin_t)
    ones = jnp.ones((nctx, LANES), F32)
    _, kx, vx = _qkv_call(ctx, mods_c, wqkv_b, bd, gains, ones, jnp.zeros_like(ones))
    o = _attn_call(sink, q, kd, vd, kx, vx)
    return _proj_call(o, x, mods_x, wo.astype(BF16))


def _pool_body(xp_ref, xc_ref, xn_ref, mods_ref, w_ref, o_ref, h_scr):
    i = pl.program_id(0)
    tm = xc_ref.shape[0]
    total = tm * pl.num_programs(0)
    x = xc_ref[...]
    mods = mods_ref[...]
    h_scr[0:POOL_HALO, :] = jnp.where(i > 0, _mod_norm(xp_ref[...], mods), 0.0)
    h_scr[POOL_HALO:POOL_HALO + tm, :] = _mod_norm(x, mods)
    h_scr[POOL_HALO + tm:, :] = jnp.where(i < pl.num_programs(0) - 1, _mod_norm(xn_ref[...], mods), 0.0)
    t = i * tm + lax.broadcasted_iota(jnp.int32, (tm, 1), 0)
    gd = POOL_GROUP_DIM
    for g, win in enumerate(POOL_WINDOWS):
        sl = slice(g * gd, (g + 1) * gd)
        half = win // 2
        tot = h_scr[POOL_HALO - half:POOL_HALO - half + tm, sl]
        for s in range(-half + 1, half):
            tot = tot + h_scr[POOL_HALO + s:POOL_HALO + s + tm, sl]
        lo = jnp.maximum(t - half, 0)
        hi = jnp.minimum(t + half - 1, total - 1)
        cnt = (hi - lo + 1).astype(F32)
        pooled = (tot / cnt - h_scr[POOL_HALO:POOL_HALO + tm, sl]).astype(BF16)
        y = _dot(pooled, w_ref[g]) * mods[4:5, sl]
        o_ref[:, sl] = x[:, sl] + mods[3:4, sl] * y


def _pool_layer(x, mods, w_bf16):
    n = x.shape[0]
    tm = min(ROW_TILE, n)
    r = tm // POOL_HALO
    last = n // POOL_HALO - 1
    return pl.pallas_call(
        _pool_body,
        grid=(n // tm,),
        in_specs=[pl.BlockSpec((POOL_HALO, D_MODEL), lambda i: (jnp.maximum(i * r - 1, 0), 0)),
                  pl.BlockSpec((tm, D_MODEL), lambda i: (i, 0)),
                  pl.BlockSpec((POOL_HALO, D_MODEL), lambda i: (jnp.minimum((i + 1) * r, last), 0)),
                  pl.BlockSpec((SUBLANES, D_MODEL), lambda i: (0, 0)),
                  pl.BlockSpec(w_bf16.shape, lambda i: (0, 0, 0))],
        out_specs=pl.BlockSpec((tm, D_MODEL), lambda i: (i, 0)),
        out_shape=jax.ShapeDtypeStruct(x.shape, F32),
        scratch_shapes=[pltpu.VMEM((tm + 2 * POOL_HALO, D_MODEL), F32)],
        compiler_params=_cparams(("arbitrary",), VMEM_LIMIT),
        name="pool",
    )(x, x, x, mods, w_bf16)


def _router_body(x_ref, mods_ref, r_ref, h_ref, route_ref):
    h = _mod_norm(x_ref[...], mods_ref[...])
    h_ref[...] = h
    logits = jnp.dot(h, r_ref[...], precision=lax.Precision.HIGHEST, preferred_element_type=F32)
    lane = lax.broadcasted_iota(jnp.int32, logits.shape, 1)
    logits = jnp.where(lane < N_EXPERTS, logits, -jnp.inf)
    m1 = jnp.max(logits, axis=-1, keepdims=True)
    i1 = jnp.min(jnp.where(logits == m1, lane, LANES), axis=-1, keepdims=True)
    rest = jnp.where(lane == i1, -jnp.inf, logits)
    m2 = jnp.max(rest, axis=-1, keepdims=True)
    i2 = jnp.min(jnp.where(rest == m2, lane, LANES), axis=-1, keepdims=True)
    e = jnp.exp(m2 - m1)
    w1 = 1.0 / (1.0 + e)
    w2 = e / (1.0 + e)
    col = lax.broadcasted_iota(jnp.int32, route_ref.shape, 1)
    route_ref[...] = jnp.where(col == 0, i1.astype(F32),
                               jnp.where(col == 1, i2.astype(F32),
                                         jnp.where(col == 2, w1, jnp.where(col == 3, w2, 0.0))))


def _router_call(x, mods, router_pad):
    n = x.shape[0]
    tm = min(ROW_TILE, n)
    return pl.pallas_call(
        _router_body,
        grid=(n // tm,),
        in_specs=[pl.BlockSpec((tm, D_MODEL), lambda i: (i, 0)),
                  pl.BlockSpec((SUBLANES, D_MODEL), lambda i: (0, 0)),
                  pl.BlockSpec((D_MODEL, LANES), lambda i: (0, 0))],
        out_specs=[pl.BlockSpec((tm, D_MODEL), lambda i: (i, 0)),
                   pl.BlockSpec((tm, SUBLANES), lambda i: (i, 0))],
        out_shape=[jax.ShapeDtypeStruct((n, D_MODEL), F32),
                   jax.ShapeDtypeStruct((n, SUBLANES), F32)],
        compiler_params=_cparams(("arbitrary",), VMEM_LIMIT),
        name="router",
    )(x, mods, router_pad)


def _row_copy(src, src_row, dst, dst_row, sem):
    return pltpu.make_async_copy(src.at[pl.ds(src_row, 1)], dst.at[pl.ds(dst_row, 1)], sem)


def _scatter_body(pos_ref, h_ref, xs_in_ref, xs_ref, sem):
    del xs_in_ref
    i = pl.program_id(0)
    tm = h_ref.shape[0]
    base = i * tm * 2

    def issue(t, c):
        _row_copy(h_ref, t, xs_ref, pos_ref[base + 2 * t], sem).start()
        _row_copy(h_ref, t, xs_ref, pos_ref[base + 2 * t + 1], sem).start()
        return c

    lax.fori_loop(0, tm, issue, 0)

    def drain(t, c):
        _row_copy(h_ref, 0, xs_ref, 0, sem).wait()
        _row_copy(h_ref, 0, xs_ref, 0, sem).wait()
        return c

    lax.fori_loop(0, tm, drain, 0)


def _scatter_call(pos, h, n_rows):
    n = h.shape[0]
    tm = min(ROUTE_TILE, n)
    zeros = jnp.zeros((n_rows, D_MODEL), F32)
    return pl.pallas_call(
        _scatter_body,
        grid_spec=pltpu.PrefetchScalarGridSpec(
            num_scalar_prefetch=1,
            grid=(n // tm,),
            in_specs=[pl.BlockSpec((tm, D_MODEL), lambda i, pos: (i, 0)),
                      pl.BlockSpec(memory_space=pl.ANY)],
            out_specs=pl.BlockSpec(memory_space=pl.ANY),
            scratch_shapes=[pltpu.SemaphoreType.DMA(())]),
        out_shape=jax.ShapeDtypeStruct((n_rows, D_MODEL), F32),
        input_output_aliases={2: 0},
        compiler_params=_cparams(("arbitrary",), VMEM_LIMIT),
        name="moe_scatter",
    )(pos, h, zeros)


def _moe_body(te_ref, tv_ref, xs_ref, wg_ref, wu_ref, wo_ref, ys_ref, h_scr, acc):
    i = pl.program_id(0)
    j = pl.program_id(1)
    live = tv_ref[i] > 0

    @pl.when(live & (j == 0))
    def _():
        h_scr[...] = xs_ref[...].astype(BF16)

    @pl.when(live)
    def _():
        h = h_scr[...]
        gate = _dot(h, wg_ref[...])
        up = _dot(h, wu_ref[...])
        act = (gate / (1.0 + jnp.exp(-gate)) * up).astype(BF16)
        part = _dot(act, wo_ref[...])

        @pl.when(j == 0)
        def _():
            acc[...] = part

        @pl.when(j != 0)
        def _():
            acc[...] += part

    @pl.when(j == pl.num_programs(1) - 1)
    def _():
        ys_ref[...] = jnp.where(live, acc[...], 0.0)


def _moe_call(tile_expert, tile_live, xs, wi_bf16, wo_bf16):
    n_rows = xs.shape[0]
    nt = n_rows // MOE_TILE
    nj = D_FF // FF_CHUNK

    def jj(i, j, tv):
        return jnp.where(tv[i] > 0, j, nj - 1)

    return pl.pallas_call(
        _moe_body,
        grid_spec=pltpu.PrefetchScalarGridSpec(
            num_scalar_prefetch=2,
            grid=(nt, nj),
            in_specs=[pl.BlockSpec((MOE_TILE, D_MODEL), lambda i, j, te, tv: (i, 0)),
                      pl.BlockSpec((None, D_MODEL, FF_CHUNK), lambda i, j, te, tv: (te[i], 0, jj(i, j, tv))),
                      pl.BlockSpec((None, D_MODEL, FF_CHUNK),
                                   lambda i, j, te, tv: (te[i], 0, nj + jj(i, j, tv))),
                      pl.BlockSpec((None, FF_CHUNK, D_MODEL), lambda i, j, te, tv: (te[i], jj(i, j, tv), 0))],
            out_specs=pl.BlockSpec((MOE_TILE, D_MODEL), lambda i, j, te, tv: (i, 0)),
            scratch_shapes=[pltpu.VMEM((MOE_TILE, D_MODEL), BF16), pltpu.VMEM((MOE_TILE, D_MODEL), F32)]),
        out_shape=jax.ShapeDtypeStruct((n_rows, D_MODEL), F32),
        compiler_params=_cparams(("arbitrary", "arbitrary"), VMEM_LIMIT),
        name="moe_ffn",
    )(tile_expert, tile_live, xs, wi_bf16, wi_bf16, wo_bf16)


def _combine_body(pos_ref, x_ref, route_ref, mods_ref, ys_ref, o_ref, ga, gb, sem):
    i = pl.program_id(0)
    tm = x_ref.shape[0]
    base = i * tm * 2

    def issue(t, c):
        _row_copy(ys_ref, pos_ref[base + 2 * t], ga, t, sem).start()
        _row_copy(ys_ref, pos_ref[base + 2 * t + 1], gb, t, sem).start()
        return c

    lax.fori_loop(0, tm, issue, 0)

    def drain(t, c):
        _row_copy(ys_ref, 0, ga, 0, sem).wait()
        _row_copy(ys_ref, 0, gb, 0, sem).wait()
        return c

    lax.fori_loop(0, tm, drain, 0)
    route = route_ref[...]
    y = route[:, 2:3] * ga[...] + route[:, 3:4] * gb[...]
    o_ref[...] = x_ref[...] + mods_ref[3:4, :] * y


def _combine_call(pos, x, route, mods, ys):
    n = x.shape[0]
    tm = min(ROUTE_TILE, n)
    return pl.pallas_call(
        _combine_body,
        grid_spec=pltpu.PrefetchScalarGridSpec(
            num_scalar_prefetch=1,
            grid=(n // tm,),
            in_specs=[pl.BlockSpec((tm, D_MODEL), lambda i, pos: (i, 0)),
                      pl.BlockSpec((tm, SUBLANES), lambda i, pos: (i, 0)),
                      pl.BlockSpec((SUBLANES, D_MODEL), lambda i, pos: (0, 0)),
                      pl.BlockSpec(memory_space=pl.ANY)],
            out_specs=pl.BlockSpec((tm, D_MODEL), lambda i, pos: (i, 0)),
            scratch_shapes=[pltpu.VMEM((tm, D_MODEL), F32), pltpu.VMEM((tm, D_MODEL), F32),
                            pltpu.SemaphoreType.DMA(())]),
        out_shape=jax.ShapeDtypeStruct(x.shape, F32),
        compiler_params=_cparams(("arbitrary",), VMEM_LIMIT),
        name="moe_combine",
    )(pos, x, route, mods, ys)


def _routing_tables(route, n_tiles):
    idx = route[:, 0:2].astype(jnp.int32).reshape(-1)
    onehot = (idx[:, None] == jnp.arange(N_EXPERTS, dtype=jnp.int32)[None, :]).astype(jnp.int32)
    incl = jnp.cumsum(onehot, axis=0)
    rank = jnp.sum((incl - onehot) * onehot, axis=1)
    counts = incl[-1]
    tiles = (counts + MOE_TILE - 1) // MOE_TILE
    tile_end = jnp.cumsum(tiles)
    start = (tile_end - tiles) * MOE_TILE
    pos = jnp.sum(onehot * start[None, :], axis=1) + rank
    t = jnp.arange(n_tiles, dtype=jnp.int32)
    expert = jnp.sum((t[:, None] >= tile_end[None, :]).astype(jnp.int32), axis=1)
    live = (t < tile_end[-1]).astype(jnp.int32)
    last_expert = jnp.sum((tile_end[-1] - 1 >= tile_end).astype(jnp.int32))
    expert = jnp.where(live > 0, expert, last_expert)
    return pos.astype(jnp.int32), expert.astype(jnp.int32), live


def _moe_layer(x, mods, router, wi, wo):
    n = x.shape[0]
    router_pad = jnp.zeros((D_MODEL, LANES), F32).at[:, :N_EXPERTS].set(router)
    h, route = _router_call(x, mods, router_pad)
    n_tiles = (2 * n) // MOE_TILE + N_EXPERTS
    pos, tile_expert, tile_live = _routing_tables(route, n_tiles)
    xs = _scatter_call(pos, h, n_tiles * MOE_TILE)
    ys = _moe_call(tile_expert, tile_live, xs, wi.astype(BF16), wo.astype(BF16))
    return _combine_call(pos, x, route, mods, ys)


def _mods(gain, shift, scale, gate, extra=None):
    rows = [gain, shift, scale, gate, extra if extra is not None else jnp.zeros_like(gain)]
    m = jnp.stack(rows, axis=0)
    return jnp.concatenate([m, jnp.zeros((SUBLANES - m.shape[0], m.shape[1]), F32)], axis=0)


def kernel(x, c, ctx, c_ctx, ada_w, ada_b, norm_mix, norm_ffn, fnet_w, attn_wqkv, attn_q_gain,
           attn_k_gain, attn_sink, attn_wo, pool_w, pool_scale, ffn_wi, ffn_wo, moe_router,
           moe_wi, moe_wo):
    assert x.shape[0] == 1 and x.shape[2] == D_MODEL
    depth = ada_w.shape[0]
    xs = x[0]
    cs = ctx[0]
    cc = jnp.zeros((SUBLANES, D_MODEL), F32).at[0].set(c[0]).at[1].set(c_ctx)
    ada = _ada_call(cc, ada_w, ada_b)

    attn_layers = [i for i in range(depth) if i % N_MIXERS == 1]
    last_ctx_read = attn_layers[-1] if attn_layers else -1
    n_side = math.isqrt(xs.shape[0])
    assert n_side * n_side == xs.shape[0] and n_side % SUBLANES == 0
    seq_tables = _sequence_tables(n_side)
    ch = _channel_table()

    for i in range(depth):
        mixer = i % N_MIXERS
        j = i // N_MIXERS
        f = i // 2
        ctx_full = i < last_ctx_read
        ctx_live = i <= last_ctx_read
        sh1, sc1, g1, sh2, sc2, g2 = [ada[i, 0, k * D_MODEL:(k + 1) * D_MODEL] for k in range(6)]
        csh1, csc1, cg1, csh2, csc2, cg2 = [ada[i, 1, k * D_MODEL:(k + 1) * D_MODEL] for k in range(6)]
        extra = pool_scale[j] if mixer == 2 else None
        m1 = _mods(norm_mix[i], sh1, sc1, g1, extra)
        m2 = _mods(norm_ffn[i], sh2, sc2, g2)
        cm1 = _mods(norm_mix[i], csh1, csc1, cg1, extra)
        cm2 = _mods(norm_ffn[i], csh2, csc2, cg2)

        if mixer == 0:
            wb = fnet_w[j].astype(BF16)
            xs = _fourier_layer(xs, m1, wb, seq_tables, ch)
            if ctx_full:
                cs = _ctx_fourier_layer(cs, cm1, wb, ch)
        elif mixer == 1:
            assert ctx_live and not ctx_full
            xs = _attention_layer(xs, cs, m1, cm1, attn_wqkv[j], attn_wo[j], attn_q_gain[j],
                                  attn_k_gain[j], attn_sink[j])
        else:
            wb = pool_w[j].astype(BF16)
            xs = _pool_layer(xs, m1, wb)
            if ctx_full:
                cs = _pool_layer(cs, cm1, wb)

        if i % 2 == 0:
            wi_b, wo_b = ffn_wi[f].astype(BF16), ffn_wo[f].astype(BF16)
            xs = _ffn_layer(xs, m2, wi_b, wo_b)
            if ctx_full:
                cs = _ffn_layer(cs, cm2, wi_b, wo_b)
        else:
            xs = _moe_layer(xs, m2, moe_router[f], moe_wi[f], moe_wo[f])
            if ctx_full:
                cs = _moe_layer(cs, cm2, moe_router[f], moe_wi[f], moe_wo[f])
    return xs[None]
```

```python
import functools
import math

import numpy as np
import jax
import jax.numpy as jnp
from jax import lax
from jax.experimental import pallas as pl
from jax.experimental.pallas import tpu as pltpu

F32 = jnp.float32
BF16 = jnp.bfloat16

D_MODEL = 1024
GRID_W = 64
N_MIXERS = 3
FNET_GROUPS = 4
FNET_GROUP_DIM = D_MODEL // FNET_GROUPS
N_HEADS = 16
N_KV_HEADS = 4
HEAD_DIM = 64
Q_COLS = N_HEADS * HEAD_DIM
KV_COLS = N_KV_HEADS * HEAD_DIM
WINDOW = 128
ATTN_BLOCK = 128
ROPE_BASE = 10000.0
POOL_WINDOWS = (2, 4, 8, 16)
POOL_GROUP_DIM = D_MODEL // len(POOL_WINDOWS)
POOL_HALO = 8
D_FF = 3584
N_EXPERTS = 8
NORM_EPS = 1e-6
NEG_INF = -1e30

LANES = 128
SUBLANES = 8
VMEM_LIMIT = 56 * 1024 * 1024

ROW_TILE = 512
FF_CHUNK = 512
MOE_TILE = 512
ROUTE_TILE = 512
SEG_ALIGN = 16


def _cparams(sem, vmem=None):
    return pltpu.CompilerParams(dimension_semantics=sem, vmem_limit_bytes=vmem)


def _mod_norm(x, mods):
    ms = jnp.mean(x * x, axis=-1, keepdims=True)
    y = x * lax.rsqrt(ms + NORM_EPS) * mods[0:1, :]
    return y * (1.0 + mods[2:3, :]) + mods[1:2, :]


def _dot(a, b):
    return jnp.dot(a, b, preferred_element_type=F32)


def _ada_body(cc_ref, w_ref, b_ref, o_ref):
    a = cc_ref[...]
    a = a / (1.0 + jnp.exp(-a))
    o_ref[0] = jnp.dot(a, w_ref[0], precision=lax.Precision.HIGHEST,
                       preferred_element_type=F32) + b_ref[0]


def _ada_call(cc, ada_w, ada_b):
    depth, d, n6 = ada_w.shape
    tn = n6 // 4
    return pl.pallas_call(
        _ada_body,
        grid=(depth, n6 // tn),
        in_specs=[pl.BlockSpec((SUBLANES, d), lambda l, j: (0, 0)),
                  pl.BlockSpec((1, d, tn), lambda l, j: (l, 0, j)),
                  pl.BlockSpec((1, 1, tn), lambda l, j: (l, 0, j))],
        out_specs=pl.BlockSpec((1, SUBLANES, tn), lambda l, j: (l, 0, j)),
        out_shape=jax.ShapeDtypeStruct((depth, SUBLANES, n6), F32),
        compiler_params=_cparams(("arbitrary", "arbitrary"), VMEM_LIMIT),
        name="ada",
    )(cc, ada_w, ada_b.reshape(depth, 1, n6))


def _cos_sin(n, period):
    k = np.arange(n, dtype=np.float64)
    ang = 2.0 * np.pi * np.outer(k, k) / period
    return np.cos(ang).astype(np.float32), np.sin(ang).astype(np.float32)


def _channel_table():
    c, s = _cos_sin(FNET_GROUP_DIM, FNET_GROUP_DIM)
    return (jnp.concatenate([jnp.asarray(c), jnp.asarray(s)], axis=0)
            * (FNET_GROUP_DIM ** -0.5)).astype(BF16)


def _sequence_tables(n):
    c, s = _cos_sin(n, n)
    c, s = jnp.asarray(c), jnp.asarray(s)
    eye = jnp.eye(SUBLANES, dtype=F32)
    rows = n * SUBLANES
    f = jnp.stack([c, -s]) * (1.0 / n)
    ka = (f[:, None, :, :, None] * eye[None, :, None, None, :]).reshape(2 * rows, rows)
    ck = (c[:, None, :, None] * eye[None, :, None, :]).reshape(rows, rows)
    sk = (s[:, None, :, None] * eye[None, :, None, :]).reshape(rows, rows)
    cs = jnp.concatenate([ck, sk], axis=1)
    tc, ts = _cos_sin(n, n * n)
    oct_ = n // SUBLANES
    tc = jnp.broadcast_to(jnp.asarray(tc).reshape(oct_, rows, 1), (oct_, rows, LANES))
    ts = jnp.broadcast_to(jnp.asarray(ts).reshape(oct_, rows, 1), (oct_, rows, LANES))
    return ka.astype(BF16), cs.astype(BF16), tc, ts


def _fft_a_body(x_ref, mods_ref, ka_ref, tc_ref, ts_ref, zr_ref, zi_ref, h_scr):
    c = pl.program_id(1)
    rows = h_scr.shape[1]
    w = zr_ref.shape[-1]

    @pl.when(c == 0)
    def _():
        x = x_ref[...].reshape(rows, D_MODEL)
        h = _mod_norm(x, mods_ref[...]).astype(BF16)
        for k in range(D_MODEL // w):
            h_scr[k] = h[:, k * w:(k + 1) * w]

    hc = h_scr[c]
    z = _dot(ka_ref[...], hc)
    zr, zi = z[:rows], z[rows:]
    tc, ts = tc_ref[0], ts_ref[0]
    for t in range(w // LANES):
        sl = slice(t * LANES, (t + 1) * LANES)
        a, b = zr[:, sl], zi[:, sl]
        zr_ref[:, :, sl] = (a * tc + b * ts).reshape(SUBLANES, rows // SUBLANES, LANES)
        zi_ref[:, :, sl] = (b * tc - a * ts).reshape(SUBLANES, rows // SUBLANES, LANES)


def _fft_b_body(zr_ref, zi_ref, x_ref, mods_ref, cs_ref, ch_ref, w_ref, o_ref, acc):
    g = pl.program_id(1)
    rows = acc.shape[0]
    gw = zr_ref.shape[-1]
    zr = zr_ref[...].reshape(rows, gw).astype(BF16)
    zi = zi_ref[...].reshape(rows, gw).astype(BF16)
    cs = cs_ref[...]
    xr = _dot(cs, jnp.concatenate([zr, zi], axis=0))
    xi = _dot(cs, jnp.concatenate([zi, -zr], axis=0))
    y = _dot(jnp.concatenate([xr, xi], axis=1).astype(BF16), ch_ref[...])
    part = _dot(y.astype(BF16), w_ref[...])

    @pl.when(g == 0)
    def _():
        acc[...] = part

    @pl.when(g != 0)
    def _():
        acc[...] += part

    @pl.when(g == pl.num_programs(1) - 1)
    def _():
        x = x_ref[...].reshape(rows, D_MODEL)
        o_ref[...] = (x + mods_ref[3:4, :] * acc[...]).reshape(o_ref.shape)


def _fourier_layer(x, mods, w_bf16, seq_tables, ch):
    L = x.shape[0]
    n = math.isqrt(L)
    rows = n * SUBLANES
    oct_ = n // SUBLANES
    ka, cs, tc, ts = seq_tables
    x3 = x.reshape(n, n, D_MODEL)
    gw = FNET_GROUP_DIM
    ng = D_MODEL // gw
    zshape = jax.ShapeDtypeStruct((n, n, D_MODEL), F32)
    zr, zi = pl.pallas_call(
        _fft_a_body,
        grid=(oct_, ng),
        in_specs=[pl.BlockSpec((n, SUBLANES, D_MODEL), lambda o, c: (0, o, 0)),
                  pl.BlockSpec((SUBLANES, D_MODEL), lambda o, c: (0, 0)),
                  pl.BlockSpec((2 * rows, rows), lambda o, c: (0, 0)),
                  pl.BlockSpec((1, rows, LANES), lambda o, c: (o, 0, 0)),
                  pl.BlockSpec((1, rows, LANES), lambda o, c: (o, 0, 0))],
        out_specs=[pl.BlockSpec((SUBLANES, n, gw), lambda o, c: (o, 0, c)),
                   pl.BlockSpec((SUBLANES, n, gw), lambda o, c: (o, 0, c))],
        out_shape=[zshape, zshape],
        scratch_shapes=[pltpu.VMEM((ng, rows, gw), BF16)],
        compiler_params=_cparams(("arbitrary", "arbitrary"), VMEM_LIMIT),
        name="fft_a",
    )(x3, mods, ka, tc, ts)
    out = pl.pallas_call(
        _fft_b_body,
        grid=(oct_, ng),
        in_specs=[pl.BlockSpec((n, SUBLANES, gw), lambda p, g: (0, p, g)),
                  pl.BlockSpec((n, SUBLANES, gw), lambda p, g: (0, p, g)),
                  pl.BlockSpec((n, SUBLANES, D_MODEL), lambda p, g: (0, p, 0)),
                  pl.BlockSpec((SUBLANES, D_MODEL), lambda p, g: (0, 0)),
                  pl.BlockSpec((rows, 2 * rows), lambda p, g: (0, 0)),
                  pl.BlockSpec((2 * gw, gw), lambda p, g: (0, 0)),
                  pl.BlockSpec((gw, D_MODEL), lambda p, g: (g, 0))],
        out_specs=pl.BlockSpec((n, SUBLANES, D_MODEL), lambda p, g: (0, p, 0)),
        out_shape=jax.ShapeDtypeStruct((n, n, D_MODEL), F32),
        scratch_shapes=[pltpu.VMEM((rows, D_MODEL), F32)],
        compiler_params=_cparams(("arbitrary", "arbitrary"), VMEM_LIMIT),
        name="fft_b",
    )(zr, zi, x3, mods, cs, ch, w_bf16)
    return out.reshape(L, D_MODEL)


def _ctx_fourier_body(x_ref, mods_ref, f_ref, ch_ref, w_ref, o_ref):
    x = x_ref[...]
    n = x.shape[0]
    h = _mod_norm(x, mods_ref[...]).astype(BF16)
    g = _dot(f_ref[...], h)
    gr, gi = g[:n], g[n:]
    gw = FNET_GROUP_DIM
    ys = []
    for k in range(D_MODEL // gw):
        sl = slice(k * gw, (k + 1) * gw)
        ys.append(_dot(jnp.concatenate([gr[:, sl], gi[:, sl]], axis=1).astype(BF16), ch_ref[...]))
    y = jnp.concatenate(ys, axis=1).astype(BF16)
    o_ref[...] = x + mods_ref[3:4, :] * _dot(y, w_ref[...])


def _ctx_fourier_layer(ctx, mods, w_bf16, ch):
    n = ctx.shape[0]
    c, s = _cos_sin(n, n)
    f = (jnp.concatenate([jnp.asarray(c), -jnp.asarray(s)], axis=0) * (n ** -0.5)).astype(BF16)
    return pl.pallas_call(
        _ctx_fourier_body,
        out_shape=jax.ShapeDtypeStruct(ctx.shape, F32),
        compiler_params=pltpu.CompilerParams(vmem_limit_bytes=VMEM_LIMIT),
        name="ctx_fourier",
    )(ctx, mods, f, ch, w_bf16)


def _ffn_body(x_ref, mods_ref, wg_ref, wu_ref, wo_ref, o_ref, h_scr, acc):
    j = pl.program_id(1)

    @pl.when(j == 0)
    def _():
        h_scr[...] = _mod_norm(x_ref[...], mods_ref[...]).astype(BF16)

    h = h_scr[...]
    gate = _dot(h, wg_ref[...])
    up = _dot(h, wu_ref[...])
    act = (gate / (1.0 + jnp.exp(-gate)) * up).astype(BF16)
    part = _dot(act, wo_ref[...])

    @pl.when(j == 0)
    def _():
        acc[...] = part

    @pl.when(j != 0)
    def _():
        acc[...] += part

    @pl.when(j == pl.num_programs(1) - 1)
    def _():
        o_ref[...] = x_ref[...] + mods_ref[3:4, :] * acc[...]


def _ffn_layer(x, mods, wi_bf16, wo_bf16):
    n = x.shape[0]
    tm = min(ROW_TILE, n)
    nj = D_FF // FF_CHUNK
    return pl.pallas_call(
        _ffn_body,
        grid=(n // tm, nj),
        in_specs=[pl.BlockSpec((tm, D_MODEL), lambda i, j: (i, 0)),
                  pl.BlockSpec((SUBLANES, D_MODEL), lambda i, j: (0, 0)),
                  pl.BlockSpec((D_MODEL, FF_CHUNK), lambda i, j: (0, j)),
                  pl.BlockSpec((D_MODEL, FF_CHUNK), lambda i, j: (0, nj + j)),
                  pl.BlockSpec((FF_CHUNK, D_MODEL), lambda i, j: (j, 0))],
        out_specs=pl.BlockSpec((tm, D_MODEL), lambda i, j: (i, 0)),
        out_shape=jax.ShapeDtypeStruct(x.shape, F32),
        scratch_shapes=[pltpu.VMEM((tm, D_MODEL), BF16), pltpu.VMEM((tm, D_MODEL), F32)],
        compiler_params=_cparams(("arbitrary", "arbitrary"), VMEM_LIMIT),
        name="ffn",
    )(x, mods, wi_bf16, wi_bf16, wo_bf16)


def _qkv_body(x_ref, mods_ref, w_ref, bd_ref, gains_ref, cos_ref, sin_ref, q_ref, k_ref, v_ref):
    h = _mod_norm(x_ref[...], mods_ref[...]).astype(BF16)
    qkv = _dot(h, w_ref[...])
    bd = bd_ref[...]
    cos, sin = cos_ref[...], sin_ref[...]
    lane = lax.broadcasted_iota(jnp.int32, (1, LANES), 1)
    first_half = (lane % (HEAD_DIM // 2)) < (HEAD_DIM // 4)
    low_head = lane < HEAD_DIM

    def norm_rope(a, gain):
        sq = a * a
        hi = sq.astype(BF16)
        lo = (sq - hi.astype(F32)).astype(BF16)
        ms = _dot(hi, bd) + _dot(lo, bd)
        an = a * lax.rsqrt(ms + NORM_EPS) * gain
        partner = jnp.where(first_half,
                            pltpu.roll(an, LANES - HEAD_DIM // 4, 1),
                            pltpu.roll(an, HEAD_DIM // 4, 1))
        return an * cos + partner * sin

    def dup_heads(a):
        r = pltpu.roll(a, HEAD_DIM, 1)
        return jnp.where(low_head, a, r), jnp.where(low_head, r, a)

    nq = Q_COLS // LANES
    for t in range(nq):
        a = norm_rope(qkv[:, t * LANES:(t + 1) * LANES], gains_ref[0:1, :])
        q_ref[:, t * LANES:(t + 1) * LANES] = a.astype(BF16)
    for t in range(KV_COLS // LANES):
        kt = norm_rope(qkv[:, Q_COLS + t * LANES:Q_COLS + (t + 1) * LANES], gains_ref[1:2, :])
        k0, k1 = dup_heads(kt)
        k_ref[2 * t] = k0.astype(BF16)
        k_ref[2 * t + 1] = k1.astype(BF16)
        v0, v1 = dup_heads(qkv[:, Q_COLS + KV_COLS + t * LANES:Q_COLS + KV_COLS + (t + 1) * LANES])
        v_ref[2 * t] = v0.astype(BF16)
        v_ref[2 * t + 1] = v1.astype(BF16)


def _qkv_call(x, mods, wqkv_bf16, bd, gains, cos_t, sin_t):
    n = x.shape[0]
    tm = min(ROW_TILE, n)
    ncol = Q_COLS + 2 * KV_COLS
    return pl.pallas_call(
        _qkv_body,
        grid=(n // tm,),
        in_specs=[pl.BlockSpec((tm, D_MODEL), lambda i: (i, 0)),
                  pl.BlockSpec((SUBLANES, D_MODEL), lambda i: (0, 0)),
                  pl.BlockSpec((D_MODEL, ncol), lambda i: (0, 0)),
                  pl.BlockSpec((LANES, LANES), lambda i: (0, 0)),
                  pl.BlockSpec((SUBLANES, LANES), lambda i: (0, 0)),
                  pl.BlockSpec((tm, LANES), lambda i: (i, 0)),
                  pl.BlockSpec((tm, LANES), lambda i: (i, 0))],
        out_specs=[pl.BlockSpec((tm, Q_COLS), lambda i: (i, 0)),
                   pl.BlockSpec((N_KV_HEADS, tm, LANES), lambda i: (0, i, 0)),
                   pl.BlockSpec((N_KV_HEADS, tm, LANES), lambda i: (0, i, 0))],
        out_shape=[jax.ShapeDtypeStruct((n, Q_COLS), BF16),
                   jax.ShapeDtypeStruct((N_KV_HEADS, n, LANES), BF16),
                   jax.ShapeDtypeStruct((N_KV_HEADS, n, LANES), BF16)],
        compiler_params=_cparams(("arbitrary",), VMEM_LIMIT),
        name="qkv",
    )(x, mods, wqkv_bf16, bd, gains, cos_t, sin_t)


def _attn_body(sink_ref, q_ref, kp_ref, kc_ref, kn_ref, vp_ref, vc_ref, vn_ref,
               kx_ref, vx_ref, o_ref):
    b = pl.program_id(0)
    nb = pl.num_programs(0)
    blk = ATTN_BLOCK
    lane = lax.broadcasted_iota(jnp.int32, (1, LANES), 1)
    low_head = lane < HEAD_DIM
    qi = lax.broadcasted_iota(jnp.int32, (blk, 3 * blk), 0)
    kj = lax.broadcasted_iota(jnp.int32, (blk, 3 * blk), 1)
    valid = (kj >= qi) & (kj <= qi + 2 * WINDOW)
    valid = valid & ((kj >= blk) | (b > 0)) & ((kj < 2 * blk) | (b < nb - 1))
    per_kv = N_HEADS // N_KV_HEADS
    for g in range(N_KV_HEADS):
        kb = jnp.concatenate([kp_ref[g], kc_ref[g], kn_ref[g], kx_ref[g]], axis=0)
        vb = jnp.concatenate([vp_ref[g], vc_ref[g], vn_ref[g], vx_ref[g]], axis=0)
        parts = []
        for p in range(per_kv // 2):
            t = g * (per_kv // 2) + p
            qt = q_ref[:, t * LANES:(t + 1) * LANES]
            zero = jnp.zeros_like(qt)
            parts += [jnp.where(low_head, qt, zero), jnp.where(low_head, zero, qt)]
        lhs = jnp.concatenate(parts, axis=0)
        s = lax.dot_general(lhs, kb, (((1,), (1,)), ((), ())), preferred_element_type=F32)
        outs = []
        for hh in range(per_kv):
            sink = sink_ref[g * per_kv + hh]
            sh = s[hh * blk:(hh + 1) * blk]
            s_loc = jnp.where(valid, sh[:, :3 * blk], NEG_INF)
            s_ctx = sh[:, 3 * blk:]
            m = jnp.maximum(jnp.maximum(jnp.max(s_loc, axis=-1, keepdims=True),
                                        jnp.max(s_ctx, axis=-1, keepdims=True)), sink)
            p_loc = jnp.exp(s_loc - m)
            p_ctx = jnp.exp(s_ctx - m)
            den = (jnp.sum(p_loc, axis=-1, keepdims=True) + jnp.sum(p_ctx, axis=-1, keepdims=True)
                   + jnp.exp(sink - m))
            pr = jnp.concatenate([p_loc, p_ctx], axis=1).astype(BF16)
            outs.append(_dot(pr, vb) / den)
        for p in range(per_kv // 2):
            t = g * (per_kv // 2) + p
            o_ref[:, t * LANES:(t + 1) * LANES] = jnp.where(
                low_head, outs[2 * p], outs[2 * p + 1]).astype(BF16)


def _attn_call(sink, q, kd, vd, kx, vx):
    n = q.shape[0]
    blk = ATTN_BLOCK
    nb = n // blk
    nctx = kx.shape[1]
    kv_spec = lambda f: pl.BlockSpec((N_KV_HEADS, blk, LANES), f)
    prev = lambda b: (0, jnp.maximum(b - 1, 0), 0)
    cur = lambda b: (0, b, 0)
    nxt = lambda b: (0, jnp.minimum(b + 1, nb - 1), 0)
    ctx_spec = pl.BlockSpec((N_KV_HEADS, nctx, LANES), lambda b: (0, 0, 0))
    return pl.pallas_call(
        _attn_body,
        grid=(nb,),
        in_specs=[pl.BlockSpec(memory_space=pltpu.SMEM),
                  pl.BlockSpec((blk, Q_COLS), lambda b: (b, 0)),
                  kv_spec(prev), kv_spec(cur), kv_spec(nxt),
                  kv_spec(prev), kv_spec(cur), kv_spec(nxt),
                  ctx_spec, ctx_spec],
        out_specs=pl.BlockSpec((blk, Q_COLS), lambda b: (b, 0)),
        out_shape=jax.ShapeDtypeStruct((n, Q_COLS), BF16),
        compiler_params=_cparams(("arbitrary",), VMEM_LIMIT),
        name="attn",
    )(sink, q, kd, kd, kd, vd, vd, vd, kx, vx)


def _proj_body(a_ref, x_ref, mods_ref, w_ref, o_ref):
    o_ref[...] = x_ref[...] + mods_ref[3:4, :] * _dot(a_ref[...], w_ref[...])


def _proj_call(a, x, mods, w_bf16):
    n = x.shape[0]
    tm = min(ROW_TILE, n)
    return pl.pallas_call(
        _proj_body,
        grid=(n // tm,),
        in_specs=[pl.BlockSpec((tm, a.shape[1]), lambda i: (i, 0)),
                  pl.BlockSpec((tm, D_MODEL), lambda i: (i, 0)),
                  pl.BlockSpec((SUBLANES, D_MODEL), lambda i: (0, 0)),
                  pl.BlockSpec(w_bf16.shape, lambda i: (0, 0))],
        out_specs=pl.BlockSpec((tm, D_MODEL), lambda i: (i, 0)),
        out_shape=jax.ShapeDtypeStruct(x.shape, F32),
        compiler_params=_cparams(("arbitrary",), VMEM_LIMIT),
        name="proj",
    )(a, x, mods, w_bf16)


def _rope_lane_tables(length):
    rows = length // GRID_W
    row_pos = jnp.repeat(jnp.arange(rows, dtype=F32), GRID_W)
    col_pos = jnp.tile(jnp.arange(GRID_W, dtype=F32), rows)
    axis_dim = HEAD_DIM // 2
    inv_freq = ROPE_BASE ** (-jnp.arange(0, axis_dim, 2, dtype=F32) / axis_dim)
    ang_r = row_pos[:, None] * inv_freq[None, :]
    ang_c = col_pos[:, None] * inv_freq[None, :]
    cos_h = jnp.concatenate([jnp.cos(ang_r), jnp.cos(ang_r), jnp.cos(ang_c), jnp.cos(ang_c)], axis=1)
    sin_h = jnp.concatenate([-jnp.sin(ang_r), jnp.sin(ang_r), -jnp.sin(ang_c), jnp.sin(ang_c)], axis=1)
    reps = LANES // HEAD_DIM
    return jnp.tile(cos_h, (1, reps)), jnp.tile(sin_h, (1, reps))


def _attention_layer(x, ctx, mods_x, mods_c, wqkv, wo, q_gain, k_gain, sink):
    L = x.shape[0]
    nctx = ctx.shape[0]
    wqkv_b = wqkv.astype(BF16)
    head = np.arange(LANES) // HEAD_DIM
    bd = jnp.asarray((head[:, None] == head[None, :]).astype(np.float32) / HEAD_DIM).astype(BF16)
    reps = LANES // HEAD_DIM
    gains = jnp.zeros((SUBLANES, LANES), F32)
    gains = gains.at[0].set(jnp.tile(q_gain, reps) * (HEAD_DIM ** -0.5))
    gains = gains.at[1].set(jnp.tile(k_gain, reps))
    cos_t, sin_t = _rope_lane_tables(L)
    q, kd, vd = _qkv_call(x, mods_x, wqkv_b, bd, gains, cos_t, sin_t)
    ones = jnp.ones((nctx, LANES), F32)
    _, kx, vx = _qkv_call(ctx, mods_c, wqkv_b, bd, gains, ones, jnp.zeros_like(ones))
    o = _attn_call(sink, q, kd, vd, kx, vx)
    return _proj_call(o, x, mods_x, wo.astype(BF16))


def _pool_body(xp_ref, xc_ref, xn_ref, mods_ref, w_ref, o_ref, h_scr):
    i = pl.program_id(0)
    tm = xc_ref.shape[0]
    total = tm * pl.num_programs(0)
    x = xc_ref[...]
    mods = mods_ref[...]
    h_scr[0:POOL_HALO, :] = jnp.where(i > 0, _mod_norm(xp_ref[...], mods), 0.0)
    h_scr[POOL_HALO:POOL_HALO + tm, :] = _mod_norm(x, mods)
    h_scr[POOL_HALO + tm:, :] = jnp.where(i < pl.num_programs(0) - 1, _mod_norm(xn_ref[...], mods), 0.0)
    t = i * tm + lax.broadcasted_iota(jnp.int32, (tm, 1), 0)
    gd = POOL_GROUP_DIM
    for g, win in enumerate(POOL_WINDOWS):
        sl = slice(g * gd, (g + 1) * gd)
        half = win // 2
        tot = h_scr[POOL_HALO - half:POOL_HALO - half + tm, sl]
        for s in range(-half + 1, half):
            tot = tot + h_scr[POOL_HALO + s:POOL_HALO + s + tm, sl]
        lo = jnp.maximum(t - half, 0)
        hi = jnp.minimum(t + half - 1, total - 1)
        cnt = (hi - lo + 1).astype(F32)
        pooled = (tot / cnt - h_scr[POOL_HALO:POOL_HALO + tm, sl]).astype(BF16)
        y = _dot(pooled, w_ref[g]) * mods[4:5, sl]
        o_ref[:, sl] = x[:, sl] + mods[3:4, sl] * y


def _pool_layer(x, mods, w_bf16):
    n = x.shape[0]
    tm = min(ROW_TILE, n)
    r = tm // POOL_HALO
    last = n // POOL_HALO - 1
    return pl.pallas_call(
        _pool_body,
        grid=(n // tm,),
        in_specs=[pl.BlockSpec((POOL_HALO, D_MODEL), lambda i: (jnp.maximum(i * r - 1, 0), 0)),
                  pl.BlockSpec((tm, D_MODEL), lambda i: (i, 0)),
                  pl.BlockSpec((POOL_HALO, D_MODEL), lambda i: (jnp.minimum((i + 1) * r, last), 0)),
                  pl.BlockSpec((SUBLANES, D_MODEL), lambda i: (0, 0)),
                  pl.BlockSpec(w_bf16.shape, lambda i: (0, 0, 0))],
        out_specs=pl.BlockSpec((tm, D_MODEL), lambda i: (i, 0)),
        out_shape=jax.ShapeDtypeStruct(x.shape, F32),
        scratch_shapes=[pltpu.VMEM((tm + 2 * POOL_HALO, D_MODEL), F32)],
        compiler_params=_cparams(("arbitrary",), VMEM_LIMIT),
        name="pool",
    )(x, x, x, mods, w_bf16)


def _router_body(x_ref, mods_ref, r_ref, h_ref, route_ref, cnt_ref):
    h = _mod_norm(x_ref[...], mods_ref[...])
    h_ref[...] = h.astype(BF16)
    logits = jnp.dot(h, r_ref[...], precision=lax.Precision.HIGHEST, preferred_element_type=F32)
    lane = lax.broadcasted_iota(jnp.int32, logits.shape, 1)
    logits = jnp.where(lane < N_EXPERTS, logits, -jnp.inf)
    m1 = jnp.max(logits, axis=-1, keepdims=True)
    i1 = jnp.min(jnp.where(logits == m1, lane, LANES), axis=-1, keepdims=True)
    rest = jnp.where(lane == i1, -jnp.inf, logits)
    m2 = jnp.max(rest, axis=-1, keepdims=True)
    i2 = jnp.min(jnp.where(rest == m2, lane, LANES), axis=-1, keepdims=True)
    e = jnp.exp(m2 - m1)
    w1 = 1.0 / (1.0 + e)
    w2 = e / (1.0 + e)
    tm = h.shape[0]
    oh = jnp.where((lane == i1) | (lane == i2 + N_EXPERTS), 1.0, 0.0)
    before = (lax.broadcasted_iota(jnp.int32, (tm, tm), 1)
              < lax.broadcasted_iota(jnp.int32, (tm, tm), 0))
    prior = _dot(jnp.where(before, 1.0, 0.0).astype(BF16), oh.astype(BF16))
    cnt = jnp.sum(oh, axis=0, keepdims=True)
    r1 = jnp.sum(jnp.where(lane == i1, prior, 0.0), axis=-1, keepdims=True)
    r2 = (jnp.sum(jnp.where(lane == i2 + N_EXPERTS, prior, 0.0), axis=-1, keepdims=True)
          + jnp.sum(jnp.where(lane == i2, cnt, 0.0), axis=-1, keepdims=True))
    col = lax.broadcasted_iota(jnp.int32, route_ref.shape, 1)
    vals = (i1.astype(F32), i2.astype(F32), w1, w2, r1, r2)
    out = jnp.zeros(route_ref.shape, F32)
    for k, v in enumerate(vals):
        out = jnp.where(col == k, v, out)
    route_ref[...] = out
    both = cnt + pltpu.roll(cnt, LANES - N_EXPERTS, 1)
    cnt_ref[0] = jnp.broadcast_to(both, cnt_ref.shape[1:])


def _router_call(x, mods, router_pad):
    n = x.shape[0]
    tm = min(ROUTE_TILE, n)
    return pl.pallas_call(
        _router_body,
        grid=(n // tm,),
        in_specs=[pl.BlockSpec((tm, D_MODEL), lambda i: (i, 0)),
                  pl.BlockSpec((SUBLANES, D_MODEL), lambda i: (0, 0)),
                  pl.BlockSpec((D_MODEL, LANES), lambda i: (0, 0))],
        out_specs=[pl.BlockSpec((tm, D_MODEL), lambda i: (i, 0)),
                   pl.BlockSpec((tm, SUBLANES), lambda i: (i, 0)),
                   pl.BlockSpec((1, SUBLANES, LANES), lambda i: (i, 0, 0))],
        out_shape=[jax.ShapeDtypeStruct((n, D_MODEL), BF16),
                   jax.ShapeDtypeStruct((n, SUBLANES), F32),
                   jax.ShapeDtypeStruct((n // tm, SUBLANES, LANES), F32)],
        compiler_params=_cparams(("arbitrary",), VMEM_LIMIT),
        name="router",
    )(x, mods, router_pad)


def _segment_copies(i, seg_ref, dst_ref, len_ref, tile_buf, sorted_hbm, sem, to_hbm):
    for e in range(N_EXPERTS):
        length = len_ref[i * N_EXPERTS + e]
        seg = seg_ref[i * N_EXPERTS + e]
        dst = dst_ref[i * N_EXPERTS + e]
        size = SEG_ALIGN
        while size <= tile_buf.shape[0] // 2:
            done = length & ~(2 * size - 1)
            a = tile_buf.at[pl.ds(pl.multiple_of(seg + done, SEG_ALIGN), size)]
            b = sorted_hbm.at[pl.ds(pl.multiple_of(dst + done, SEG_ALIGN), size)]
            copy = pltpu.make_async_copy(a, b, sem) if to_hbm else pltpu.make_async_copy(b, a, sem)
            yield (length & size) != 0, copy
            size *= 2


def _scatter_body(seg_ref, dst_ref, len_ref, h_ref, prow_ref, xs_in_ref, xs_ref, sbuf, sem):
    del xs_in_ref
    i = pl.program_id(0)
    cap, tm = sbuf.shape[0], h_ref.shape[0]
    r = lax.broadcasted_iota(jnp.int32, (cap, tm), 0)
    hit = (r == prow_ref[0:1, :]) | (r == prow_ref[1:2, :])
    sel = jnp.where(hit, 1.0, 0.0).astype(BF16)
    sbuf[...] = _dot(sel, h_ref[...]).astype(BF16)
    for pred, copy in _segment_copies(i, seg_ref, dst_ref, len_ref, sbuf, xs_ref, sem, True):
        pl.when(pred)(copy.start)
    for pred, copy in _segment_copies(i, seg_ref, dst_ref, len_ref, sbuf, xs_ref, sem, True):
        pl.when(pred)(copy.wait)


def _route_cap(tm):
    return 2 * tm + N_EXPERTS * SEG_ALIGN


def _scatter_call(seg, dst, seg_len, h, prow, n_rows):
    n = h.shape[0]
    tm = min(ROUTE_TILE, n)
    zeros = jnp.zeros((n_rows, D_MODEL), BF16)
    return pl.pallas_call(
        _scatter_body,
        grid_spec=pltpu.PrefetchScalarGridSpec(
            num_scalar_prefetch=3,
            grid=(n // tm,),
            in_specs=[pl.BlockSpec((tm, D_MODEL), lambda i, *_: (i, 0)),
                      pl.BlockSpec((SUBLANES, tm), lambda i, *_: (i, 0)),
                      pl.BlockSpec(memory_space=pl.ANY)],
            out_specs=pl.BlockSpec(memory_space=pl.ANY),
            scratch_shapes=[pltpu.VMEM((_route_cap(tm), D_MODEL), BF16),
                            pltpu.SemaphoreType.DMA(())]),
        out_shape=jax.ShapeDtypeStruct((n_rows, D_MODEL), BF16),
        input_output_aliases={5: 0},
        compiler_params=_cparams(("arbitrary",), VMEM_LIMIT),
        name="moe_scatter",
    )(seg, dst, seg_len, h, prow, zeros)


def _moe_body(te_ref, tv_ref, xs_ref, wg_ref, wu_ref, wo_ref, ys_ref, act):
    i = pl.program_id(0)
    j = pl.program_id(1)
    nj = act.shape[0]
    live = tv_ref[i] > 0

    @pl.when(live & (j < nj))
    def _():
        h = xs_ref[...]
        gate = _dot(h, wg_ref[...])
        up = _dot(h, wu_ref[...])
        act[j] = (gate / (1.0 + jnp.exp(-gate)) * up).astype(BF16)

    @pl.when(live & (j == nj))
    def _():
        a = jnp.concatenate([act[k] for k in range(nj)], axis=1)
        ys_ref[...] = _dot(a, wo_ref[...]).astype(BF16)

    @pl.when(jnp.logical_not(live) & (j == nj))
    def _():
        ys_ref[...] = jnp.zeros(ys_ref.shape, BF16)


def _moe_call(tile_expert, tile_live, xs, wi_bf16, wo_bf16):
    n_rows = xs.shape[0]
    nt = n_rows // MOE_TILE
    nj = D_FF // FF_CHUNK

    def jj(j):
        return jnp.minimum(j, nj - 1)

    return pl.pallas_call(
        _moe_body,
        grid_spec=pltpu.PrefetchScalarGridSpec(
            num_scalar_prefetch=2,
            grid=(nt, nj + 1),
            in_specs=[pl.BlockSpec((MOE_TILE, D_MODEL), lambda i, j, te, tv: (i, 0)),
                      pl.BlockSpec((None, D_MODEL, FF_CHUNK),
                                   lambda i, j, te, tv: (te[i], 0, jnp.where(tv[i] > 0, jj(j), nj - 1))),
                      pl.BlockSpec((None, D_MODEL, FF_CHUNK),
                                   lambda i, j, te, tv: (te[i], 0, nj + jnp.where(tv[i] > 0, jj(j), nj - 1))),
                      pl.BlockSpec((None, D_FF, D_MODEL), lambda i, j, te, tv: (te[i], 0, 0))],
            out_specs=pl.BlockSpec((MOE_TILE, D_MODEL), lambda i, j, te, tv: (i, 0)),
            scratch_shapes=[pltpu.VMEM((nj, MOE_TILE, FF_CHUNK), BF16)]),
        out_shape=jax.ShapeDtypeStruct((n_rows, D_MODEL), BF16),
        compiler_params=_cparams(("arbitrary", "arbitrary"), VMEM_LIMIT),
        name="moe_ffn",
    )(tile_expert, tile_live, xs, wi_bf16, wi_bf16, wo_bf16)


def _combine_body(seg_ref, dst_ref, len_ref, x_ref, route_ref, pcol_ref, mods_ref, ys_ref, o_ref,
                  ybuf, sem):
    i = pl.program_id(0)
    tm, cap = x_ref.shape[0], ybuf.shape[0]

    @pl.when(i == 0)
    def _():
        ybuf[...] = jnp.zeros(ybuf.shape, BF16)

    for pred, copy in _segment_copies(i, seg_ref, dst_ref, len_ref, ybuf, ys_ref, sem, False):
        pl.when(pred)(copy.start)
    for pred, copy in _segment_copies(i, seg_ref, dst_ref, len_ref, ybuf, ys_ref, sem, False):
        pl.when(pred)(copy.wait)
    c = lax.broadcasted_iota(jnp.int32, (tm, cap), 1)
    pcol = pcol_ref[...]
    y = ybuf[...]
    ya = _dot(jnp.where(c == pcol[:, 0:1], 1.0, 0.0).astype(BF16), y)
    yb = _dot(jnp.where(c == pcol[:, 1:2], 1.0, 0.0).astype(BF16), y)
    route = route_ref[...]
    o_ref[...] = x_ref[...] + mods_ref[3:4, :] * (route[:, 2:3] * ya + route[:, 3:4] * yb)


def _combine_call(seg, dst, seg_len, x, route, pcol, mods, ys):
    n = x.shape[0]
    tm = min(ROUTE_TILE, n)
    return pl.pallas_call(
        _combine_body,
        grid_spec=pltpu.PrefetchScalarGridSpec(
            num_scalar_prefetch=3,
            grid=(n // tm,),
            in_specs=[pl.BlockSpec((tm, D_MODEL), lambda i, *_: (i, 0)),
                      pl.BlockSpec((tm, SUBLANES), lambda i, *_: (i, 0)),
                      pl.BlockSpec((tm, SUBLANES), lambda i, *_: (i, 0)),
                      pl.BlockSpec((SUBLANES, D_MODEL), lambda i, *_: (0, 0)),
                      pl.BlockSpec(memory_space=pl.ANY)],
            out_specs=pl.BlockSpec((tm, D_MODEL), lambda i, *_: (i, 0)),
            scratch_shapes=[pltpu.VMEM((_route_cap(tm), D_MODEL), BF16),
                            pltpu.SemaphoreType.DMA(())]),
        out_shape=jax.ShapeDtypeStruct(x.shape, F32),
        compiler_params=_cparams(("arbitrary",), VMEM_LIMIT),
        name="moe_combine",
    )(seg, dst, seg_len, x, route, pcol, mods, ys)


def _routing_tables(route, counts, tm, n_tiles):
    nt = counts.shape[0]
    cnt = counts[:, 0, :N_EXPERTS].astype(jnp.int32)
    seg_len = (cnt + SEG_ALIGN - 1) // SEG_ALIGN * SEG_ALIGN
    seg = jnp.cumsum(seg_len, axis=1) - seg_len
    tiles = (jnp.sum(seg_len, axis=0) + MOE_TILE - 1) // MOE_TILE
    tile_end = jnp.cumsum(tiles)
    start = (tile_end - tiles) * MOE_TILE
    dst = start[None, :] + jnp.cumsum(seg_len, axis=0) - seg_len
    t = jnp.arange(n_tiles, dtype=jnp.int32)
    expert = jnp.sum((t[:, None] >= tile_end[None, :]).astype(jnp.int32), axis=1)
    live = (t < tile_end[-1]).astype(jnp.int32)
    last_expert = jnp.sum((tile_end[-1] - 1 >= tile_end).astype(jnp.int32))
    expert = jnp.where(live > 0, expert, last_expert).astype(jnp.int32)
    e12 = route[:, 0:2].astype(jnp.int32)
    tile_of = (jnp.arange(route.shape[0], dtype=jnp.int32) // tm)[:, None]
    p12 = jnp.take(seg.reshape(-1), tile_of * N_EXPERTS + e12) + route[:, 4:6].astype(jnp.int32)
    pcol = jnp.concatenate([p12, jnp.zeros((p12.shape[0], SUBLANES - 2), jnp.int32)], axis=1)
    prow = jnp.full((nt, SUBLANES, tm), -1, jnp.int32)
    prow = prow.at[:, 0:2, :].set(p12.reshape(nt, tm, 2).transpose(0, 2, 1)).reshape(nt * SUBLANES, tm)
    return (seg.reshape(-1).astype(jnp.int32), dst.reshape(-1).astype(jnp.int32),
            seg_len.reshape(-1).astype(jnp.int32), expert, live, pcol, prow)


def _moe_layer(x, mods, router, wi, wo):
    n = x.shape[0]
    tm = min(ROUTE_TILE, n)
    router_pad = jnp.zeros((D_MODEL, LANES), F32).at[:, :N_EXPERTS].set(router)
    h, route, counts = _router_call(x, mods, router_pad)
    max_rows = 2 * n + (n // tm) * N_EXPERTS * (SEG_ALIGN - 1)
    n_tiles = -(-max_rows // MOE_TILE) + N_EXPERTS
    seg, dst, seg_len, tile_expert, tile_live, pcol, prow = _routing_tables(route, counts, tm, n_tiles)
    xs = _scatter_call(seg, dst, seg_len, h, prow, n_tiles * MOE_TILE)
    ys = _moe_call(tile_expert, tile_live, xs, wi.astype(BF16), wo.astype(BF16))
    return _combine_call(seg, dst, seg_len, x, route, pcol, mods, ys)


def _mods(gain, shift, scale, gate, extra=None):
    rows = [gain, shift, scale, gate, extra if extra is not None else jnp.zeros_like(gain)]
    m = jnp.stack(rows, axis=0)
    return jnp.concatenate([m, jnp.zeros((SUBLANES - m.shape[0], m.shape[1]), F32)], axis=0)


def kernel(x, c, ctx, c_ctx, ada_w, ada_b, norm_mix, norm_ffn, fnet_w, attn_wqkv, attn_q_gain,
           attn_k_gain, attn_sink, attn_wo, pool_w, pool_scale, ffn_wi, ffn_wo, moe_router,
           moe_wi, moe_wo):
    assert x.shape[0] == 1 and x.shape[2] == D_MODEL
    depth = ada_w.shape[0]
    xs = x[0]
    cs = ctx[0]
    cc = jnp.zeros((SUBLANES, D_MODEL), F32).at[0].set(c[0]).at[1].set(c_ctx)
    ada = _ada_call(cc, ada_w, ada_b)

    attn_layers = [i for i in range(depth) if i % N_MIXERS == 1]
    last_ctx_read = attn_layers[-1] if attn_layers else -1
    n_side = math.isqrt(xs.shape[0])
    assert n_side * n_side == xs.shape[0] and n_side % SUBLANES == 0
    seq_tables = _sequence_tables(n_side)
    ch = _channel_table()

    for i in range(depth):
        mixer = i % N_MIXERS
        j = i // N_MIXERS
        f = i // 2
        ctx_full = i < last_ctx_read
        ctx_live = i <= last_ctx_read
        sh1, sc1, g1, sh2, sc2, g2 = [ada[i, 0, k * D_MODEL:(k + 1) * D_MODEL] for k in range(6)]
        csh1, csc1, cg1, csh2, csc2, cg2 = [ada[i, 1, k * D_MODEL:(k + 1) * D_MODEL] for k in range(6)]
        extra = pool_scale[j] if mixer == 2 else None
        m1 = _mods(norm_mix[i], sh1, sc1, g1, extra)
        m2 = _mods(norm_ffn[i], sh2, sc2, g2)
        cm1 = _mods(norm_mix[i], csh1, csc1, cg1, extra)
        cm2 = _mods(norm_ffn[i], csh2, csc2, cg2)

        if mixer == 0:
            wb = fnet_w[j].astype(BF16)
            xs = _fourier_layer(xs, m1, wb, seq_tables, ch)
            if ctx_full:
                cs = _ctx_fourier_layer(cs, cm1, wb, ch)
        elif mixer == 1:
            assert ctx_live and not ctx_full
            xs = _attention_layer(xs, cs, m1, cm1, attn_wqkv[j], attn_wo[j], attn_q_gain[j],
                                  attn_k_gain[j], attn_sink[j])
        else:
            wb = pool_w[j].astype(BF16)
            xs = _pool_layer(xs, m1, wb)
            if ctx_full:
                cs = _pool_layer(cs, cm1, wb)

        if i % 2 == 0:
            wi_b, wo_b = ffn_wi[f].astype(BF16), ffn_wo[f].astype(BF16)
            xs = _ffn_layer(xs, m2, wi_b, wo_b)
            if ctx_full:
                cs = _ffn_layer(cs, cm2, wi_b, wo_b)
        else:
            xs = _moe_layer(xs, m2, moe_router[f], moe_wi[f], moe_wo[f])
            if ctx_full:
                cs = _moe_layer(cs, cm2, moe_router[f], moe_wi[f], moe_wo[f])
    return xs[None]
```

```python
import functools
import math

import numpy as np
import jax
import jax.numpy as jnp
from jax import lax
from jax.experimental import pallas as pl
from jax.experimental.pallas import tpu as pltpu

F32 = jnp.float32
BF16 = jnp.bfloat16

D_MODEL = 1024
GRID_W = 64
N_MIXERS = 3
FNET_GROUPS = 4
FNET_GROUP_DIM = D_MODEL // FNET_GROUPS
N_HEADS = 16
N_KV_HEADS = 4
HEAD_DIM = 64
Q_COLS = N_HEADS * HEAD_DIM
KV_COLS = N_KV_HEADS * HEAD_DIM
WINDOW = 128
ATTN_BLOCK = 128
ROPE_BASE = 10000.0
POOL_WINDOWS = (2, 4, 8, 16)
POOL_GROUP_DIM = D_MODEL // len(POOL_WINDOWS)
POOL_HALO = 8
D_FF = 3584
N_EXPERTS = 8
NORM_EPS = 1e-6
NEG_INF = -1e30

LANES = 128
SUBLANES = 8
VMEM_LIMIT = 56 * 1024 * 1024

ROW_TILE = 512
FF_CHUNK = 512
MOE_TILE = 256
W_CHUNK = 512
ROUTE_TILE = 512
SEG_ALIGN = 16


def _cparams(sem, vmem=None):
    return pltpu.CompilerParams(dimension_semantics=sem, vmem_limit_bytes=vmem)


def _mod_norm(x, mods):
    ms = jnp.mean(x * x, axis=-1, keepdims=True)
    y = x * lax.rsqrt(ms + NORM_EPS) * mods[0:1, :]
    return y * (1.0 + mods[2:3, :]) + mods[1:2, :]


def _dot(a, b):
    return jnp.dot(a, b, preferred_element_type=F32)


def _ada_body(cc_ref, w_ref, b_ref, o_ref):
    a = cc_ref[...]
    a = a / (1.0 + jnp.exp(-a))
    o_ref[0] = jnp.dot(a, w_ref[0], precision=lax.Precision.HIGHEST,
                       preferred_element_type=F32) + b_ref[0]


def _ada_call(cc, ada_w, ada_b):
    depth, d, n6 = ada_w.shape
    tn = n6 // 4
    return pl.pallas_call(
        _ada_body,
        grid=(depth, n6 // tn),
        in_specs=[pl.BlockSpec((SUBLANES, d), lambda l, j: (0, 0)),
                  pl.BlockSpec((1, d, tn), lambda l, j: (l, 0, j)),
                  pl.BlockSpec((1, 1, tn), lambda l, j: (l, 0, j))],
        out_specs=pl.BlockSpec((1, SUBLANES, tn), lambda l, j: (l, 0, j)),
        out_shape=jax.ShapeDtypeStruct((depth, SUBLANES, n6), F32),
        compiler_params=_cparams(("arbitrary", "arbitrary"), VMEM_LIMIT),
        name="ada",
    )(cc, ada_w, ada_b.reshape(depth, 1, n6))


def _cos_sin(n, period):
    k = np.arange(n, dtype=np.float64)
    ang = 2.0 * np.pi * np.outer(k, k) / period
    return np.cos(ang).astype(np.float32), np.sin(ang).astype(np.float32)


def _channel_table():
    c, s = _cos_sin(FNET_GROUP_DIM, FNET_GROUP_DIM)
    return (jnp.concatenate([jnp.asarray(c), jnp.asarray(s)], axis=0)
            * (FNET_GROUP_DIM ** -0.5)).astype(BF16)


def _sequence_tables(n):
    c, s = _cos_sin(n, n)
    eye = np.eye(SUBLANES, dtype=np.float32)
    rows = n * SUBLANES
    f = np.stack([c, -s]) * np.float32(1.0 / n)
    ka = (f[:, None, :, :, None] * eye[None, :, None, None, :]).reshape(2 * rows, rows)
    ck = (c[:, None, :, None] * eye[None, :, None, :]).reshape(rows, rows)
    sk = (s[:, None, :, None] * eye[None, :, None, :]).reshape(rows, rows)
    cs = np.concatenate([ck, sk], axis=1)
    tc, ts = _cos_sin(n, n * n)
    oct_ = n // SUBLANES
    tc = np.ascontiguousarray(np.broadcast_to(tc.reshape(oct_, rows, 1), (oct_, rows, LANES)))
    ts = np.ascontiguousarray(np.broadcast_to(ts.reshape(oct_, rows, 1), (oct_, rows, LANES)))
    return jnp.asarray(ka).astype(BF16), jnp.asarray(cs).astype(BF16), jnp.asarray(tc), jnp.asarray(ts)


def _fft_a_body(x_ref, mods_ref, ka_ref, tc_ref, ts_ref, zr_ref, zi_ref, h_scr):
    c = pl.program_id(1)
    rows = h_scr.shape[1]
    w = zr_ref.shape[-1]

    @pl.when(c == 0)
    def _():
        x = x_ref[...].reshape(rows, D_MODEL)
        h = _mod_norm(x, mods_ref[...]).astype(BF16)
        for k in range(D_MODEL // w):
            h_scr[k] = h[:, k * w:(k + 1) * w]

    hc = h_scr[c]
    z = _dot(ka_ref[...], hc)
    zr, zi = z[:rows], z[rows:]
    tc, ts = tc_ref[0], ts_ref[0]
    for t in range(w // LANES):
        sl = slice(t * LANES, (t + 1) * LANES)
        a, b = zr[:, sl], zi[:, sl]
        zr_ref[:, :, sl] = (a * tc + b * ts).reshape(SUBLANES, rows // SUBLANES, LANES)
        zi_ref[:, :, sl] = (b * tc - a * ts).reshape(SUBLANES, rows // SUBLANES, LANES)


def _fft_b_body(zr_ref, zi_ref, x_ref, mods_ref, cs_ref, ch_ref, w_ref, o_ref, acc):
    g = pl.program_id(1)
    rows = acc.shape[0]
    gw = zr_ref.shape[-1]
    zr = zr_ref[...].reshape(rows, gw).astype(BF16)
    zi = zi_ref[...].reshape(rows, gw).astype(BF16)
    cs = cs_ref[...]
    xr = _dot(cs, jnp.concatenate([zr, zi], axis=0))
    xi = _dot(cs, jnp.concatenate([zi, -zr], axis=0))
    y = _dot(jnp.concatenate([xr, xi], axis=1).astype(BF16), ch_ref[...])
    part = _dot(y.astype(BF16), w_ref[...])

    @pl.when(g == 0)
    def _():
        acc[...] = part

    @pl.when(g != 0)
    def _():
        acc[...] += part

    @pl.when(g == pl.num_programs(1) - 1)
    def _():
        x = x_ref[...].reshape(rows, D_MODEL)
        o_ref[...] = (x + mods_ref[3:4, :] * acc[...]).reshape(o_ref.shape)


def _fourier_layer(x, mods, w_bf16, seq_tables, ch):
    L = x.shape[0]
    n = math.isqrt(L)
    rows = n * SUBLANES
    oct_ = n // SUBLANES
    ka, cs, tc, ts = seq_tables
    x3 = x.reshape(n, n, D_MODEL)
    gw = FNET_GROUP_DIM
    ng = D_MODEL // gw
    zshape = jax.ShapeDtypeStruct((n, n, D_MODEL), F32)
    zr, zi = pl.pallas_call(
        _fft_a_body,
        grid=(oct_, ng),
        in_specs=[pl.BlockSpec((n, SUBLANES, D_MODEL), lambda o, c: (0, o, 0)),
                  pl.BlockSpec((SUBLANES, D_MODEL), lambda o, c: (0, 0)),
                  pl.BlockSpec((2 * rows, rows), lambda o, c: (0, 0)),
                  pl.BlockSpec((1, rows, LANES), lambda o, c: (o, 0, 0)),
                  pl.BlockSpec((1, rows, LANES), lambda o, c: (o, 0, 0))],
        out_specs=[pl.BlockSpec((SUBLANES, n, gw), lambda o, c: (o, 0, c)),
                   pl.BlockSpec((SUBLANES, n, gw), lambda o, c: (o, 0, c))],
        out_shape=[zshape, zshape],
        scratch_shapes=[pltpu.VMEM((ng, rows, gw), BF16)],
        compiler_params=_cparams(("arbitrary", "arbitrary"), VMEM_LIMIT),
        name="fft_a",
    )(x3, mods, ka, tc, ts)
    out = pl.pallas_call(
        _fft_b_body,
        grid=(oct_, ng),
        in_specs=[pl.BlockSpec((n, SUBLANES, gw), lambda p, g: (0, p, g)),
                  pl.BlockSpec((n, SUBLANES, gw), lambda p, g: (0, p, g)),
                  pl.BlockSpec((n, SUBLANES, D_MODEL), lambda p, g: (0, p, 0)),
                  pl.BlockSpec((SUBLANES, D_MODEL), lambda p, g: (0, 0)),
                  pl.BlockSpec((rows, 2 * rows), lambda p, g: (0, 0)),
                  pl.BlockSpec((2 * gw, gw), lambda p, g: (0, 0)),
                  pl.BlockSpec((gw, D_MODEL), lambda p, g: (g, 0))],
        out_specs=pl.BlockSpec((n, SUBLANES, D_MODEL), lambda p, g: (0, p, 0)),
        out_shape=jax.ShapeDtypeStruct((n, n, D_MODEL), F32),
        scratch_shapes=[pltpu.VMEM((rows, D_MODEL), F32)],
        compiler_params=_cparams(("arbitrary", "arbitrary"), VMEM_LIMIT),
        name="fft_b",
    )(zr, zi, x3, mods, cs, ch, w_bf16)
    return out.reshape(L, D_MODEL)


def _ctx_fourier_body(x_ref, mods_ref, f_ref, ch_ref, w_ref, o_ref):
    x = x_ref[...]
    n = x.shape[0]
    h = _mod_norm(x, mods_ref[...]).astype(BF16)
    g = _dot(f_ref[...], h)
    gr, gi = g[:n], g[n:]
    gw = FNET_GROUP_DIM
    ys = []
    for k in range(D_MODEL // gw):
        sl = slice(k * gw, (k + 1) * gw)
        ys.append(_dot(jnp.concatenate([gr[:, sl], gi[:, sl]], axis=1).astype(BF16), ch_ref[...]))
    y = jnp.concatenate(ys, axis=1).astype(BF16)
    o_ref[...] = x + mods_ref[3:4, :] * _dot(y, w_ref[...])


def _ctx_fourier_layer(ctx, mods, w_bf16, ch):
    n = ctx.shape[0]
    c, s = _cos_sin(n, n)
    f = (jnp.concatenate([jnp.asarray(c), -jnp.asarray(s)], axis=0) * (n ** -0.5)).astype(BF16)
    return pl.pallas_call(
        _ctx_fourier_body,
        out_shape=jax.ShapeDtypeStruct(ctx.shape, F32),
        compiler_params=pltpu.CompilerParams(vmem_limit_bytes=VMEM_LIMIT),
        name="ctx_fourier",
    )(ctx, mods, f, ch, w_bf16)


def _load_swiglu_weights(e, wi_hbm, wo_hbm, wi_res, wo_res, stage_i, stage_o, sems):
    n_i = wi_res.shape[1] // W_CHUNK
    total = n_i + wo_res.shape[0] // W_CHUNK

    def copy(c):
        slot = c % 2
        if c < n_i:
            src = wi_hbm.at[e, :, pl.ds(c * W_CHUNK, W_CHUNK)]
            return pltpu.make_async_copy(src, stage_i.at[slot], sems.at[slot])
        src = wo_hbm.at[e, pl.ds((c - n_i) * W_CHUNK, W_CHUNK), :]
        return pltpu.make_async_copy(src, stage_o.at[slot], sems.at[slot])

    copy(0).start()
    for c in range(total):
        if c + 1 < total:
            copy(c + 1).start()
        copy(c).wait()
        if c < n_i:
            wi_res[:, c * W_CHUNK:(c + 1) * W_CHUNK] = stage_i[c % 2].astype(BF16)
        else:
            k = c - n_i
            wo_res[k * W_CHUNK:(k + 1) * W_CHUNK, :] = stage_o[c % 2].astype(BF16)


def _swiglu_tile(h, wi_res, wo_res, act):
    for k in range(D_FF // FF_CHUNK):
        gate = _dot(h, wi_res[:, k * FF_CHUNK:(k + 1) * FF_CHUNK])
        up = _dot(h, wi_res[:, D_FF + k * FF_CHUNK:D_FF + (k + 1) * FF_CHUNK])
        act[:, k * FF_CHUNK:(k + 1) * FF_CHUNK] = (gate / (1.0 + jnp.exp(-gate)) * up).astype(BF16)
    return _dot(act[...], wo_res[...])


def _swiglu_scratch(tm):
    return [pltpu.VMEM((D_MODEL, 2 * D_FF), BF16), pltpu.VMEM((D_FF, D_MODEL), BF16),
            pltpu.VMEM((2, D_MODEL, W_CHUNK), F32), pltpu.VMEM((2, W_CHUNK, D_MODEL), F32),
            pltpu.VMEM((tm, D_FF), BF16), pltpu.SemaphoreType.DMA((2,))]


def _ffn_body(layer, x_ref, mods_ref, wi_hbm, wo_hbm, o_ref, wi_res, wo_res, stage_i, stage_o,
              act, sems):
    @pl.when(pl.program_id(0) == 0)
    def _():
        _load_swiglu_weights(layer, wi_hbm, wo_hbm, wi_res, wo_res, stage_i, stage_o, sems)

    x = x_ref[...]
    h = _mod_norm(x, mods_ref[...]).astype(BF16)
    o_ref[...] = x + mods_ref[3:4, :] * _swiglu_tile(h, wi_res, wo_res, act)


def _ffn_layer(x, mods, wi_all, wo_all, layer):
    n = x.shape[0]
    tm = min(ROW_TILE, n)
    return pl.pallas_call(
        functools.partial(_ffn_body, layer),
        grid=(n // tm,),
        in_specs=[pl.BlockSpec((tm, D_MODEL), lambda i: (i, 0)),
                  pl.BlockSpec((SUBLANES, D_MODEL), lambda i: (0, 0)),
                  pl.BlockSpec(memory_space=pl.ANY),
                  pl.BlockSpec(memory_space=pl.ANY)],
        out_specs=pl.BlockSpec((tm, D_MODEL), lambda i: (i, 0)),
        out_shape=jax.ShapeDtypeStruct(x.shape, F32),
        scratch_shapes=_swiglu_scratch(tm),
        compiler_params=_cparams(("arbitrary",), VMEM_LIMIT),
        name="ffn",
    )(x, mods, wi_all, wo_all)


def _qkv_body(x_ref, mods_ref, w_ref, bd_ref, gains_ref, cos_ref, sin_ref, q_ref, k_ref, v_ref):
    h = _mod_norm(x_ref[...], mods_ref[...]).astype(BF16)
    qkv = _dot(h, w_ref[...])
    bd = bd_ref[...]
    cos, sin = cos_ref[...], sin_ref[...]
    lane = lax.broadcasted_iota(jnp.int32, (1, LANES), 1)
    first_half = (lane % (HEAD_DIM // 2)) < (HEAD_DIM // 4)
    low_head = lane < HEAD_DIM

    def norm_rope(a, gain):
        sq = a * a
        hi = sq.astype(BF16)
        lo = (sq - hi.astype(F32)).astype(BF16)
        ms = _dot(hi, bd) + _dot(lo, bd)
        an = a * lax.rsqrt(ms + NORM_EPS) * gain
        partner = jnp.where(first_half,
                            pltpu.roll(an, LANES - HEAD_DIM // 4, 1),
                            pltpu.roll(an, HEAD_DIM // 4, 1))
        return an * cos + partner * sin

    def dup_heads(a):
        r = pltpu.roll(a, HEAD_DIM, 1)
        return jnp.where(low_head, a, r), jnp.where(low_head, r, a)

    nq = Q_COLS // LANES
    for t in range(nq):
        a = norm_rope(qkv[:, t * LANES:(t + 1) * LANES], gains_ref[0:1, :])
        q_ref[:, t * LANES:(t + 1) * LANES] = a.astype(BF16)
    for t in range(KV_COLS // LANES):
        kt = norm_rope(qkv[:, Q_COLS + t * LANES:Q_COLS + (t + 1) * LANES], gains_ref[1:2, :])
        k0, k1 = dup_heads(kt)
        k_ref[2 * t] = k0.astype(BF16)
        k_ref[2 * t + 1] = k1.astype(BF16)
        v0, v1 = dup_heads(qkv[:, Q_COLS + KV_COLS + t * LANES:Q_COLS + KV_COLS + (t + 1) * LANES])
        v_ref[2 * t] = v0.astype(BF16)
        v_ref[2 * t + 1] = v1.astype(BF16)


def _qkv_call(x, mods, wqkv_bf16, bd, gains, cos_t, sin_t):
    n = x.shape[0]
    tm = min(ROW_TILE, n)
    ncol = Q_COLS + 2 * KV_COLS
    return pl.pallas_call(
        _qkv_body,
        grid=(n // tm,),
        in_specs=[pl.BlockSpec((tm, D_MODEL), lambda i: (i, 0)),
                  pl.BlockSpec((SUBLANES, D_MODEL), lambda i: (0, 0)),
                  pl.BlockSpec((D_MODEL, ncol), lambda i: (0, 0)),
                  pl.BlockSpec((LANES, LANES), lambda i: (0, 0)),
                  pl.BlockSpec((SUBLANES, LANES), lambda i: (0, 0)),
                  pl.BlockSpec((tm, LANES), lambda i: (i, 0)),
                  pl.BlockSpec((tm, LANES), lambda i: (i, 0))],
        out_specs=[pl.BlockSpec((tm, Q_COLS), lambda i: (i, 0)),
                   pl.BlockSpec((N_KV_HEADS, tm, LANES), lambda i: (0, i, 0)),
                   pl.BlockSpec((N_KV_HEADS, tm, LANES), lambda i: (0, i, 0))],
        out_shape=[jax.ShapeDtypeStruct((n, Q_COLS), BF16),
                   jax.ShapeDtypeStruct((N_KV_HEADS, n, LANES), BF16),
                   jax.ShapeDtypeStruct((N_KV_HEADS, n, LANES), BF16)],
        compiler_params=_cparams(("arbitrary",), VMEM_LIMIT),
        name="qkv",
    )(x, mods, wqkv_bf16, bd, gains, cos_t, sin_t)


def _attn_body(sink_ref, q_ref, kp_ref, kc_ref, kn_ref, vp_ref, vc_ref, vn_ref,
               kx_ref, vx_ref, o_ref):
    b = pl.program_id(0)
    nb = pl.num_programs(0)
    blk = ATTN_BLOCK
    lane = lax.broadcasted_iota(jnp.int32, (1, LANES), 1)
    low_head = lane < HEAD_DIM
    qi = lax.broadcasted_iota(jnp.int32, (blk, 3 * blk), 0)
    kj = lax.broadcasted_iota(jnp.int32, (blk, 3 * blk), 1)
    valid = (kj >= qi) & (kj <= qi + 2 * WINDOW)
    valid = valid & ((kj >= blk) | (b > 0)) & ((kj < 2 * blk) | (b < nb - 1))
    per_kv = N_HEADS // N_KV_HEADS
    for g in range(N_KV_HEADS):
        kb = jnp.concatenate([kp_ref[g], kc_ref[g], kn_ref[g], kx_ref[g]], axis=0)
        vb = jnp.concatenate([vp_ref[g], vc_ref[g], vn_ref[g], vx_ref[g]], axis=0)
        parts = []
        for p in range(per_kv // 2):
            t = g * (per_kv // 2) + p
            qt = q_ref[:, t * LANES:(t + 1) * LANES]
            zero = jnp.zeros_like(qt)
            parts += [jnp.where(low_head, qt, zero), jnp.where(low_head, zero, qt)]
        lhs = jnp.concatenate(parts, axis=0)
        s = lax.dot_general(lhs, kb, (((1,), (1,)), ((), ())), preferred_element_type=F32)
        outs = []
        for hh in range(per_kv):
            sink = sink_ref[g * per_kv + hh]
            sh = s[hh * blk:(hh + 1) * blk]
            s_loc = jnp.where(valid, sh[:, :3 * blk], NEG_INF)
            s_ctx = sh[:, 3 * blk:]
            m = jnp.maximum(jnp.maximum(jnp.max(s_loc, axis=-1, keepdims=True),
                                        jnp.max(s_ctx, axis=-1, keepdims=True)), sink)
            p_loc = jnp.exp(s_loc - m)
            p_ctx = jnp.exp(s_ctx - m)
            den = (jnp.sum(p_loc, axis=-1, keepdims=True) + jnp.sum(p_ctx, axis=-1, keepdims=True)
                   + jnp.exp(sink - m))
            pr = jnp.concatenate([p_loc, p_ctx], axis=1).astype(BF16)
            outs.append(_dot(pr, vb) / den)
        for p in range(per_kv // 2):
            t = g * (per_kv // 2) + p
            o_ref[:, t * LANES:(t + 1) * LANES] = jnp.where(
                low_head, outs[2 * p], outs[2 * p + 1]).astype(BF16)


def _attn_call(sink, q, kd, vd, kx, vx):
    n = q.shape[0]
    blk = ATTN_BLOCK
    nb = n // blk
    nctx = kx.shape[1]
    kv_spec = lambda f: pl.BlockSpec((N_KV_HEADS, blk, LANES), f)
    prev = lambda b: (0, jnp.maximum(b - 1, 0), 0)
    cur = lambda b: (0, b, 0)
    nxt = lambda b: (0, jnp.minimum(b + 1, nb - 1), 0)
    ctx_spec = pl.BlockSpec((N_KV_HEADS, nctx, LANES), lambda b: (0, 0, 0))
    return pl.pallas_call(
        _attn_body,
        grid=(nb,),
        in_specs=[pl.BlockSpec(memory_space=pltpu.SMEM),
                  pl.BlockSpec((blk, Q_COLS), lambda b: (b, 0)),
                  kv_spec(prev), kv_spec(cur), kv_spec(nxt),
                  kv_spec(prev), kv_spec(cur), kv_spec(nxt),
                  ctx_spec, ctx_spec],
        out_specs=pl.BlockSpec((blk, Q_COLS), lambda b: (b, 0)),
        out_shape=jax.ShapeDtypeStruct((n, Q_COLS), BF16),
        compiler_params=_cparams(("arbitrary",), VMEM_LIMIT),
        name="attn",
    )(sink, q, kd, kd, kd, vd, vd, vd, kx, vx)


def _proj_body(a_ref, x_ref, mods_ref, w_ref, o_ref):
    o_ref[...] = x_ref[...] + mods_ref[3:4, :] * _dot(a_ref[...], w_ref[...])


def _proj_call(a, x, mods, w_bf16):
    n = x.shape[0]
    tm = min(ROW_TILE, n)
    return pl.pallas_call(
        _proj_body,
        grid=(n // tm,),
        in_specs=[pl.BlockSpec((tm, a.shape[1]), lambda i: (i, 0)),
                  pl.BlockSpec((tm, D_MODEL), lambda i: (i, 0)),
                  pl.BlockSpec((SUBLANES, D_MODEL), lambda i: (0, 0)),
                  pl.BlockSpec(w_bf16.shape, lambda i: (0, 0))],
        out_specs=pl.BlockSpec((tm, D_MODEL), lambda i: (i, 0)),
        out_shape=jax.ShapeDtypeStruct(x.shape, F32),
        compiler_params=_cparams(("arbitrary",), VMEM_LIMIT),
        name="proj",
    )(a, x, mods, w_bf16)


def _rope_lane_tables(length):
    rows = length // GRID_W
    row_pos = jnp.repeat(jnp.arange(rows, dtype=F32), GRID_W)
    col_pos = jnp.tile(jnp.arange(GRID_W, dtype=F32), rows)
    axis_dim = HEAD_DIM // 2
    inv_freq = ROPE_BASE ** (-jnp.arange(0, axis_dim, 2, dtype=F32) / axis_dim)
    ang_r = row_pos[:, None] * inv_freq[None, :]
    ang_c = col_pos[:, None] * inv_freq[None, :]
    cos_h = jnp.concatenate([jnp.cos(ang_r), jnp.cos(ang_r), jnp.cos(ang_c), jnp.cos(ang_c)], axis=1)
    sin_h = jnp.concatenate([-jnp.sin(ang_r), jnp.sin(ang_r), -jnp.sin(ang_c), jnp.sin(ang_c)], axis=1)
    reps = LANES // HEAD_DIM
    return jnp.tile(cos_h, (1, reps)), jnp.tile(sin_h, (1, reps))


def _attention_layer(x, ctx, mods_x, mods_c, wqkv, wo, q_gain, k_gain, sink):
    L = x.shape[0]
    nctx = ctx.shape[0]
    wqkv_b = wqkv.astype(BF16)
    head = np.arange(LANES) // HEAD_DIM
    bd = jnp.asarray((head[:, None] == head[None, :]).astype(np.float32) / HEAD_DIM).astype(BF16)
    reps = LANES // HEAD_DIM
    gains = jnp.zeros((SUBLANES, LANES), F32)
    gains = gains.at[0].set(jnp.tile(q_gain, reps) * (HEAD_DIM ** -0.5))
    gains = gains.at[1].set(jnp.tile(k_gain, reps))
    cos_t, sin_t = _rope_lane_tables(L)
    q, kd, vd = _qkv_call(x, mods_x, wqkv_b, bd, gains, cos_t, sin_t)
    ones = jnp.ones((nctx, LANES), F32)
    _, kx, vx = _qkv_call(ctx, mods_c, wqkv_b, bd, gains, ones, jnp.zeros_like(ones))
    o = _attn_call(sink, q, kd, vd, kx, vx)
    return _proj_call(o, x, mods_x, wo.astype(BF16))


def _pool_body(xp_ref, xc_ref, xn_ref, mods_ref, w_ref, o_ref, h_scr):
    i = pl.program_id(0)
    tm = xc_ref.shape[0]
    total = tm * pl.num_programs(0)
    x = xc_ref[...]
    mods = mods_ref[...]
    h_scr[0:POOL_HALO, :] = jnp.where(i > 0, _mod_norm(xp_ref[...], mods), 0.0)
    h_scr[POOL_HALO:POOL_HALO + tm, :] = _mod_norm(x, mods)
    h_scr[POOL_HALO + tm:, :] = jnp.where(i < pl.num_programs(0) - 1, _mod_norm(xn_ref[...], mods), 0.0)
    t = i * tm + lax.broadcasted_iota(jnp.int32, (tm, 1), 0)
    gd = POOL_GROUP_DIM
    for g, win in enumerate(POOL_WINDOWS):
        sl = slice(g * gd, (g + 1) * gd)
        half = win // 2
        tot = h_scr[POOL_HALO - half:POOL_HALO - half + tm, sl]
        for s in range(-half + 1, half):
            tot = tot + h_scr[POOL_HALO + s:POOL_HALO + s + tm, sl]
        lo = jnp.maximum(t - half, 0)
        hi = jnp.minimum(t + half - 1, total - 1)
        cnt = (hi - lo + 1).astype(F32)
        pooled = (tot / cnt - h_scr[POOL_HALO:POOL_HALO + tm, sl]).astype(BF16)
        y = _dot(pooled, w_ref[g]) * mods[4:5, sl]
        o_ref[:, sl] = x[:, sl] + mods[3:4, sl] * y


def _pool_layer(x, mods, w_bf16):
    n = x.shape[0]
    tm = min(ROW_TILE, n)
    r = tm // POOL_HALO
    last = n // POOL_HALO - 1
    return pl.pallas_call(
        _pool_body,
        grid=(n // tm,),
        in_specs=[pl.BlockSpec((POOL_HALO, D_MODEL), lambda i: (jnp.maximum(i * r - 1, 0), 0)),
                  pl.BlockSpec((tm, D_MODEL), lambda i: (i, 0)),
                  pl.BlockSpec((POOL_HALO, D_MODEL), lambda i: (jnp.minimum((i + 1) * r, last), 0)),
                  pl.BlockSpec((SUBLANES, D_MODEL), lambda i: (0, 0)),
                  pl.BlockSpec(w_bf16.shape, lambda i: (0, 0, 0))],
        out_specs=pl.BlockSpec((tm, D_MODEL), lambda i: (i, 0)),
        out_shape=jax.ShapeDtypeStruct(x.shape, F32),
        scratch_shapes=[pltpu.VMEM((tm + 2 * POOL_HALO, D_MODEL), F32)],
        compiler_params=_cparams(("arbitrary",), VMEM_LIMIT),
        name="pool",
    )(x, x, x, mods, w_bf16)


def _router_body(x_ref, mods_ref, r_ref, h_ref, route_ref, cnt_ref):
    h = _mod_norm(x_ref[...], mods_ref[...])
    h_ref[...] = h.astype(BF16)
    logits = jnp.dot(h, r_ref[...], precision=lax.Precision.HIGHEST, preferred_element_type=F32)
    lane = lax.broadcasted_iota(jnp.int32, logits.shape, 1)
    logits = jnp.where(lane < N_EXPERTS, logits, -jnp.inf)
    m1 = jnp.max(logits, axis=-1, keepdims=True)
    i1 = jnp.min(jnp.where(logits == m1, lane, LANES), axis=-1, keepdims=True)
    rest = jnp.where(lane == i1, -jnp.inf, logits)
    m2 = jnp.max(rest, axis=-1, keepdims=True)
    i2 = jnp.min(jnp.where(rest == m2, lane, LANES), axis=-1, keepdims=True)
    e = jnp.exp(m2 - m1)
    w1 = 1.0 / (1.0 + e)
    w2 = e / (1.0 + e)
    tm = h.shape[0]
    oh = jnp.where((lane == i1) | (lane == i2 + N_EXPERTS), 1.0, 0.0)
    before = (lax.broadcasted_iota(jnp.int32, (tm, tm), 1)
              < lax.broadcasted_iota(jnp.int32, (tm, tm), 0))
    prior = _dot(jnp.where(before, 1.0, 0.0).astype(BF16), oh.astype(BF16))
    cnt = jnp.sum(oh, axis=0, keepdims=True)
    r1 = jnp.sum(jnp.where(lane == i1, prior, 0.0), axis=-1, keepdims=True)
    r2 = (jnp.sum(jnp.where(lane == i2 + N_EXPERTS, prior, 0.0), axis=-1, keepdims=True)
          + jnp.sum(jnp.where(lane == i2, cnt, 0.0), axis=-1, keepdims=True))
    lane8 = lane[0:SUBLANES, :]
    cnt8 = jnp.broadcast_to(cnt, (SUBLANES, LANES))
    both = cnt8 + pltpu.roll(cnt8, LANES - N_EXPERTS, 1)
    seg_len = jnp.where(lane8 < N_EXPERTS, jnp.ceil(both * (1.0 / SEG_ALIGN)) * SEG_ALIGN, 0.0)
    incl = seg_len
    shift = 1
    while shift < N_EXPERTS:
        incl = incl + jnp.where(lane8 >= shift, pltpu.roll(incl, shift, 1), 0.0)
        shift *= 2
    seg = incl - seg_len
    p1 = r1 + jnp.sum(jnp.where(lane == i1, seg[0:1, :], 0.0), axis=-1, keepdims=True)
    p2 = r2 + jnp.sum(jnp.where(lane == i2, seg[0:1, :], 0.0), axis=-1, keepdims=True)
    col = lax.broadcasted_iota(jnp.int32, route_ref.shape, 1)
    vals = (i1.astype(F32), i2.astype(F32), w1, w2, p1, p2)
    out = jnp.zeros(route_ref.shape, F32)
    for k, v in enumerate(vals):
        out = jnp.where(col == k, v, out)
    route_ref[...] = out
    row = lax.broadcasted_iota(jnp.int32, cnt_ref.shape[1:], 0)
    cnt_ref[0] = jnp.where(row == 0, seg_len, jnp.where(row == 1, seg, 0.0))


def _router_call(x, mods, router_pad):
    n = x.shape[0]
    tm = min(ROUTE_TILE, n)
    return pl.pallas_call(
        _router_body,
        grid=(n // tm,),
        in_specs=[pl.BlockSpec((tm, D_MODEL), lambda i: (i, 0)),
                  pl.BlockSpec((SUBLANES, D_MODEL), lambda i: (0, 0)),
                  pl.BlockSpec((D_MODEL, LANES), lambda i: (0, 0))],
        out_specs=[pl.BlockSpec((tm, D_MODEL), lambda i: (i, 0)),
                   pl.BlockSpec((tm, SUBLANES), lambda i: (i, 0)),
                   pl.BlockSpec((1, SUBLANES, LANES), lambda i: (i, 0, 0))],
        out_shape=[jax.ShapeDtypeStruct((n, D_MODEL), BF16),
                   jax.ShapeDtypeStruct((n, SUBLANES), F32),
                   jax.ShapeDtypeStruct((n // tm, SUBLANES, LANES), F32)],
        compiler_params=_cparams(("arbitrary",), VMEM_LIMIT),
        name="router",
    )(x, mods, router_pad)


def _segment_copies(i, seg_ref, dst_ref, len_ref, tile_buf, sorted_hbm, sem, to_hbm):
    for e in range(N_EXPERTS):
        length = len_ref[i * N_EXPERTS + e]
        seg = seg_ref[i * N_EXPERTS + e]
        dst = dst_ref[i * N_EXPERTS + e]
        size = SEG_ALIGN
        while size <= tile_buf.shape[0] // 2:
            done = length & ~(2 * size - 1)
            a = tile_buf.at[pl.ds(pl.multiple_of(seg + done, SEG_ALIGN), size)]
            b = sorted_hbm.at[pl.ds(pl.multiple_of(dst + done, SEG_ALIGN), size)]
            copy = pltpu.make_async_copy(a, b, sem) if to_hbm else pltpu.make_async_copy(b, a, sem)
            yield (length & size) != 0, copy
            size *= 2


def _scatter_body(seg_ref, dst_ref, len_ref, h_ref, prow_ref, xs_in_ref, xs_ref, sbuf, sem):
    del xs_in_ref
    i = pl.program_id(0)
    cap, tm = sbuf.shape[0], h_ref.shape[0]
    r = lax.broadcasted_iota(jnp.int32, (cap, tm), 0)
    hit = (r == prow_ref[0:1, :]) | (r == prow_ref[1:2, :])
    sel = jnp.where(hit, 1.0, 0.0).astype(BF16)
    sbuf[...] = _dot(sel, h_ref[...]).astype(BF16)
    for pred, copy in _segment_copies(i, seg_ref, dst_ref, len_ref, sbuf, xs_ref, sem, True):
        pl.when(pred)(copy.start)
    for pred, copy in _segment_copies(i, seg_ref, dst_ref, len_ref, sbuf, xs_ref, sem, True):
        pl.when(pred)(copy.wait)


def _route_cap(tm):
    return 2 * tm + N_EXPERTS * SEG_ALIGN


def _scatter_call(seg, dst, seg_len, h, prow, n_rows):
    n = h.shape[0]
    tm = min(ROUTE_TILE, n)
    zeros = jnp.zeros((n_rows, D_MODEL), BF16)
    return pl.pallas_call(
        _scatter_body,
        grid_spec=pltpu.PrefetchScalarGridSpec(
            num_scalar_prefetch=3,
            grid=(n // tm,),
            in_specs=[pl.BlockSpec((tm, D_MODEL), lambda i, *_: (i, 0)),
                      pl.BlockSpec((SUBLANES, tm), lambda i, *_: (i, 0)),
                      pl.BlockSpec(memory_space=pl.ANY)],
            out_specs=pl.BlockSpec(memory_space=pl.ANY),
            scratch_shapes=[pltpu.VMEM((_route_cap(tm), D_MODEL), BF16),
                            pltpu.SemaphoreType.DMA(())]),
        out_shape=jax.ShapeDtypeStruct((n_rows, D_MODEL), BF16),
        input_output_aliases={5: 0},
        compiler_params=_cparams(("arbitrary",), VMEM_LIMIT),
        name="moe_scatter",
    )(seg, dst, seg_len, h, prow, zeros)


def _moe_body(te_ref, tv_ref, tf_ref, xs_ref, wi_hbm, wo_hbm, ys_ref, wi_res, wo_res, stage_i,
              stage_o, act, sems):
    i = pl.program_id(0)

    @pl.when(tf_ref[i] > 0)
    def _():
        _load_swiglu_weights(te_ref[i], wi_hbm, wo_hbm, wi_res, wo_res, stage_i, stage_o, sems)

    @pl.when(tv_ref[i] > 0)
    def _():
        ys_ref[...] = _swiglu_tile(xs_ref[...], wi_res, wo_res, act).astype(BF16)

    @pl.when(tv_ref[i] == 0)
    def _():
        ys_ref[...] = jnp.zeros(ys_ref.shape, BF16)


def _moe_call(tile_expert, tile_live, tile_first, xs, wi_all, wo_all):
    n_rows = xs.shape[0]
    return pl.pallas_call(
        _moe_body,
        grid_spec=pltpu.PrefetchScalarGridSpec(
            num_scalar_prefetch=3,
            grid=(n_rows // MOE_TILE,),
            in_specs=[pl.BlockSpec((MOE_TILE, D_MODEL), lambda i, *_: (i, 0)),
                      pl.BlockSpec(memory_space=pl.ANY),
                      pl.BlockSpec(memory_space=pl.ANY)],
            out_specs=pl.BlockSpec((MOE_TILE, D_MODEL), lambda i, *_: (i, 0)),
            scratch_shapes=_swiglu_scratch(MOE_TILE)),
        out_shape=jax.ShapeDtypeStruct((n_rows, D_MODEL), BF16),
        compiler_params=_cparams(("arbitrary",), VMEM_LIMIT),
        name="moe_ffn",
    )(tile_expert, tile_live, tile_first, xs, wi_all, wo_all)


def _combine_body(seg_ref, dst_ref, len_ref, x_ref, route_ref, pcol_ref, mods_ref, ys_ref, o_ref,
                  ybuf, sem):
    i = pl.program_id(0)
    tm, cap = x_ref.shape[0], ybuf.shape[0]

    @pl.when(i == 0)
    def _():
        ybuf[...] = jnp.zeros(ybuf.shape, BF16)

    for pred, copy in _segment_copies(i, seg_ref, dst_ref, len_ref, ybuf, ys_ref, sem, False):
        pl.when(pred)(copy.start)
    for pred, copy in _segment_copies(i, seg_ref, dst_ref, len_ref, ybuf, ys_ref, sem, False):
        pl.when(pred)(copy.wait)
    c = lax.broadcasted_iota(jnp.int32, (tm, cap), 1)
    pcol = pcol_ref[...]
    y = ybuf[...]
    ya = _dot(jnp.where(c == pcol[:, 0:1], 1.0, 0.0).astype(BF16), y)
    yb = _dot(jnp.where(c == pcol[:, 1:2], 1.0, 0.0).astype(BF16), y)
    route = route_ref[...]
    o_ref[...] = x_ref[...] + mods_ref[3:4, :] * (route[:, 2:3] * ya + route[:, 3:4] * yb)


def _combine_call(seg, dst, seg_len, x, route, pcol, mods, ys):
    n = x.shape[0]
    tm = min(ROUTE_TILE, n)
    return pl.pallas_call(
        _combine_body,
        grid_spec=pltpu.PrefetchScalarGridSpec(
            num_scalar_prefetch=3,
            grid=(n // tm,),
            in_specs=[pl.BlockSpec((tm, D_MODEL), lambda i, *_: (i, 0)),
                      pl.BlockSpec((tm, SUBLANES), lambda i, *_: (i, 0)),
                      pl.BlockSpec((tm, SUBLANES), lambda i, *_: (i, 0)),
                      pl.BlockSpec((SUBLANES, D_MODEL), lambda i, *_: (0, 0)),
                      pl.BlockSpec(memory_space=pl.ANY)],
            out_specs=pl.BlockSpec((tm, D_MODEL), lambda i, *_: (i, 0)),
            scratch_shapes=[pltpu.VMEM((_route_cap(tm), D_MODEL), BF16),
                            pltpu.SemaphoreType.DMA(())]),
        out_shape=jax.ShapeDtypeStruct(x.shape, F32),
        compiler_params=_cparams(("arbitrary",), VMEM_LIMIT),
        name="moe_combine",
    )(seg, dst, seg_len, x, route, pcol, mods, ys)


def _routing_tables(route, counts, tm, n_tiles):
    nt = counts.shape[0]
    seg_len = counts[:, 0, :N_EXPERTS].astype(jnp.int32)
    seg = counts[:, 1, :N_EXPERTS].astype(jnp.int32)
    tiles = (jnp.sum(seg_len, axis=0) + MOE_TILE - 1) // MOE_TILE
    tile_end = jnp.cumsum(tiles)
    start = (tile_end - tiles) * MOE_TILE
    dst = start[None, :] + jnp.cumsum(seg_len, axis=0) - seg_len
    t = jnp.arange(n_tiles, dtype=jnp.int32)
    expert = jnp.sum((t[:, None] >= tile_end[None, :]).astype(jnp.int32), axis=1)
    live = (t < tile_end[-1]).astype(jnp.int32)
    last_expert = jnp.sum((tile_end[-1] - 1 >= tile_end).astype(jnp.int32))
    expert = jnp.where(live > 0, expert, last_expert).astype(jnp.int32)
    prev = jnp.concatenate([jnp.full((1,), -1, jnp.int32), expert[:-1]])
    first = (live * (expert != prev)).astype(jnp.int32)
    p12 = route[:, 4:6].astype(jnp.int32)
    pcol = jnp.concatenate([p12, jnp.zeros((p12.shape[0], SUBLANES - 2), jnp.int32)], axis=1)
    prow = jnp.concatenate([p12.reshape(nt, tm, 2).transpose(0, 2, 1),
                            jnp.full((nt, SUBLANES - 2, tm), -1, jnp.int32)], axis=1)
    return (seg.reshape(-1), dst.reshape(-1).astype(jnp.int32), seg_len.reshape(-1),
            expert, live, first, pcol, prow.reshape(nt * SUBLANES, tm))


def _moe_layer(x, mods, router, wi_all, wo_all, layer):
    n = x.shape[0]
    tm = min(ROUTE_TILE, n)
    router_pad = jnp.zeros((D_MODEL, LANES), F32).at[:, :N_EXPERTS].set(router)
    h, route, counts = _router_call(x, mods, router_pad)
    max_rows = 2 * n + (n // tm) * N_EXPERTS * (SEG_ALIGN - 1)
    n_tiles = -(-max_rows // MOE_TILE) + N_EXPERTS
    seg, dst, seg_len, tile_expert, tile_live, tile_first, pcol, prow = _routing_tables(
        route, counts, tm, n_tiles)
    xs = _scatter_call(seg, dst, seg_len, h, prow, n_tiles * MOE_TILE)
    ys = _moe_call(tile_expert + layer * N_EXPERTS, tile_live, tile_first, xs, wi_all, wo_all)
    return _combine_call(seg, dst, seg_len, x, route, pcol, mods, ys)


def _mods(gain, shift, scale, gate, extra=None):
    rows = [gain, shift, scale, gate, extra if extra is not None else jnp.zeros_like(gain)]
    m = jnp.stack(rows, axis=0)
    return jnp.concatenate([m, jnp.zeros((SUBLANES - m.shape[0], m.shape[1]), F32)], axis=0)


def kernel(x, c, ctx, c_ctx, ada_w, ada_b, norm_mix, norm_ffn, fnet_w, attn_wqkv, attn_q_gain,
           attn_k_gain, attn_sink, attn_wo, pool_w, pool_scale, ffn_wi, ffn_wo, moe_router,
           moe_wi, moe_wo):
    assert x.shape[0] == 1 and x.shape[2] == D_MODEL
    depth = ada_w.shape[0]
    xs = x[0]
    cs = ctx[0]
    cc = jnp.zeros((SUBLANES, D_MODEL), F32).at[0].set(c[0]).at[1].set(c_ctx)
    ada = _ada_call(cc, ada_w, ada_b)

    attn_layers = [i for i in range(depth) if i % N_MIXERS == 1]
    last_ctx_read = attn_layers[-1] if attn_layers else -1
    n_side = math.isqrt(xs.shape[0])
    assert n_side * n_side == xs.shape[0] and n_side % SUBLANES == 0
    seq_tables = _sequence_tables(n_side)
    ch = _channel_table()
    moe_wi_all = moe_wi.reshape((-1,) + moe_wi.shape[2:])
    moe_wo_all = moe_wo.reshape((-1,) + moe_wo.shape[2:])

    for i in range(depth):
        mixer = i % N_MIXERS
        j = i // N_MIXERS
        f = i // 2
        ctx_full = i < last_ctx_read
        ctx_live = i <= last_ctx_read
        sh1, sc1, g1, sh2, sc2, g2 = [ada[i, 0, k * D_MODEL:(k + 1) * D_MODEL] for k in range(6)]
        csh1, csc1, cg1, csh2, csc2, cg2 = [ada[i, 1, k * D_MODEL:(k + 1) * D_MODEL] for k in range(6)]
        extra = pool_scale[j] if mixer == 2 else None
        m1 = _mods(norm_mix[i], sh1, sc1, g1, extra)
        m2 = _mods(norm_ffn[i], sh2, sc2, g2)
        cm1 = _mods(norm_mix[i], csh1, csc1, cg1, extra)
        cm2 = _mods(norm_ffn[i], csh2, csc2, cg2)

        if mixer == 0:
            wb = fnet_w[j].astype(BF16)
            xs = _fourier_layer(xs, m1, wb, seq_tables, ch)
            if ctx_full:
                cs = _ctx_fourier_layer(cs, cm1, wb, ch)
        elif mixer == 1:
            assert ctx_live and not ctx_full
            xs = _attention_layer(xs, cs, m1, cm1, attn_wqkv[j], attn_wo[j], attn_q_gain[j],
                                  attn_k_gain[j], attn_sink[j])
        else:
            wb = pool_w[j].astype(BF16)
            xs = _pool_layer(xs, m1, wb)
            if ctx_full:
                cs = _pool_layer(cs, cm1, wb)

        if i % 2 == 0:
            xs = _ffn_layer(xs, m2, ffn_wi, ffn_wo, f)
            if ctx_full:
                cs = _ffn_layer(cs, cm2, ffn_wi, ffn_wo, f)
        else:
            xs = _moe_layer(xs, m2, moe_router[f], moe_wi_all, moe_wo_all, f)
            if ctx_full:
                cs = _moe_layer(cs, cm2, moe_router[f], moe_wi_all, moe_wo_all, f)
    return xs[None]
```

```python
import functools
import math

import numpy as np
import jax
import jax.numpy as jnp
from jax import lax
from jax.experimental import pallas as pl
from jax.experimental.pallas import tpu as pltpu

F32 = jnp.float32
BF16 = jnp.bfloat16

D_MODEL = 1024
GRID_W = 64
N_MIXERS = 3
FNET_GROUPS = 4
FNET_GROUP_DIM = D_MODEL // FNET_GROUPS
N_HEADS = 16
N_KV_HEADS = 4
HEAD_DIM = 64
Q_COLS = N_HEADS * HEAD_DIM
KV_COLS = N_KV_HEADS * HEAD_DIM
WINDOW = 128
ATTN_BLOCK = 128
ROPE_BASE = 10000.0
POOL_WINDOWS = (2, 4, 8, 16)
POOL_GROUP_DIM = D_MODEL // len(POOL_WINDOWS)
POOL_HALO = 8
D_FF = 3584
N_EXPERTS = 8
NORM_EPS = 1e-6
NEG_INF = -1e30
LOG2_E = 1.4426950408889634

LANES = 128
SUBLANES = 8
VMEM_LIMIT = 56 * 1024 * 1024

ROW_TILE = 512
FF_CHUNK = 512
MOE_TILE = 256
W_CHUNK = 512
ROUTE_TILE = 512
SEG_ALIGN = 16


def _cparams(sem, vmem=None):
    return pltpu.CompilerParams(dimension_semantics=sem, vmem_limit_bytes=vmem)


def _mod_norm(x, mods):
    ms = jnp.mean(x * x, axis=-1, keepdims=True)
    y = x * lax.rsqrt(ms + NORM_EPS) * mods[0:1, :]
    return y * (1.0 + mods[2:3, :]) + mods[1:2, :]


def _dot(a, b):
    return jnp.dot(a, b, preferred_element_type=F32)


def _ada_body(cc_ref, w_ref, b_ref, o_ref):
    a = cc_ref[...]
    a = a / (1.0 + jnp.exp(-a))
    o_ref[0] = jnp.dot(a, w_ref[0], precision=lax.Precision.HIGHEST,
                       preferred_element_type=F32) + b_ref[0]


def _ada_call(cc, ada_w, ada_b):
    depth, d, n6 = ada_w.shape
    tn = n6 // 4
    return pl.pallas_call(
        _ada_body,
        grid=(depth, n6 // tn),
        in_specs=[pl.BlockSpec((SUBLANES, d), lambda l, j: (0, 0)),
                  pl.BlockSpec((1, d, tn), lambda l, j: (l, 0, j)),
                  pl.BlockSpec((1, 1, tn), lambda l, j: (l, 0, j))],
        out_specs=pl.BlockSpec((1, SUBLANES, tn), lambda l, j: (l, 0, j)),
        out_shape=jax.ShapeDtypeStruct((depth, SUBLANES, n6), F32),
        compiler_params=_cparams(("arbitrary", "arbitrary"), VMEM_LIMIT),
        name="ada",
    )(cc, ada_w, ada_b.reshape(depth, 1, n6))


def _cos_sin(n, period):
    k = np.arange(n, dtype=np.float64)
    ang = 2.0 * np.pi * np.outer(k, k) / period
    return np.cos(ang).astype(np.float32), np.sin(ang).astype(np.float32)


def _channel_table():
    c, s = _cos_sin(FNET_GROUP_DIM, FNET_GROUP_DIM)
    return (jnp.concatenate([jnp.asarray(c), jnp.asarray(s)], axis=0)
            * (FNET_GROUP_DIM ** -0.5)).astype(BF16)


def _sequence_tables(n):
    c, s = _cos_sin(n, n)
    eye = np.eye(SUBLANES, dtype=np.float32)
    rows = n * SUBLANES
    f = np.stack([c, -s]) * np.float32(1.0 / n)
    ka = (f[:, None, :, :, None] * eye[None, :, None, None, :]).reshape(2 * rows, rows)
    ck = (c[:, None, :, None] * eye[None, :, None, :]).reshape(rows, rows)
    sk = (s[:, None, :, None] * eye[None, :, None, :]).reshape(rows, rows)
    cs = np.concatenate([ck, sk], axis=1)
    tc, ts = _cos_sin(n, n * n)
    oct_ = n // SUBLANES
    tc = np.ascontiguousarray(np.broadcast_to(tc.reshape(oct_, rows, 1), (oct_, rows, LANES)))
    ts = np.ascontiguousarray(np.broadcast_to(ts.reshape(oct_, rows, 1), (oct_, rows, LANES)))
    return jnp.asarray(ka).astype(BF16), jnp.asarray(cs).astype(BF16), jnp.asarray(tc), jnp.asarray(ts)


def _fft_a_body(x_ref, mods_ref, ka_ref, tc_ref, ts_ref, zr_ref, zi_ref):
    rows = ka_ref.shape[1]
    x = x_ref[...].reshape(rows, D_MODEL)
    h = _mod_norm(x, mods_ref[...]).astype(BF16)
    ka = ka_ref[...]
    tc, ts = tc_ref[0], ts_ref[0]
    gw = FNET_GROUP_DIM
    for g in range(D_MODEL // gw):
        z = _dot(ka, h[:, g * gw:(g + 1) * gw])
        zr, zi = z[:rows], z[rows:]
        for t in range(gw // LANES):
            a = zr[:, t * LANES:(t + 1) * LANES]
            b = zi[:, t * LANES:(t + 1) * LANES]
            sl = slice(g * gw + t * LANES, g * gw + (t + 1) * LANES)
            zr_ref[:, :, sl] = (a * tc + b * ts).reshape(SUBLANES, rows // SUBLANES, LANES)
            zi_ref[:, :, sl] = (b * tc - a * ts).reshape(SUBLANES, rows // SUBLANES, LANES)


def _fft_b_body(zr_ref, zi_ref, x_ref, mods_ref, cs_ref, ch_ref, w_ref, o_ref):
    rows = cs_ref.shape[0]
    cs = cs_ref[...]
    gw = FNET_GROUP_DIM
    ys = []
    for g in range(D_MODEL // gw):
        zr = zr_ref[:, :, g * gw:(g + 1) * gw].reshape(rows, gw).astype(BF16)
        zi = zi_ref[:, :, g * gw:(g + 1) * gw].reshape(rows, gw).astype(BF16)
        xr = _dot(cs, jnp.concatenate([zr, zi], axis=0))
        xi = _dot(cs, jnp.concatenate([zi, -zr], axis=0))
        ys.append(_dot(jnp.concatenate([xr, xi], axis=1).astype(BF16), ch_ref[...]).astype(BF16))
    y = _dot(jnp.concatenate(ys, axis=1), w_ref[...])
    x = x_ref[...].reshape(rows, D_MODEL)
    o_ref[...] = (x + mods_ref[3:4, :] * y).reshape(o_ref.shape)


def _fourier_layer(x, mods, w_bf16, seq_tables, ch):
    L = x.shape[0]
    n = math.isqrt(L)
    rows = n * SUBLANES
    oct_ = n // SUBLANES
    ka, cs, tc, ts = seq_tables
    x3 = x.reshape(n, n, D_MODEL)
    gw = FNET_GROUP_DIM
    zshape = jax.ShapeDtypeStruct((n, n, D_MODEL), F32)
    once = dict(pipeline_mode=pl.Buffered(1))
    zr, zi = pl.pallas_call(
        _fft_a_body,
        grid=(oct_,),
        in_specs=[pl.BlockSpec((n, SUBLANES, D_MODEL), lambda o: (0, o, 0)),
                  pl.BlockSpec((SUBLANES, D_MODEL), lambda o: (0, 0)),
                  pl.BlockSpec((2 * rows, rows), lambda o: (0, 0), **once),
                  pl.BlockSpec((1, rows, LANES), lambda o: (o, 0, 0)),
                  pl.BlockSpec((1, rows, LANES), lambda o: (o, 0, 0))],
        out_specs=[pl.BlockSpec((SUBLANES, n, D_MODEL), lambda o: (o, 0, 0)),
                   pl.BlockSpec((SUBLANES, n, D_MODEL), lambda o: (o, 0, 0))],
        out_shape=[zshape, zshape],
        compiler_params=_cparams(("arbitrary",), VMEM_LIMIT),
        name="fft_a",
    )(x3, mods, ka, tc, ts)
    out = pl.pallas_call(
        _fft_b_body,
        grid=(oct_,),
        in_specs=[pl.BlockSpec((n, SUBLANES, D_MODEL), lambda p: (0, p, 0)),
                  pl.BlockSpec((n, SUBLANES, D_MODEL), lambda p: (0, p, 0)),
                  pl.BlockSpec((n, SUBLANES, D_MODEL), lambda p: (0, p, 0)),
                  pl.BlockSpec((SUBLANES, D_MODEL), lambda p: (0, 0)),
                  pl.BlockSpec((rows, 2 * rows), lambda p: (0, 0), **once),
                  pl.BlockSpec((2 * gw, gw), lambda p: (0, 0), **once),
                  pl.BlockSpec((D_MODEL, D_MODEL), lambda p: (0, 0), **once)],
        out_specs=pl.BlockSpec((n, SUBLANES, D_MODEL), lambda p: (0, p, 0)),
        out_shape=jax.ShapeDtypeStruct((n, n, D_MODEL), F32),
        compiler_params=_cparams(("arbitrary",), VMEM_LIMIT),
        name="fft_b",
    )(zr, zi, x3, mods, cs, ch, w_bf16)
    return out.reshape(L, D_MODEL)


def _ctx_fourier_body(x_ref, mods_ref, f_ref, ch_ref, w_ref, o_ref):
    x = x_ref[...]
    n = x.shape[0]
    h = _mod_norm(x, mods_ref[...]).astype(BF16)
    g = _dot(f_ref[...], h)
    gr, gi = g[:n], g[n:]
    gw = FNET_GROUP_DIM
    ys = []
    for k in range(D_MODEL // gw):
        sl = slice(k * gw, (k + 1) * gw)
        ys.append(_dot(jnp.concatenate([gr[:, sl], gi[:, sl]], axis=1).astype(BF16), ch_ref[...]))
    y = jnp.concatenate(ys, axis=1).astype(BF16)
    o_ref[...] = x + mods_ref[3:4, :] * _dot(y, w_ref[...])


def _ctx_fourier_layer(ctx, mods, w_bf16, ch):
    n = ctx.shape[0]
    c, s = _cos_sin(n, n)
    f = (jnp.concatenate([jnp.asarray(c), -jnp.asarray(s)], axis=0) * (n ** -0.5)).astype(BF16)
    return pl.pallas_call(
        _ctx_fourier_body,
        out_shape=jax.ShapeDtypeStruct(ctx.shape, F32),
        compiler_params=pltpu.CompilerParams(vmem_limit_bytes=VMEM_LIMIT),
        name="ctx_fourier",
    )(ctx, mods, f, ch, w_bf16)


def _load_swiglu_weights(e, wi_hbm, wo_hbm, wi_res, wo_res, stage_i, stage_o, sems):
    n_i = wi_res.shape[1] // W_CHUNK
    total = n_i + wo_res.shape[0] // W_CHUNK

    def copy(c):
        slot = c % 2
        if c < n_i:
            src = wi_hbm.at[e, :, pl.ds(c * W_CHUNK, W_CHUNK)]
            return pltpu.make_async_copy(src, stage_i.at[slot], sems.at[slot])
        src = wo_hbm.at[e, pl.ds((c - n_i) * W_CHUNK, W_CHUNK), :]
        return pltpu.make_async_copy(src, stage_o.at[slot], sems.at[slot])

    copy(0).start()
    for c in range(total):
        if c + 1 < total:
            copy(c + 1).start()
        copy(c).wait()
        if c < n_i:
            wi_res[:, c * W_CHUNK:(c + 1) * W_CHUNK] = stage_i[c % 2].astype(BF16)
        else:
            k = c - n_i
            wo_res[k * W_CHUNK:(k + 1) * W_CHUNK, :] = stage_o[c % 2].astype(BF16)


def _swiglu_tile(h, wi_res, wo_res, act):
    for k in range(D_FF // FF_CHUNK):
        gate = _dot(h, wi_res[:, k * FF_CHUNK:(k + 1) * FF_CHUNK])
        up = _dot(h, wi_res[:, D_FF + k * FF_CHUNK:D_FF + (k + 1) * FF_CHUNK])
        act[:, k * FF_CHUNK:(k + 1) * FF_CHUNK] = (gate / (1.0 + jnp.exp(-gate)) * up).astype(BF16)
    return _dot(act[...], wo_res[...])


def _swiglu_scratch(tm):
    return [pltpu.VMEM((D_MODEL, 2 * D_FF), BF16), pltpu.VMEM((D_FF, D_MODEL), BF16),
            pltpu.VMEM((2, D_MODEL, W_CHUNK), F32), pltpu.VMEM((2, W_CHUNK, D_MODEL), F32),
            pltpu.VMEM((tm, D_FF), BF16), pltpu.SemaphoreType.DMA((2,))]


def _ffn_body(layer, x_ref, mods_ref, wi_hbm, wo_hbm, o_ref, wi_res, wo_res, stage_i, stage_o,
              act, sems):
    @pl.when(pl.program_id(0) == 0)
    def _():
        _load_swiglu_weights(layer, wi_hbm, wo_hbm, wi_res, wo_res, stage_i, stage_o, sems)

    x = x_ref[...]
    h = _mod_norm(x, mods_ref[...]).astype(BF16)
    o_ref[...] = x + mods_ref[3:4, :] * _swiglu_tile(h, wi_res, wo_res, act)


def _ffn_layer(x, mods, wi_all, wo_all, layer):
    n = x.shape[0]
    tm = min(ROW_TILE, n)
    return pl.pallas_call(
        functools.partial(_ffn_body, layer),
        grid=(n // tm,),
        in_specs=[pl.BlockSpec((tm, D_MODEL), lambda i: (i, 0)),
                  pl.BlockSpec((SUBLANES, D_MODEL), lambda i: (0, 0)),
                  pl.BlockSpec(memory_space=pl.ANY),
                  pl.BlockSpec(memory_space=pl.ANY)],
        out_specs=pl.BlockSpec((tm, D_MODEL), lambda i: (i, 0)),
        out_shape=jax.ShapeDtypeStruct(x.shape, F32),
        scratch_shapes=_swiglu_scratch(tm),
        compiler_params=_cparams(("arbitrary",), VMEM_LIMIT),
        name="ffn",
    )(x, mods, wi_all, wo_all)


def _qkv_body(x_ref, mods_ref, w_ref, bd_ref, gains_ref, cos_ref, sin_ref, q_ref, k_ref, v_ref):
    h = _mod_norm(x_ref[...], mods_ref[...]).astype(BF16)
    qkv = _dot(h, w_ref[...])
    bd = bd_ref[...]
    cos, sin = cos_ref[...], sin_ref[...]
    lane = lax.broadcasted_iota(jnp.int32, (1, LANES), 1)
    first_half = (lane % (HEAD_DIM // 2)) < (HEAD_DIM // 4)
    low_head = lane < HEAD_DIM

    def norm_rope(a, gain):
        sq = a * a
        hi = sq.astype(BF16)
        lo = (sq - hi.astype(F32)).astype(BF16)
        ms = _dot(hi, bd) + _dot(lo, bd)
        an = a * lax.rsqrt(ms + NORM_EPS) * gain
        partner = jnp.where(first_half,
                            pltpu.roll(an, LANES - HEAD_DIM // 4, 1),
                            pltpu.roll(an, HEAD_DIM // 4, 1))
        return an * cos + partner * sin

    def dup_heads(a):
        r = pltpu.roll(a, HEAD_DIM, 1)
        return jnp.where(low_head, a, r), jnp.where(low_head, r, a)

    nq = Q_COLS // LANES
    for t in range(nq):
        a = norm_rope(qkv[:, t * LANES:(t + 1) * LANES], gains_ref[0:1, :])
        q_ref[:, t * LANES:(t + 1) * LANES] = a.astype(BF16)
    for t in range(KV_COLS // LANES):
        kt = norm_rope(qkv[:, Q_COLS + t * LANES:Q_COLS + (t + 1) * LANES], gains_ref[1:2, :])
        k0, k1 = dup_heads(kt)
        k_ref[2 * t] = k0.astype(BF16)
        k_ref[2 * t + 1] = k1.astype(BF16)
        v0, v1 = dup_heads(qkv[:, Q_COLS + KV_COLS + t * LANES:Q_COLS + KV_COLS + (t + 1) * LANES])
        for hk, vv in ((2 * t, v0), (2 * t + 1, v1)):
            v_ref[2 * hk] = jnp.where(low_head, vv, 1.0).astype(BF16)
            v_ref[2 * hk + 1] = jnp.where(low_head, 1.0, vv).astype(BF16)


def _qkv_call(x, mods, wqkv_bf16, bd, gains, cos_t, sin_t):
    n = x.shape[0]
    tm = min(ROW_TILE, n)
    ncol = Q_COLS + 2 * KV_COLS
    return pl.pallas_call(
        _qkv_body,
        grid=(n // tm,),
        in_specs=[pl.BlockSpec((tm, D_MODEL), lambda i: (i, 0)),
                  pl.BlockSpec((SUBLANES, D_MODEL), lambda i: (0, 0)),
                  pl.BlockSpec((D_MODEL, ncol), lambda i: (0, 0)),
                  pl.BlockSpec((LANES, LANES), lambda i: (0, 0)),
                  pl.BlockSpec((SUBLANES, LANES), lambda i: (0, 0)),
                  pl.BlockSpec((tm, LANES), lambda i: (i, 0)),
                  pl.BlockSpec((tm, LANES), lambda i: (i, 0))],
        out_specs=[pl.BlockSpec((tm, Q_COLS), lambda i: (i, 0)),
                   pl.BlockSpec((N_KV_HEADS, tm, LANES), lambda i: (0, i, 0)),
                   pl.BlockSpec((2 * N_KV_HEADS, tm, LANES), lambda i: (0, i, 0))],
        out_shape=[jax.ShapeDtypeStruct((n, Q_COLS), BF16),
                   jax.ShapeDtypeStruct((N_KV_HEADS, n, LANES), BF16),
                   jax.ShapeDtypeStruct((2 * N_KV_HEADS, n, LANES), BF16)],
        compiler_params=_cparams(("arbitrary",), VMEM_LIMIT),
        name="qkv",
    )(x, mods, wqkv_bf16, bd, gains, cos_t, sin_t)


def _attn_body(sink_ref, q_ref, kp_ref, kc_ref, kn_ref, vp_ref, vc_ref, vn_ref,
               kx_ref, vx_ref, o_ref):
    b = pl.program_id(0)
    nb = pl.num_programs(0)
    blk = ATTN_BLOCK
    lane = lax.broadcasted_iota(jnp.int32, (1, LANES), 1)
    low_head = lane < HEAD_DIM
    qi = lax.broadcasted_iota(jnp.int32, (blk, 3 * blk), 0)
    kj = lax.broadcasted_iota(jnp.int32, (blk, 3 * blk), 1)
    valid = (kj >= qi) & (kj <= qi + 2 * WINDOW)
    valid = valid & ((kj >= blk) | (b > 0)) & ((kj < 2 * blk) | (b < nb - 1))
    per_kv = N_HEADS // N_KV_HEADS
    for g in range(N_KV_HEADS):
        kb = jnp.concatenate([kp_ref[g], kc_ref[g], kn_ref[g], kx_ref[g]], axis=0)
        vbs = [jnp.concatenate([vp_ref[2 * g + a], vc_ref[2 * g + a], vn_ref[2 * g + a],
                                vx_ref[2 * g + a]], axis=0) for a in range(2)]
        parts = []
        for p in range(per_kv // 2):
            t = g * (per_kv // 2) + p
            qt = q_ref[:, t * LANES:(t + 1) * LANES]
            zero = jnp.zeros_like(qt)
            parts += [jnp.where(low_head, qt, zero), jnp.where(low_head, zero, qt)]
        lhs = jnp.concatenate(parts, axis=0)
        s = lax.dot_general(lhs, kb, (((1,), (1,)), ((), ())), preferred_element_type=F32)
        outs = []
        for hh in range(per_kv):
            sink = sink_ref[g * per_kv + hh] * LOG2_E
            sh = s[hh * blk:(hh + 1) * blk]
            s_loc = jnp.where(valid, sh[:, :3 * blk], NEG_INF)
            s_ctx = sh[:, 3 * blk:]
            m = jnp.maximum(jnp.maximum(jnp.max(s_loc, axis=-1, keepdims=True),
                                        jnp.max(s_ctx, axis=-1, keepdims=True)), sink)
            pr = jnp.concatenate([jnp.exp2(s_loc - m), jnp.exp2(s_ctx - m)], axis=1).astype(BF16)
            pv = _dot(pr, vbs[hh % 2])
            den = pltpu.roll(pv, HEAD_DIM, 1) + jnp.exp2(sink - m)
            outs.append(pv / den)
        for p in range(per_kv // 2):
            t = g * (per_kv // 2) + p
            o_ref[:, t * LANES:(t + 1) * LANES] = jnp.where(
                low_head, outs[2 * p], outs[2 * p + 1]).astype(BF16)


def _attn_call(sink, q, kd, vd, kx, vx):
    n = q.shape[0]
    blk = ATTN_BLOCK
    nb = n // blk
    nctx = kx.shape[1]
    k_spec = lambda f: pl.BlockSpec((N_KV_HEADS, blk, LANES), f)
    v_spec = lambda f: pl.BlockSpec((2 * N_KV_HEADS, blk, LANES), f)
    prev = lambda b: (0, jnp.maximum(b - 1, 0), 0)
    cur = lambda b: (0, b, 0)
    nxt = lambda b: (0, jnp.minimum(b + 1, nb - 1), 0)
    return pl.pallas_call(
        _attn_body,
        grid=(nb,),
        in_specs=[pl.BlockSpec(memory_space=pltpu.SMEM),
                  pl.BlockSpec((blk, Q_COLS), lambda b: (b, 0)),
                  k_spec(prev), k_spec(cur), k_spec(nxt),
                  v_spec(prev), v_spec(cur), v_spec(nxt),
                  pl.BlockSpec((N_KV_HEADS, nctx, LANES), lambda b: (0, 0, 0)),
                  pl.BlockSpec((2 * N_KV_HEADS, nctx, LANES), lambda b: (0, 0, 0))],
        out_specs=pl.BlockSpec((blk, Q_COLS), lambda b: (b, 0)),
        out_shape=jax.ShapeDtypeStruct((n, Q_COLS), BF16),
        compiler_params=_cparams(("arbitrary",), VMEM_LIMIT),
        name="attn",
    )(sink, q, kd, kd, kd, vd, vd, vd, kx, vx)


def _proj_body(a_ref, x_ref, mods_ref, w_ref, o_ref):
    o_ref[...] = x_ref[...] + mods_ref[3:4, :] * _dot(a_ref[...], w_ref[...])


def _proj_call(a, x, mods, w_bf16):
    n = x.shape[0]
    tm = min(ROW_TILE, n)
    return pl.pallas_call(
        _proj_body,
        grid=(n // tm,),
        in_specs=[pl.BlockSpec((tm, a.shape[1]), lambda i: (i, 0)),
                  pl.BlockSpec((tm, D_MODEL), lambda i: (i, 0)),
                  pl.BlockSpec((SUBLANES, D_MODEL), lambda i: (0, 0)),
                  pl.BlockSpec(w_bf16.shape, lambda i: (0, 0))],
        out_specs=pl.BlockSpec((tm, D_MODEL), lambda i: (i, 0)),
        out_shape=jax.ShapeDtypeStruct(x.shape, F32),
        compiler_params=_cparams(("arbitrary",), VMEM_LIMIT),
        name="proj",
    )(a, x, mods, w_bf16)


def _rope_lane_tables(length):
    rows = length // GRID_W
    row_pos = jnp.repeat(jnp.arange(rows, dtype=F32), GRID_W)
    col_pos = jnp.tile(jnp.arange(GRID_W, dtype=F32), rows)
    axis_dim = HEAD_DIM // 2
    inv_freq = ROPE_BASE ** (-jnp.arange(0, axis_dim, 2, dtype=F32) / axis_dim)
    ang_r = row_pos[:, None] * inv_freq[None, :]
    ang_c = col_pos[:, None] * inv_freq[None, :]
    cos_h = jnp.concatenate([jnp.cos(ang_r), jnp.cos(ang_r), jnp.cos(ang_c), jnp.cos(ang_c)], axis=1)
    sin_h = jnp.concatenate([-jnp.sin(ang_r), jnp.sin(ang_r), -jnp.sin(ang_c), jnp.sin(ang_c)], axis=1)
    reps = LANES // HEAD_DIM
    return jnp.tile(cos_h, (1, reps)), jnp.tile(sin_h, (1, reps))


def _attention_layer(x, ctx, mods_x, mods_c, wqkv, wo, q_gain, k_gain, sink):
    L = x.shape[0]
    nctx = ctx.shape[0]
    wqkv_b = wqkv.astype(BF16)
    head = np.arange(LANES) // HEAD_DIM
    bd = jnp.asarray((head[:, None] == head[None, :]).astype(np.float32) / HEAD_DIM).astype(BF16)
    reps = LANES // HEAD_DIM
    gains = jnp.zeros((SUBLANES, LANES), F32)
    gains = gains.at[0].set(jnp.tile(q_gain, reps) * (HEAD_DIM ** -0.5 * LOG2_E))
    gains = gains.at[1].set(jnp.tile(k_gain, reps))
    cos_t, sin_t = _rope_lane_tables(L)
    q, kd, vd = _qkv_call(x, mods_x, wqkv_b, bd, gains, cos_t, sin_t)
    ones = jnp.ones((nctx, LANES), F32)
    _, kx, vx = _qkv_call(ctx, mods_c, wqkv_b, bd, gains, ones, jnp.zeros_like(ones))
    o = _attn_call(sink, q, kd, vd, kx, vx)
    return _proj_call(o, x, mods_x, wo.astype(BF16))


def _pool_body(xp_ref, xc_ref, xn_ref, mods_ref, w_ref, o_ref, h_scr):
    i = pl.program_id(0)
    tm = xc_ref.shape[0]
    total = tm * pl.num_programs(0)
    x = xc_ref[...]
    mods = mods_ref[...]
    h_scr[0:POOL_HALO, :] = jnp.where(i > 0, _mod_norm(xp_ref[...], mods), 0.0)
    h_scr[POOL_HALO:POOL_HALO + tm, :] = _mod_norm(x, mods)
    h_scr[POOL_HALO + tm:, :] = jnp.where(i < pl.num_programs(0) - 1, _mod_norm(xn_ref[...], mods), 0.0)
    t = i * tm + lax.broadcasted_iota(jnp.int32, (tm, 1), 0)
    gd = POOL_GROUP_DIM
    for g, win in enumerate(POOL_WINDOWS):
        sl = slice(g * gd, (g + 1) * gd)
        half = win // 2
        tot = h_scr[POOL_HALO - half:POOL_HALO - half + tm, sl]
        for s in range(-half + 1, half):
            tot = tot + h_scr[POOL_HALO + s:POOL_HALO + s + tm, sl]
        lo = jnp.maximum(t - half, 0)
        hi = jnp.minimum(t + half - 1, total - 1)
        cnt = (hi - lo + 1).astype(F32)
        pooled = (tot / cnt - h_scr[POOL_HALO:POOL_HALO + tm, sl]).astype(BF16)
        y = _dot(pooled, w_ref[g]) * mods[4:5, sl]
        o_ref[:, sl] = x[:, sl] + mods[3:4, sl] * y


def _pool_layer(x, mods, w_bf16):
    n = x.shape[0]
    tm = min(ROW_TILE, n)
    r = tm // POOL_HALO
    last = n // POOL_HALO - 1
    return pl.pallas_call(
        _pool_body,
        grid=(n // tm,),
        in_specs=[pl.BlockSpec((POOL_HALO, D_MODEL), lambda i: (jnp.maximum(i * r - 1, 0), 0)),
                  pl.BlockSpec((tm, D_MODEL), lambda i: (i, 0)),
                  pl.BlockSpec((POOL_HALO, D_MODEL), lambda i: (jnp.minimum((i + 1) * r, last), 0)),
                  pl.BlockSpec((SUBLANES, D_MODEL), lambda i: (0, 0)),
                  pl.BlockSpec(w_bf16.shape, lambda i: (0, 0, 0))],
        out_specs=pl.BlockSpec((tm, D_MODEL), lambda i: (i, 0)),
        out_shape=jax.ShapeDtypeStruct(x.shape, F32),
        scratch_shapes=[pltpu.VMEM((tm + 2 * POOL_HALO, D_MODEL), F32)],
        compiler_params=_cparams(("arbitrary",), VMEM_LIMIT),
        name="pool",
    )(x, x, x, mods, w_bf16)


def _router_body(x_ref, mods_ref, r_ref, h_ref, route_ref, cnt_ref):
    h = _mod_norm(x_ref[...], mods_ref[...])
    h_hi = h.astype(BF16)
    h_ref[...] = h_hi
    h_lo = (h - h_hi.astype(F32)).astype(BF16)
    r = r_ref[...]
    r_hi = r.astype(BF16)
    r_lo = (r - r_hi.astype(F32)).astype(BF16)
    logits = _dot(h_hi, r_hi) + (_dot(h_lo, r_hi) + _dot(h_hi, r_lo))
    lane = lax.broadcasted_iota(jnp.int32, logits.shape, 1)
    logits = jnp.where(lane < N_EXPERTS, logits, -jnp.inf)
    m1 = jnp.max(logits, axis=-1, keepdims=True)
    i1 = jnp.min(jnp.where(logits == m1, lane, LANES), axis=-1, keepdims=True)
    rest = jnp.where(lane == i1, -jnp.inf, logits)
    m2 = jnp.max(rest, axis=-1, keepdims=True)
    i2 = jnp.min(jnp.where(rest == m2, lane, LANES), axis=-1, keepdims=True)
    e = jnp.exp(m2 - m1)
    w1 = 1.0 / (1.0 + e)
    w2 = e / (1.0 + e)
    tm = h.shape[0]
    oh = jnp.where((lane == i1) | (lane == i2 + N_EXPERTS), 1.0, 0.0)
    before = (lax.broadcasted_iota(jnp.int32, (tm, tm), 1)
              < lax.broadcasted_iota(jnp.int32, (tm, tm), 0))
    prior = _dot(jnp.where(before, 1.0, 0.0).astype(BF16), oh.astype(BF16))
    cnt = jnp.sum(oh, axis=0, keepdims=True)
    r1 = jnp.sum(jnp.where(lane == i1, prior, 0.0), axis=-1, keepdims=True)
    r2 = (jnp.sum(jnp.where(lane == i2 + N_EXPERTS, prior, 0.0), axis=-1, keepdims=True)
          + jnp.sum(jnp.where(lane == i2, cnt, 0.0), axis=-1, keepdims=True))
    lane8 = lane[0:SUBLANES, :]
    cnt8 = jnp.broadcast_to(cnt, (SUBLANES, LANES))
    both = cnt8 + pltpu.roll(cnt8, LANES - N_EXPERTS, 1)
    seg_len = jnp.where(lane8 < N_EXPERTS, jnp.ceil(both * (1.0 / SEG_ALIGN)) * SEG_ALIGN, 0.0)
    incl = seg_len
    shift = 1
    while shift < N_EXPERTS:
        incl = incl + jnp.where(lane8 >= shift, pltpu.roll(incl, shift, 1), 0.0)
        shift *= 2
    seg = incl - seg_len
    p1 = r1 + jnp.sum(jnp.where(lane == i1, seg[0:1, :], 0.0), axis=-1, keepdims=True)
    p2 = r2 + jnp.sum(jnp.where(lane == i2, seg[0:1, :], 0.0), axis=-1, keepdims=True)
    col = lax.broadcasted_iota(jnp.int32, route_ref.shape, 1)
    vals = (i1.astype(F32), i2.astype(F32), w1, w2, p1, p2)
    out = jnp.zeros(route_ref.shape, F32)
    for k, v in enumerate(vals):
        out = jnp.where(col == k, v, out)
    route_ref[...] = out
    row = lax.broadcasted_iota(jnp.int32, cnt_ref.shape[1:], 0)
    cnt_ref[0] = jnp.where(row == 0, seg_len, jnp.where(row == 1, seg, 0.0))


def _router_call(x, mods, router_pad):
    n = x.shape[0]
    tm = min(ROUTE_TILE, n)
    return pl.pallas_call(
        _router_body,
        grid=(n // tm,),
        in_specs=[pl.BlockSpec((tm, D_MODEL), lambda i: (i, 0)),
                  pl.BlockSpec((SUBLANES, D_MODEL), lambda i: (0, 0)),
                  pl.BlockSpec((D_MODEL, LANES), lambda i: (0, 0))],
        out_specs=[pl.BlockSpec((tm, D_MODEL), lambda i: (i, 0)),
                   pl.BlockSpec((tm, SUBLANES), lambda i: (i, 0)),
                   pl.BlockSpec((1, SUBLANES, LANES), lambda i: (i, 0, 0))],
        out_shape=[jax.ShapeDtypeStruct((n, D_MODEL), BF16),
                   jax.ShapeDtypeStruct((n, SUBLANES), F32),
                   jax.ShapeDtypeStruct((n // tm, SUBLANES, LANES), F32)],
        compiler_params=_cparams(("arbitrary",), VMEM_LIMIT),
        name="router",
    )(x, mods, router_pad)


def _segment_copies(i, seg_ref, dst_ref, len_ref, tile_buf, sorted_hbm, sem, to_hbm):
    for e in range(N_EXPERTS):
        length = len_ref[i * N_EXPERTS + e]
        seg = seg_ref[i * N_EXPERTS + e]
        dst = dst_ref[i * N_EXPERTS + e]
        size = SEG_ALIGN
        while size <= tile_buf.shape[0] // 2:
            done = length & ~(2 * size - 1)
            a = tile_buf.at[pl.ds(pl.multiple_of(seg + done, SEG_ALIGN), size)]
            b = sorted_hbm.at[pl.ds(pl.multiple_of(dst + done, SEG_ALIGN), size)]
            copy = pltpu.make_async_copy(a, b, sem) if to_hbm else pltpu.make_async_copy(b, a, sem)
            yield (length & size) != 0, copy
            size *= 2


def _scatter_body(seg_ref, dst_ref, len_ref, h_ref, prow_ref, xs_in_ref, xs_ref, sbuf, sems):
    del xs_in_ref
    i = pl.program_id(0)
    last = pl.num_programs(0) - 1
    slot = i % 2
    cap, tm = sbuf.shape[1], h_ref.shape[0]
    r = lax.broadcasted_iota(jnp.int32, (cap, tm), 0)
    hit = (r == prow_ref[0:1, :]) | (r == prow_ref[1:2, :])
    sel = jnp.where(hit, 1.0, 0.0).astype(BF16)
    sbuf[slot] = _dot(sel, h_ref[...]).astype(BF16)

    def copies(tile, s):
        return _segment_copies(tile, seg_ref, dst_ref, len_ref, sbuf.at[s], xs_ref, sems.at[s], True)

    for pred, copy in copies(i, slot):
        pl.when(pred)(copy.start)
    for pred, copy in copies(jnp.maximum(i - 1, 0), 1 - slot):
        pl.when(pred & (i > 0))(copy.wait)
    for pred, copy in copies(i, slot):
        pl.when(pred & (i == last))(copy.wait)


def _route_cap(tm):
    return 2 * tm + N_EXPERTS * SEG_ALIGN


def _scatter_call(seg, dst, seg_len, h, prow, n_rows):
    n = h.shape[0]
    tm = min(ROUTE_TILE, n)
    zeros = jnp.zeros((n_rows, D_MODEL), BF16)
    return pl.pallas_call(
        _scatter_body,
        grid_spec=pltpu.PrefetchScalarGridSpec(
            num_scalar_prefetch=3,
            grid=(n // tm,),
            in_specs=[pl.BlockSpec((tm, D_MODEL), lambda i, *_: (i, 0)),
                      pl.BlockSpec((SUBLANES, tm), lambda i, *_: (i, 0)),
                      pl.BlockSpec(memory_space=pl.ANY)],
            out_specs=pl.BlockSpec(memory_space=pl.ANY),
            scratch_shapes=[pltpu.VMEM((2, _route_cap(tm), D_MODEL), BF16),
                            pltpu.SemaphoreType.DMA((2,))]),
        out_shape=jax.ShapeDtypeStruct((n_rows, D_MODEL), BF16),
        input_output_aliases={5: 0},
        compiler_params=_cparams(("arbitrary",), VMEM_LIMIT),
        name="moe_scatter",
    )(seg, dst, seg_len, h, prow, zeros)


def _moe_body(te_ref, tv_ref, tf_ref, xs_ref, wi_hbm, wo_hbm, ys_ref, wi_res, wo_res, stage_i,
              stage_o, act, sems):
    i = pl.program_id(0)

    @pl.when(tf_ref[i] > 0)
    def _():
        _load_swiglu_weights(te_ref[i], wi_hbm, wo_hbm, wi_res, wo_res, stage_i, stage_o, sems)

    @pl.when(tv_ref[i] > 0)
    def _():
        ys_ref[...] = _swiglu_tile(xs_ref[...], wi_res, wo_res, act).astype(BF16)

    @pl.when(tv_ref[i] == 0)
    def _():
        ys_ref[...] = jnp.zeros(ys_ref.shape, BF16)


def _moe_call(tile_expert, tile_live, tile_first, xs, wi_all, wo_all):
    n_rows = xs.shape[0]
    return pl.pallas_call(
        _moe_body,
        grid_spec=pltpu.PrefetchScalarGridSpec(
            num_scalar_prefetch=3,
            grid=(n_rows // MOE_TILE,),
            in_specs=[pl.BlockSpec((MOE_TILE, D_MODEL), lambda i, *_: (i, 0)),
                      pl.BlockSpec(memory_space=pl.ANY),
                      pl.BlockSpec(memory_space=pl.ANY)],
            out_specs=pl.BlockSpec((MOE_TILE, D_MODEL), lambda i, *_: (i, 0)),
            scratch_shapes=_swiglu_scratch(MOE_TILE)),
        out_shape=jax.ShapeDtypeStruct((n_rows, D_MODEL), BF16),
        compiler_params=_cparams(("arbitrary",), VMEM_LIMIT),
        name="moe_ffn",
    )(tile_expert, tile_live, tile_first, xs, wi_all, wo_all)


def _combine_body(seg_ref, dst_ref, len_ref, x_ref, route_ref, pcol_ref, mods_ref, ys_ref, o_ref,
                  ybuf, sems):
    i = pl.program_id(0)
    last = pl.num_programs(0) - 1
    slot = i % 2
    tm, cap = x_ref.shape[0], ybuf.shape[1]

    def copies(tile, s):
        return _segment_copies(tile, seg_ref, dst_ref, len_ref, ybuf.at[s], ys_ref, sems.at[s], False)

    @pl.when(i == 0)
    def _():
        ybuf[...] = jnp.zeros(ybuf.shape, BF16)

    for pred, copy in copies(0, 0):
        pl.when(pred & (i == 0))(copy.start)
    for pred, copy in copies(jnp.minimum(i + 1, last), 1 - slot):
        pl.when(pred & (i < last))(copy.start)
    for pred, copy in copies(i, slot):
        pl.when(pred)(copy.wait)
    c = lax.broadcasted_iota(jnp.int32, (tm, cap), 1)
    pcol = pcol_ref[...]
    y = ybuf[slot]
    ya = _dot(jnp.where(c == pcol[:, 0:1], 1.0, 0.0).astype(BF16), y)
    yb = _dot(jnp.where(c == pcol[:, 1:2], 1.0, 0.0).astype(BF16), y)
    route = route_ref[...]
    o_ref[...] = x_ref[...] + mods_ref[3:4, :] * (route[:, 2:3] * ya + route[:, 3:4] * yb)


def _combine_call(seg, dst, seg_len, x, route, pcol, mods, ys):
    n = x.shape[0]
    tm = min(ROUTE_TILE, n)
    return pl.pallas_call(
        _combine_body,
        grid_spec=pltpu.PrefetchScalarGridSpec(
            num_scalar_prefetch=3,
            grid=(n // tm,),
            in_specs=[pl.BlockSpec((tm, D_MODEL), lambda i, *_: (i, 0)),
                      pl.BlockSpec((tm, SUBLANES), lambda i, *_: (i, 0)),
                      pl.BlockSpec((tm, SUBLANES), lambda i, *_: (i, 0)),
                      pl.BlockSpec((SUBLANES, D_MODEL), lambda i, *_: (0, 0)),
                      pl.BlockSpec(memory_space=pl.ANY)],
            out_specs=pl.BlockSpec((tm, D_MODEL), lambda i, *_: (i, 0)),
            scratch_shapes=[pltpu.VMEM((2, _route_cap(tm), D_MODEL), BF16),
                            pltpu.SemaphoreType.DMA((2,))]),
        out_shape=jax.ShapeDtypeStruct(x.shape, F32),
        compiler_params=_cparams(("arbitrary",), VMEM_LIMIT),
        name="moe_combine",
    )(seg, dst, seg_len, x, route, pcol, mods, ys)


def _routing_tables(route, counts, tm, n_tiles):
    nt = counts.shape[0]
    seg_len = counts[:, 0, :N_EXPERTS].astype(jnp.int32)
    seg = counts[:, 1, :N_EXPERTS].astype(jnp.int32)
    tiles = (jnp.sum(seg_len, axis=0) + MOE_TILE - 1) // MOE_TILE
    tile_end = jnp.cumsum(tiles)
    start = (tile_end - tiles) * MOE_TILE
    dst = start[None, :] + jnp.cumsum(seg_len, axis=0) - seg_len
    t = jnp.arange(n_tiles, dtype=jnp.int32)
    expert = jnp.sum((t[:, None] >= tile_end[None, :]).astype(jnp.int32), axis=1)
    live = (t < tile_end[-1]).astype(jnp.int32)
    last_expert = jnp.sum((tile_end[-1] - 1 >= tile_end).astype(jnp.int32))
    expert = jnp.where(live > 0, expert, last_expert).astype(jnp.int32)
    prev = jnp.concatenate([jnp.full((1,), -1, jnp.int32), expert[:-1]])
    first = (live * (expert != prev)).astype(jnp.int32)
    p12 = route[:, 4:6].astype(jnp.int32)
    pcol = jnp.concatenate([p12, jnp.zeros((p12.shape[0], SUBLANES - 2), jnp.int32)], axis=1)
    prow = jnp.concatenate([p12.reshape(nt, tm, 2).transpose(0, 2, 1),
                            jnp.full((nt, SUBLANES - 2, tm), -1, jnp.int32)], axis=1)
    return (seg.reshape(-1), dst.reshape(-1).astype(jnp.int32), seg_len.reshape(-1),
            expert, live, first, pcol, prow.reshape(nt * SUBLANES, tm))


def _moe_layer(x, mods, router, wi_all, wo_all, layer):
    n = x.shape[0]
    tm = min(ROUTE_TILE, n)
    router_pad = jnp.zeros((D_MODEL, LANES), F32).at[:, :N_EXPERTS].set(router)
    h, route, counts = _router_call(x, mods, router_pad)
    max_rows = 2 * n + (n // tm) * N_EXPERTS * (SEG_ALIGN - 1)
    n_tiles = -(-max_rows // MOE_TILE) + N_EXPERTS
    seg, dst, seg_len, tile_expert, tile_live, tile_first, pcol, prow = _routing_tables(
        route, counts, tm, n_tiles)
    xs = _scatter_call(seg, dst, seg_len, h, prow, n_tiles * MOE_TILE)
    ys = _moe_call(tile_expert + layer * N_EXPERTS, tile_live, tile_first, xs, wi_all, wo_all)
    return _combine_call(seg, dst, seg_len, x, route, pcol, mods, ys)


def _mods(gain, shift, scale, gate, extra=None):
    rows = [gain, shift, scale, gate, extra if extra is not None else jnp.zeros_like(gain)]
    m = jnp.stack(rows, axis=0)
    return jnp.concatenate([m, jnp.zeros((SUBLANES - m.shape[0], m.shape[1]), F32)], axis=0)


def kernel(x, c, ctx, c_ctx, ada_w, ada_b, norm_mix, norm_ffn, fnet_w, attn_wqkv, attn_q_gain,
           attn_k_gain, attn_sink, attn_wo, pool_w, pool_scale, ffn_wi, ffn_wo, moe_router,
           moe_wi, moe_wo):
    assert x.shape[0] == 1 and x.shape[2] == D_MODEL
    depth = ada_w.shape[0]
    xs = x[0]
    cs = ctx[0]
    cc = jnp.zeros((SUBLANES, D_MODEL), F32).at[0].set(c[0]).at[1].set(c_ctx)
    ada = _ada_call(cc, ada_w, ada_b)

    attn_layers = [i for i in range(depth) if i % N_MIXERS == 1]
    last_ctx_read = attn_layers[-1] if attn_layers else -1
    n_side = math.isqrt(xs.shape[0])
    assert n_side * n_side == xs.shape[0] and n_side % SUBLANES == 0
    seq_tables = _sequence_tables(n_side)
    ch = _channel_table()
    moe_wi_all = moe_wi.reshape((-1,) + moe_wi.shape[2:])
    moe_wo_all = moe_wo.reshape((-1,) + moe_wo.shape[2:])

    for i in range(depth):
        mixer = i % N_MIXERS
        j = i // N_MIXERS
        f = i // 2
        ctx_full = i < last_ctx_read
        ctx_live = i <= last_ctx_read
        sh1, sc1, g1, sh2, sc2, g2 = [ada[i, 0, k * D_MODEL:(k + 1) * D_MODEL] for k in range(6)]
        csh1, csc1, cg1, csh2, csc2, cg2 = [ada[i, 1, k * D_MODEL:(k + 1) * D_MODEL] for k in range(6)]
        extra = pool_scale[j] if mixer == 2 else None
        m1 = _mods(norm_mix[i], sh1, sc1, g1, extra)
        m2 = _mods(norm_ffn[i], sh2, sc2, g2)
        cm1 = _mods(norm_mix[i], csh1, csc1, cg1, extra)
        cm2 = _mods(norm_ffn[i], csh2, csc2, cg2)

        if mixer == 0:
            wb = fnet_w[j].astype(BF16)
            xs = _fourier_layer(xs, m1, wb, seq_tables, ch)
            if ctx_full:
                cs = _ctx_fourier_layer(cs, cm1, wb, ch)
        elif mixer == 1:
            assert ctx_live and not ctx_full
            xs = _attention_layer(xs, cs, m1, cm1, attn_wqkv[j], attn_wo[j], attn_q_gain[j],
                                  attn_k_gain[j], attn_sink[j])
        else:
            wb = pool_w[j].astype(BF16)
            xs = _pool_layer(xs, m1, wb)
            if ctx_full:
                cs = _pool_layer(cs, cm1, wb)

        if i % 2 == 0:
            xs = _ffn_layer(xs, m2, ffn_wi, ffn_wo, f)
            if ctx_full:
                cs = _ffn_layer(cs, cm2, ffn_wi, ffn_wo, f)
        else:
            xs = _moe_layer(xs, m2, moe_router[f], moe_wi_all, moe_wo_all, f)
            if ctx_full:
                cs = _moe_layer(cs, cm2, moe_router[f], moe_wi_all, moe_wo_all, f)
    return xs[None]
```

```python
import functools
import math

import numpy as np
import jax
import jax.numpy as jnp
from jax import lax
from jax.experimental import pallas as pl
from jax.experimental.pallas import tpu as pltpu

F32 = jnp.float32
BF16 = jnp.bfloat16

D_MODEL = 1024
GRID_W = 64
N_MIXERS = 3
FNET_GROUPS = 4
FNET_GROUP_DIM = D_MODEL // FNET_GROUPS
N_HEADS = 16
N_KV_HEADS = 4
HEAD_DIM = 64
Q_COLS = N_HEADS * HEAD_DIM
KV_COLS = N_KV_HEADS * HEAD_DIM
WINDOW = 128
ATTN_BLOCK = 128
ROPE_BASE = 10000.0
POOL_WINDOWS = (2, 4, 8, 16)
POOL_GROUP_DIM = D_MODEL // len(POOL_WINDOWS)
POOL_HALO = 8
D_FF = 3584
N_EXPERTS = 8
NORM_EPS = 1e-6
NEG_INF = -1e30
LOG2_E = 1.4426950408889634

LANES = 128
SUBLANES = 8
VMEM_LIMIT = 56 * 1024 * 1024

ROW_TILE = 512
FF_CHUNK = 512
MOE_TILE = 256
W_CHUNK = 512
ROUTE_TILE = 512
SEG_ALIGN = 16
N_COND = 2
POS_RADIX = 2048.0


def _cparams(sem, vmem=None):
    return pltpu.CompilerParams(dimension_semantics=sem, vmem_limit_bytes=vmem)


def _mod_norm(x, mods):
    ms = jnp.mean(x * x, axis=-1, keepdims=True)
    y = x * lax.rsqrt(ms + NORM_EPS) * mods[0:1, :]
    return y * (1.0 + mods[2:3, :]) + mods[1:2, :]


def _dot(a, b):
    return jnp.dot(a, b, preferred_element_type=F32)


def _ada_body(cc_ref, w_ref, b_ref, o_ref):
    a = cc_ref[...]
    a = a / (1.0 + jnp.exp(-a))
    row = lax.broadcasted_iota(jnp.int32, (SUBLANES, LANES), 0)
    for t in range(w_ref.shape[-1] // LANES):
        sl = slice(t * LANES, (t + 1) * LANES)
        wt = w_ref[0, :, sl]
        tile = jnp.zeros((SUBLANES, LANES), F32)
        for r in range(N_COND):
            s = jnp.sum(wt * a[r], axis=0, keepdims=True) + b_ref[0, :, sl]
            tile = jnp.where(row == r, s, tile)
        o_ref[0, :, sl] = tile


def _ada_call(cc, ada_w, ada_b):
    depth, d, n6 = ada_w.shape
    tn = n6 // 4
    cc_lanes = jnp.broadcast_to(cc[:, :, None], (N_COND, d, LANES))
    return pl.pallas_call(
        _ada_body,
        grid=(depth, n6 // tn),
        in_specs=[pl.BlockSpec((N_COND, d, LANES), lambda l, j: (0, 0, 0)),
                  pl.BlockSpec((1, d, tn), lambda l, j: (l, 0, j)),
                  pl.BlockSpec((1, 1, tn), lambda l, j: (l, 0, j))],
        out_specs=pl.BlockSpec((1, SUBLANES, tn), lambda l, j: (l, 0, j)),
        out_shape=jax.ShapeDtypeStruct((depth, SUBLANES, n6), F32),
        compiler_params=_cparams(("arbitrary", "arbitrary"), VMEM_LIMIT),
        name="ada",
    )(cc_lanes, ada_w, ada_b.reshape(depth, 1, n6))


def _cos_sin(n, period):
    k = np.arange(n, dtype=np.float64)
    ang = 2.0 * np.pi * np.outer(k, k) / period
    return np.cos(ang).astype(np.float32), np.sin(ang).astype(np.float32)


def _channel_table():
    c, s = _cos_sin(FNET_GROUP_DIM, FNET_GROUP_DIM)
    return (jnp.concatenate([jnp.asarray(c), jnp.asarray(s)], axis=0)
            * (FNET_GROUP_DIM ** -0.5)).astype(BF16)


def _sequence_tables(n):
    c, s = _cos_sin(n, n)
    eye = np.eye(SUBLANES, dtype=np.float32)
    rows = n * SUBLANES
    f = np.stack([c, -s]) * np.float32(1.0 / n)
    ka = (f[:, None, :, :, None] * eye[None, :, None, None, :]).reshape(2 * rows, rows)
    ck = (c[:, None, :, None] * eye[None, :, None, :]).reshape(rows, rows)
    sk = (s[:, None, :, None] * eye[None, :, None, :]).reshape(rows, rows)
    cs = np.concatenate([ck, sk], axis=1)
    tc, ts = _cos_sin(n, n * n)
    oct_ = n // SUBLANES
    tc = np.ascontiguousarray(np.broadcast_to(tc.reshape(oct_, rows, 1), (oct_, rows, LANES)))
    ts = np.ascontiguousarray(np.broadcast_to(ts.reshape(oct_, rows, 1), (oct_, rows, LANES)))
    return jnp.asarray(ka).astype(BF16), jnp.asarray(cs).astype(BF16), jnp.asarray(tc), jnp.asarray(ts)


def _fft_a_body(x_ref, mods_ref, ka_ref, tc_ref, ts_ref, zr_ref, zi_ref):
    rows = ka_ref.shape[1]
    x = x_ref[...].reshape(rows, D_MODEL)
    h = _mod_norm(x, mods_ref[...]).astype(BF16)
    ka = ka_ref[...]
    tc, ts = tc_ref[0], ts_ref[0]
    gw = FNET_GROUP_DIM
    for g in range(D_MODEL // gw):
        z = _dot(ka, h[:, g * gw:(g + 1) * gw])
        zr, zi = z[:rows], z[rows:]
        for t in range(gw // LANES):
            a = zr[:, t * LANES:(t + 1) * LANES]
            b = zi[:, t * LANES:(t + 1) * LANES]
            sl = slice(g * gw + t * LANES, g * gw + (t + 1) * LANES)
            zr_ref[:, :, sl] = (a * tc + b * ts).reshape(SUBLANES, rows // SUBLANES, LANES)
            zi_ref[:, :, sl] = (b * tc - a * ts).reshape(SUBLANES, rows // SUBLANES, LANES)


def _fft_b_body(zr_ref, zi_ref, x_ref, mods_ref, cs_ref, ch_ref, w_ref, o_ref):
    rows = cs_ref.shape[0]
    cs = cs_ref[...]
    gw = FNET_GROUP_DIM
    ys = []
    for g in range(D_MODEL // gw):
        zr = zr_ref[:, :, g * gw:(g + 1) * gw].reshape(rows, gw).astype(BF16)
        zi = zi_ref[:, :, g * gw:(g + 1) * gw].reshape(rows, gw).astype(BF16)
        xr = _dot(cs, jnp.concatenate([zr, zi], axis=0))
        xi = _dot(cs, jnp.concatenate([zi, -zr], axis=0))
        ys.append(_dot(jnp.concatenate([xr, xi], axis=1).astype(BF16), ch_ref[...]).astype(BF16))
    y = _dot(jnp.concatenate(ys, axis=1), w_ref[...])
    x = x_ref[...].reshape(rows, D_MODEL)
    o_ref[...] = (x + mods_ref[3:4, :] * y).reshape(o_ref.shape)


def _fourier_layer(x, mods, w_bf16, seq_tables, ch):
    L = x.shape[0]
    n = math.isqrt(L)
    rows = n * SUBLANES
    oct_ = n // SUBLANES
    ka, cs, tc, ts = seq_tables
    x3 = x.reshape(n, n, D_MODEL)
    gw = FNET_GROUP_DIM
    zshape = jax.ShapeDtypeStruct((n, n, D_MODEL), F32)
    once = dict(pipeline_mode=pl.Buffered(1))
    zr, zi = pl.pallas_call(
        _fft_a_body,
        grid=(oct_,),
        in_specs=[pl.BlockSpec((n, SUBLANES, D_MODEL), lambda o: (0, o, 0)),
                  pl.BlockSpec((SUBLANES, D_MODEL), lambda o: (0, 0)),
                  pl.BlockSpec((2 * rows, rows), lambda o: (0, 0), **once),
                  pl.BlockSpec((1, rows, LANES), lambda o: (o, 0, 0)),
                  pl.BlockSpec((1, rows, LANES), lambda o: (o, 0, 0))],
        out_specs=[pl.BlockSpec((SUBLANES, n, D_MODEL), lambda o: (o, 0, 0)),
                   pl.BlockSpec((SUBLANES, n, D_MODEL), lambda o: (o, 0, 0))],
        out_shape=[zshape, zshape],
        compiler_params=_cparams(("arbitrary",), VMEM_LIMIT),
        name="fft_a",
    )(x3, mods, ka, tc, ts)
    out = pl.pallas_call(
        _fft_b_body,
        grid=(oct_,),
        in_specs=[pl.BlockSpec((n, SUBLANES, D_MODEL), lambda p: (0, p, 0)),
                  pl.BlockSpec((n, SUBLANES, D_MODEL), lambda p: (0, p, 0)),
                  pl.BlockSpec((n, SUBLANES, D_MODEL), lambda p: (0, p, 0)),
                  pl.BlockSpec((SUBLANES, D_MODEL), lambda p: (0, 0)),
                  pl.BlockSpec((rows, 2 * rows), lambda p: (0, 0), **once),
                  pl.BlockSpec((2 * gw, gw), lambda p: (0, 0), **once),
                  pl.BlockSpec((D_MODEL, D_MODEL), lambda p: (0, 0), **once)],
        out_specs=pl.BlockSpec((n, SUBLANES, D_MODEL), lambda p: (0, p, 0)),
        out_shape=jax.ShapeDtypeStruct((n, n, D_MODEL), F32),
        compiler_params=_cparams(("arbitrary",), VMEM_LIMIT),
        name="fft_b",
    )(zr, zi, x3, mods, cs, ch, w_bf16)
    return out.reshape(L, D_MODEL)


def _ctx_fourier_body(x_ref, mods_ref, f_ref, ch_ref, w_ref, o_ref):
    x = x_ref[...]
    n = x.shape[0]
    h = _mod_norm(x, mods_ref[...]).astype(BF16)
    g = _dot(f_ref[...], h)
    gr, gi = g[:n], g[n:]
    gw = FNET_GROUP_DIM
    ys = []
    for k in range(D_MODEL // gw):
        sl = slice(k * gw, (k + 1) * gw)
        ys.append(_dot(jnp.concatenate([gr[:, sl], gi[:, sl]], axis=1).astype(BF16), ch_ref[...]))
    y = jnp.concatenate(ys, axis=1).astype(BF16)
    o_ref[...] = x + mods_ref[3:4, :] * _dot(y, w_ref[...])


def _ctx_fourier_layer(ctx, mods, w_bf16, ch):
    n = ctx.shape[0]
    c, s = _cos_sin(n, n)
    f = (jnp.concatenate([jnp.asarray(c), -jnp.asarray(s)], axis=0) * (n ** -0.5)).astype(BF16)
    return pl.pallas_call(
        _ctx_fourier_body,
        out_shape=jax.ShapeDtypeStruct(ctx.shape, F32),
        compiler_params=pltpu.CompilerParams(vmem_limit_bytes=VMEM_LIMIT),
        name="ctx_fourier",
    )(ctx, mods, f, ch, w_bf16)


def _load_swiglu_weights(e, wi_hbm, wo_hbm, wi_res, wo_res, stage_i, stage_o, sems):
    n_i = wi_res.shape[1] // W_CHUNK
    total = n_i + wo_res.shape[0] // W_CHUNK

    def copy(c):
        slot = c % 2
        if c < n_i:
            src = wi_hbm.at[e, :, pl.ds(c * W_CHUNK, W_CHUNK)]
            return pltpu.make_async_copy(src, stage_i.at[slot], sems.at[slot])
        src = wo_hbm.at[e, pl.ds((c - n_i) * W_CHUNK, W_CHUNK), :]
        return pltpu.make_async_copy(src, stage_o.at[slot], sems.at[slot])

    copy(0).start()
    for c in range(total):
        if c + 1 < total:
            copy(c + 1).start()
        copy(c).wait()
        if c < n_i:
            wi_res[:, c * W_CHUNK:(c + 1) * W_CHUNK] = stage_i[c % 2].astype(BF16)
        else:
            k = c - n_i
            wo_res[k * W_CHUNK:(k + 1) * W_CHUNK, :] = stage_o[c % 2].astype(BF16)


def _swiglu_tile(h, wi_res, wo_res, act):
    for k in range(D_FF // FF_CHUNK):
        gate = _dot(h, wi_res[:, k * FF_CHUNK:(k + 1) * FF_CHUNK])
        up = _dot(h, wi_res[:, D_FF + k * FF_CHUNK:D_FF + (k + 1) * FF_CHUNK])
        act[:, k * FF_CHUNK:(k + 1) * FF_CHUNK] = (gate / (1.0 + jnp.exp(-gate)) * up).astype(BF16)
    return _dot(act[...], wo_res[...])


def _swiglu_scratch(tm):
    return [pltpu.VMEM((D_MODEL, 2 * D_FF), BF16), pltpu.VMEM((D_FF, D_MODEL), BF16),
            pltpu.VMEM((2, D_MODEL, W_CHUNK), F32), pltpu.VMEM((2, W_CHUNK, D_MODEL), F32),
            pltpu.VMEM((tm, D_FF), BF16), pltpu.SemaphoreType.DMA((2,))]


def _ffn_body(layer, x_ref, mods_ref, wi_hbm, wo_hbm, o_ref, wi_res, wo_res, stage_i, stage_o,
              act, sems):
    @pl.when(pl.program_id(0) == 0)
    def _():
        _load_swiglu_weights(layer, wi_hbm, wo_hbm, wi_res, wo_res, stage_i, stage_o, sems)

    x = x_ref[...]
    h = _mod_norm(x, mods_ref[...]).astype(BF16)
    o_ref[...] = x + mods_ref[3:4, :] * _swiglu_tile(h, wi_res, wo_res, act)


def _ffn_layer(x, mods, wi_all, wo_all, layer):
    n = x.shape[0]
    tm = min(ROW_TILE, n)
    return pl.pallas_call(
        functools.partial(_ffn_body, layer),
        grid=(n // tm,),
        in_specs=[pl.BlockSpec((tm, D_MODEL), lambda i: (i, 0)),
                  pl.BlockSpec((SUBLANES, D_MODEL), lambda i: (0, 0)),
                  pl.BlockSpec(memory_space=pl.ANY),
                  pl.BlockSpec(memory_space=pl.ANY)],
        out_specs=pl.BlockSpec((tm, D_MODEL), lambda i: (i, 0)),
        out_shape=jax.ShapeDtypeStruct(x.shape, F32),
        scratch_shapes=_swiglu_scratch(tm),
        compiler_params=_cparams(("arbitrary",), VMEM_LIMIT),
        name="ffn",
    )(x, mods, wi_all, wo_all)


def _qkv_body(x_ref, mods_ref, w_ref, bd_ref, gains_ref, cos_ref, sin_ref, q_ref, k_ref, v_ref):
    h = _mod_norm(x_ref[...], mods_ref[...]).astype(BF16)
    qkv = _dot(h, w_ref[...])
    bd = bd_ref[...]
    cos, sin = cos_ref[...], sin_ref[...]
    lane = lax.broadcasted_iota(jnp.int32, (1, LANES), 1)
    first_half = (lane % (HEAD_DIM // 2)) < (HEAD_DIM // 4)
    low_head = lane < HEAD_DIM

    def norm_rope(a, gain):
        sq = a * a
        hi = sq.astype(BF16)
        lo = (sq - hi.astype(F32)).astype(BF16)
        ms = _dot(hi, bd) + _dot(lo, bd)
        an = a * lax.rsqrt(ms + NORM_EPS) * gain
        partner = jnp.where(first_half,
                            pltpu.roll(an, LANES - HEAD_DIM // 4, 1),
                            pltpu.roll(an, HEAD_DIM // 4, 1))
        return an * cos + partner * sin

    def dup_heads(a):
        r = pltpu.roll(a, HEAD_DIM, 1)
        return jnp.where(low_head, a, r), jnp.where(low_head, r, a)

    nq = Q_COLS // LANES
    for t in range(nq):
        a = norm_rope(qkv[:, t * LANES:(t + 1) * LANES], gains_ref[0:1, :])
        q_ref[:, t * LANES:(t + 1) * LANES] = a.astype(BF16)
    for t in range(KV_COLS // LANES):
        kt = norm_rope(qkv[:, Q_COLS + t * LANES:Q_COLS + (t + 1) * LANES], gains_ref[1:2, :])
        k0, k1 = dup_heads(kt)
        k_ref[2 * t] = k0.astype(BF16)
        k_ref[2 * t + 1] = k1.astype(BF16)
        v0, v1 = dup_heads(qkv[:, Q_COLS + KV_COLS + t * LANES:Q_COLS + KV_COLS + (t + 1) * LANES])
        for hk, vv in ((2 * t, v0), (2 * t + 1, v1)):
            v_ref[2 * hk] = jnp.where(low_head, vv, 1.0).astype(BF16)
            v_ref[2 * hk + 1] = jnp.where(low_head, 1.0, vv).astype(BF16)


def _qkv_call(x, mods, wqkv_bf16, bd, gains, cos_t, sin_t):
    n = x.shape[0]
    tm = min(ROW_TILE, n)
    ncol = Q_COLS + 2 * KV_COLS
    return pl.pallas_call(
        _qkv_body,
        grid=(n // tm,),
        in_specs=[pl.BlockSpec((tm, D_MODEL), lambda i: (i, 0)),
                  pl.BlockSpec((SUBLANES, D_MODEL), lambda i: (0, 0)),
                  pl.BlockSpec((D_MODEL, ncol), lambda i: (0, 0)),
                  pl.BlockSpec((LANES, LANES), lambda i: (0, 0)),
                  pl.BlockSpec((SUBLANES, LANES), lambda i: (0, 0)),
                  pl.BlockSpec((tm, LANES), lambda i: (i, 0)),
                  pl.BlockSpec((tm, LANES), lambda i: (i, 0))],
        out_specs=[pl.BlockSpec((tm, Q_COLS), lambda i: (i, 0)),
                   pl.BlockSpec((N_KV_HEADS, tm, LANES), lambda i: (0, i, 0)),
                   pl.BlockSpec((2 * N_KV_HEADS, tm, LANES), lambda i: (0, i, 0))],
        out_shape=[jax.ShapeDtypeStruct((n, Q_COLS), BF16),
                   jax.ShapeDtypeStruct((N_KV_HEADS, n, LANES), BF16),
                   jax.ShapeDtypeStruct((2 * N_KV_HEADS, n, LANES), BF16)],
        compiler_params=_cparams(("arbitrary",), VMEM_LIMIT),
        name="qkv",
    )(x, mods, wqkv_bf16, bd, gains, cos_t, sin_t)


def _attn_body(sink_ref, q_ref, kp_ref, kc_ref, kn_ref, vp_ref, vc_ref, vn_ref,
               kx_ref, vx_ref, o_ref):
    b = pl.program_id(0)
    nb = pl.num_programs(0)
    blk = ATTN_BLOCK
    lane = lax.broadcasted_iota(jnp.int32, (1, LANES), 1)
    low_head = lane < HEAD_DIM
    qi = lax.broadcasted_iota(jnp.int32, (blk, 3 * blk), 0)
    kj = lax.broadcasted_iota(jnp.int32, (blk, 3 * blk), 1)
    valid = (kj >= qi) & (kj <= qi + 2 * WINDOW)
    valid = valid & ((kj >= blk) | (b > 0)) & ((kj < 2 * blk) | (b < nb - 1))
    per_kv = N_HEADS // N_KV_HEADS
    for g in range(N_KV_HEADS):
        kb = jnp.concatenate([kp_ref[g], kc_ref[g], kn_ref[g], kx_ref[g]], axis=0)
        vbs = [jnp.concatenate([vp_ref[2 * g + a], vc_ref[2 * g + a], vn_ref[2 * g + a],
                                vx_ref[2 * g + a]], axis=0) for a in range(2)]
        parts = []
        for p in range(per_kv // 2):
            t = g * (per_kv // 2) + p
            qt = q_ref[:, t * LANES:(t + 1) * LANES]
            zero = jnp.zeros_like(qt)
            parts += [jnp.where(low_head, qt, zero), jnp.where(low_head, zero, qt)]
        lhs = jnp.concatenate(parts, axis=0)
        s = lax.dot_general(lhs, kb, (((1,), (1,)), ((), ())), preferred_element_type=F32)
        outs = []
        for hh in range(per_kv):
            sink = sink_ref[g * per_kv + hh] * LOG2_E
            sh = s[hh * blk:(hh + 1) * blk]
            s_loc = jnp.where(valid, sh[:, :3 * blk], NEG_INF)
            s_ctx = sh[:, 3 * blk:]
            m = jnp.maximum(jnp.maximum(jnp.max(s_loc, axis=-1, keepdims=True),
                                        jnp.max(s_ctx, axis=-1, keepdims=True)), sink)
            pr = jnp.concatenate([jnp.exp2(s_loc - m), jnp.exp2(s_ctx - m)], axis=1).astype(BF16)
            pv = _dot(pr, vbs[hh % 2])
            den = pltpu.roll(pv, HEAD_DIM, 1) + jnp.exp2(sink - m)
            outs.append(pv / den)
        for p in range(per_kv // 2):
            t = g * (per_kv // 2) + p
            o_ref[:, t * LANES:(t + 1) * LANES] = jnp.where(
                low_head, outs[2 * p], outs[2 * p + 1]).astype(BF16)


def _attn_call(sink, q, kd, vd, kx, vx):
    n = q.shape[0]
    blk = ATTN_BLOCK
    nb = n // blk
    nctx = kx.shape[1]
    k_spec = lambda f: pl.BlockSpec((N_KV_HEADS, blk, LANES), f)
    v_spec = lambda f: pl.BlockSpec((2 * N_KV_HEADS, blk, LANES), f)
    prev = lambda b: (0, jnp.maximum(b - 1, 0), 0)
    cur = lambda b: (0, b, 0)
    nxt = lambda b: (0, jnp.minimum(b + 1, nb - 1), 0)
    return pl.pallas_call(
        _attn_body,
        grid=(nb,),
        in_specs=[pl.BlockSpec(memory_space=pltpu.SMEM),
                  pl.BlockSpec((blk, Q_COLS), lambda b: (b, 0)),
                  k_spec(prev), k_spec(cur), k_spec(nxt),
                  v_spec(prev), v_spec(cur), v_spec(nxt),
                  pl.BlockSpec((N_KV_HEADS, nctx, LANES), lambda b: (0, 0, 0)),
                  pl.BlockSpec((2 * N_KV_HEADS, nctx, LANES), lambda b: (0, 0, 0))],
        out_specs=pl.BlockSpec((blk, Q_COLS), lambda b: (b, 0)),
        out_shape=jax.ShapeDtypeStruct((n, Q_COLS), BF16),
        compiler_params=_cparams(("arbitrary",), VMEM_LIMIT),
        name="attn",
    )(sink, q, kd, kd, kd, vd, vd, vd, kx, vx)


def _proj_body(a_ref, x_ref, mods_ref, w_ref, o_ref):
    o_ref[...] = x_ref[...] + mods_ref[3:4, :] * _dot(a_ref[...], w_ref[...])


def _proj_call(a, x, mods, w_bf16):
    n = x.shape[0]
    tm = min(ROW_TILE, n)
    return pl.pallas_call(
        _proj_body,
        grid=(n // tm,),
        in_specs=[pl.BlockSpec((tm, a.shape[1]), lambda i: (i, 0)),
                  pl.BlockSpec((tm, D_MODEL), lambda i: (i, 0)),
                  pl.BlockSpec((SUBLANES, D_MODEL), lambda i: (0, 0)),
                  pl.BlockSpec(w_bf16.shape, lambda i: (0, 0))],
        out_specs=pl.BlockSpec((tm, D_MODEL), lambda i: (i, 0)),
        out_shape=jax.ShapeDtypeStruct(x.shape, F32),
        compiler_params=_cparams(("arbitrary",), VMEM_LIMIT),
        name="proj",
    )(a, x, mods, w_bf16)


def _rope_lane_tables(length):
    rows = length // GRID_W
    row_pos = jnp.repeat(jnp.arange(rows, dtype=F32), GRID_W)
    col_pos = jnp.tile(jnp.arange(GRID_W, dtype=F32), rows)
    axis_dim = HEAD_DIM // 2
    inv_freq = ROPE_BASE ** (-jnp.arange(0, axis_dim, 2, dtype=F32) / axis_dim)
    ang_r = row_pos[:, None] * inv_freq[None, :]
    ang_c = col_pos[:, None] * inv_freq[None, :]
    cos_h = jnp.concatenate([jnp.cos(ang_r), jnp.cos(ang_r), jnp.cos(ang_c), jnp.cos(ang_c)], axis=1)
    sin_h = jnp.concatenate([-jnp.sin(ang_r), jnp.sin(ang_r), -jnp.sin(ang_c), jnp.sin(ang_c)], axis=1)
    reps = LANES // HEAD_DIM
    return jnp.tile(cos_h, (1, reps)), jnp.tile(sin_h, (1, reps))


def _attention_layer(x, ctx, mods_x, mods_c, wqkv, wo, q_gain, k_gain, sink):
    L = x.shape[0]
    nctx = ctx.shape[0]
    wqkv_b = wqkv.astype(BF16)
    head = np.arange(LANES) // HEAD_DIM
    bd = jnp.asarray((head[:, None] == head[None, :]).astype(np.float32) / HEAD_DIM).astype(BF16)
    reps = LANES // HEAD_DIM
    gains = jnp.zeros((SUBLANES, LANES), F32)
    gains = gains.at[0].set(jnp.tile(q_gain, reps) * (HEAD_DIM ** -0.5 * LOG2_E))
    gains = gains.at[1].set(jnp.tile(k_gain, reps))
    cos_t, sin_t = _rope_lane_tables(L)
    q, kd, vd = _qkv_call(x, mods_x, wqkv_b, bd, gains, cos_t, sin_t)
    ones = jnp.ones((nctx, LANES), F32)
    _, kx, vx = _qkv_call(ctx, mods_c, wqkv_b, bd, gains, ones, jnp.zeros_like(ones))
    o = _attn_call(sink, q, kd, vd, kx, vx)
    return _proj_call(o, x, mods_x, wo.astype(BF16))


def _pool_body(xp_ref, xc_ref, xn_ref, mods_ref, w_ref, o_ref, h_scr):
    i = pl.program_id(0)
    tm = xc_ref.shape[0]
    total = tm * pl.num_programs(0)
    x = xc_ref[...]
    mods = mods_ref[...]
    h_scr[0:POOL_HALO, :] = jnp.where(i > 0, _mod_norm(xp_ref[...], mods), 0.0)
    h_scr[POOL_HALO:POOL_HALO + tm, :] = _mod_norm(x, mods)
    h_scr[POOL_HALO + tm:, :] = jnp.where(i < pl.num_programs(0) - 1, _mod_norm(xn_ref[...], mods), 0.0)
    t = i * tm + lax.broadcasted_iota(jnp.int32, (tm, 1), 0)
    gd = POOL_GROUP_DIM
    for g, win in enumerate(POOL_WINDOWS):
        sl = slice(g * gd, (g + 1) * gd)
        half = win // 2
        tot = h_scr[POOL_HALO - half:POOL_HALO - half + tm, sl]
        for s in range(-half + 1, half):
            tot = tot + h_scr[POOL_HALO + s:POOL_HALO + s + tm, sl]
        lo = jnp.maximum(t - half, 0)
        hi = jnp.minimum(t + half - 1, total - 1)
        cnt = (hi - lo + 1).astype(F32)
        pooled = (tot / cnt - h_scr[POOL_HALO:POOL_HALO + tm, sl]).astype(BF16)
        y = _dot(pooled, w_ref[g]) * mods[4:5, sl]
        o_ref[:, sl] = x[:, sl] + mods[3:4, sl] * y


def _pool_layer(x, mods, w_bf16):
    n = x.shape[0]
    tm = min(ROW_TILE, n)
    r = tm // POOL_HALO
    last = n // POOL_HALO - 1
    return pl.pallas_call(
        _pool_body,
        grid=(n // tm,),
        in_specs=[pl.BlockSpec((POOL_HALO, D_MODEL), lambda i: (jnp.maximum(i * r - 1, 0), 0)),
                  pl.BlockSpec((tm, D_MODEL), lambda i: (i, 0)),
                  pl.BlockSpec((POOL_HALO, D_MODEL), lambda i: (jnp.minimum((i + 1) * r, last), 0)),
                  pl.BlockSpec((SUBLANES, D_MODEL), lambda i: (0, 0)),
                  pl.BlockSpec(w_bf16.shape, lambda i: (0, 0, 0))],
        out_specs=pl.BlockSpec((tm, D_MODEL), lambda i: (i, 0)),
        out_shape=jax.ShapeDtypeStruct(x.shape, F32),
        scratch_shapes=[pltpu.VMEM((tm + 2 * POOL_HALO, D_MODEL), F32)],
        compiler_params=_cparams(("arbitrary",), VMEM_LIMIT),
        name="pool",
    )(x, x, x, mods, w_bf16)


def _router_body(x_ref, mods_ref, r_ref, h_ref, route_ref, cnt_ref):
    h = _mod_norm(x_ref[...], mods_ref[...])
    h_hi = h.astype(BF16)
    h_ref[...] = h_hi
    h_lo = (h - h_hi.astype(F32)).astype(BF16)
    r = r_ref[...]
    r_hi = r.astype(BF16)
    r_lo = (r - r_hi.astype(F32)).astype(BF16)
    logits = _dot(h_hi, r_hi) + (_dot(h_lo, r_hi) + _dot(h_hi, r_lo))
    lane = lax.broadcasted_iota(jnp.int32, logits.shape, 1)
    logits = jnp.where(lane < N_EXPERTS, logits, -jnp.inf)
    m1 = jnp.max(logits, axis=-1, keepdims=True)
    i1 = jnp.min(jnp.where(logits == m1, lane, LANES), axis=-1, keepdims=True)
    rest = jnp.where(lane == i1, -jnp.inf, logits)
    m2 = jnp.max(rest, axis=-1, keepdims=True)
    i2 = jnp.min(jnp.where(rest == m2, lane, LANES), axis=-1, keepdims=True)
    e = jnp.exp(m2 - m1)
    w1 = 1.0 / (1.0 + e)
    w2 = e / (1.0 + e)
    tm = h.shape[0]
    oh = jnp.where((lane == i1) | (lane == i2 + N_EXPERTS), 1.0, 0.0)
    before = (lax.broadcasted_iota(jnp.int32, (tm, tm), 1)
              < lax.broadcasted_iota(jnp.int32, (tm, tm), 0))
    prior = _dot(jnp.where(before, 1.0, 0.0).astype(BF16), oh.astype(BF16))
    cnt = jnp.sum(oh, axis=0, keepdims=True)
    lane8 = lane[0:SUBLANES, :]
    cnt8 = jnp.broadcast_to(cnt, (SUBLANES, LANES))
    both = cnt8 + pltpu.roll(cnt8, LANES - N_EXPERTS, 1)
    seg_len = jnp.where(lane8 < N_EXPERTS, jnp.ceil(both * (1.0 / SEG_ALIGN)) * SEG_ALIGN, 0.0)
    incl = seg_len
    shift = 1
    while shift < N_EXPERTS:
        incl = incl + jnp.where(lane8 >= shift, pltpu.roll(incl, shift, 1), 0.0)
        shift *= 2
    seg = incl - seg_len
    seg = jnp.where(lane8 < N_EXPERTS, seg, 0.0)
    base = seg + pltpu.roll(jnp.where(lane8 < N_EXPERTS, cnt8 + seg, 0.0), N_EXPERTS, 1)
    pos = prior + base[0:1, :]
    packed = jnp.sum(jnp.where(lane == i1, pos, 0.0) + jnp.where(lane == i2 + N_EXPERTS, pos * POS_RADIX, 0.0),
                     axis=-1, keepdims=True)
    p2 = jnp.floor(packed * (1.0 / POS_RADIX))
    p1 = packed - p2 * POS_RADIX
    col = lax.broadcasted_iota(jnp.int32, route_ref.shape, 1)
    vals = (i1.astype(F32), i2.astype(F32), w1, w2, p1, p2)
    out = jnp.zeros(route_ref.shape, F32)
    for k, v in enumerate(vals):
        out = jnp.where(col == k, v, out)
    route_ref[...] = out
    row = lax.broadcasted_iota(jnp.int32, cnt_ref.shape[1:], 0)
    cnt_ref[0] = jnp.where(row == 0, seg_len, jnp.where(row == 1, seg, 0.0))


def _router_call(x, mods, router_pad):
    n = x.shape[0]
    tm = min(ROUTE_TILE, n)
    assert _route_cap(tm) <= POS_RADIX
    return pl.pallas_call(
        _router_body,
        grid=(n // tm,),
        in_specs=[pl.BlockSpec((tm, D_MODEL), lambda i: (i, 0)),
                  pl.BlockSpec((SUBLANES, D_MODEL), lambda i: (0, 0)),
                  pl.BlockSpec((D_MODEL, LANES), lambda i: (0, 0))],
        out_specs=[pl.BlockSpec((tm, D_MODEL), lambda i: (i, 0)),
                   pl.BlockSpec((tm, SUBLANES), lambda i: (i, 0)),
                   pl.BlockSpec((1, SUBLANES, LANES), lambda i: (i, 0, 0))],
        out_shape=[jax.ShapeDtypeStruct((n, D_MODEL), BF16),
                   jax.ShapeDtypeStruct((n, SUBLANES), F32),
                   jax.ShapeDtypeStruct((n // tm, SUBLANES, LANES), F32)],
        compiler_params=_cparams(("arbitrary",), VMEM_LIMIT),
        name="router",
    )(x, mods, router_pad)


def _segment_copies(i, seg_ref, dst_ref, len_ref, tile_buf, sorted_hbm, sem, to_hbm):
    for e in range(N_EXPERTS):
        length = len_ref[i * N_EXPERTS + e]
        seg = seg_ref[i * N_EXPERTS + e]
        dst = dst_ref[i * N_EXPERTS + e]
        size = SEG_ALIGN
        while size <= tile_buf.shape[0] // 2:
            done = length & ~(2 * size - 1)
            a = tile_buf.at[pl.ds(pl.multiple_of(seg + done, SEG_ALIGN), size)]
            b = sorted_hbm.at[pl.ds(pl.multiple_of(dst + done, SEG_ALIGN), size)]
            copy = pltpu.make_async_copy(a, b, sem) if to_hbm else pltpu.make_async_copy(b, a, sem)
            yield (length & size) != 0, copy
            size *= 2


def _scatter_body(seg_ref, dst_ref, len_ref, h_ref, prow_ref, xs_in_ref, xs_ref, sbuf, sems):
    del xs_in_ref
    i = pl.program_id(0)
    last = pl.num_programs(0) - 1
    slot = i % 2
    cap, tm = sbuf.shape[1], h_ref.shape[0]
    r = lax.broadcasted_iota(jnp.int32, (cap, tm), 0)
    hit = (r == prow_ref[0:1, :]) | (r == prow_ref[1:2, :])
    sel = jnp.where(hit, 1.0, 0.0).astype(BF16)
    sbuf[slot] = _dot(sel, h_ref[...]).astype(BF16)

    def copies(tile, s):
        return _segment_copies(tile, seg_ref, dst_ref, len_ref, sbuf.at[s], xs_ref, sems.at[s], True)

    for pred, copy in copies(i, slot):
        pl.when(pred)(copy.start)
    for pred, copy in copies(jnp.maximum(i - 1, 0), 1 - slot):
        pl.when(pred & (i > 0))(copy.wait)
    for pred, copy in copies(i, slot):
        pl.when(pred & (i == last))(copy.wait)


def _route_cap(tm):
    return 2 * tm + N_EXPERTS * SEG_ALIGN


def _scatter_call(seg, dst, seg_len, h, prow, n_rows):
    n = h.shape[0]
    tm = min(ROUTE_TILE, n)
    zeros = jnp.zeros((n_rows, D_MODEL), BF16)
    return pl.pallas_call(
        _scatter_body,
        grid_spec=pltpu.PrefetchScalarGridSpec(
            num_scalar_prefetch=3,
            grid=(n // tm,),
            in_specs=[pl.BlockSpec((tm, D_MODEL), lambda i, *_: (i, 0)),
                      pl.BlockSpec((SUBLANES, tm), lambda i, *_: (i, 0)),
                      pl.BlockSpec(memory_space=pl.ANY)],
            out_specs=pl.BlockSpec(memory_space=pl.ANY),
            scratch_shapes=[pltpu.VMEM((2, _route_cap(tm), D_MODEL), BF16),
                            pltpu.SemaphoreType.DMA((2,))]),
        out_shape=jax.ShapeDtypeStruct((n_rows, D_MODEL), BF16),
        input_output_aliases={5: 0},
        compiler_params=_cparams(("arbitrary",), VMEM_LIMIT),
        name="moe_scatter",
    )(seg, dst, seg_len, h, prow, zeros)


def _moe_body(te_ref, tv_ref, tf_ref, xs_ref, wi_hbm, wo_hbm, ys_ref, wi_res, wo_res, stage_i,
              stage_o, act, sems):
    i = pl.program_id(0)

    @pl.when(tf_ref[i] > 0)
    def _():
        _load_swiglu_weights(te_ref[i], wi_hbm, wo_hbm, wi_res, wo_res, stage_i, stage_o, sems)

    @pl.when(tv_ref[i] > 0)
    def _():
        ys_ref[...] = _swiglu_tile(xs_ref[...], wi_res, wo_res, act).astype(BF16)

    @pl.when(tv_ref[i] == 0)
    def _():
        ys_ref[...] = jnp.zeros(ys_ref.shape, BF16)


def _moe_call(tile_expert, tile_live, tile_first, xs, wi_all, wo_all):
    n_rows = xs.shape[0]
    return pl.pallas_call(
        _moe_body,
        grid_spec=pltpu.PrefetchScalarGridSpec(
            num_scalar_prefetch=3,
            grid=(n_rows // MOE_TILE,),
            in_specs=[pl.BlockSpec((MOE_TILE, D_MODEL), lambda i, *_: (i, 0)),
                      pl.BlockSpec(memory_space=pl.ANY),
                      pl.BlockSpec(memory_space=pl.ANY)],
            out_specs=pl.BlockSpec((MOE_TILE, D_MODEL), lambda i, *_: (i, 0)),
            scratch_shapes=_swiglu_scratch(MOE_TILE)),
        out_shape=jax.ShapeDtypeStruct((n_rows, D_MODEL), BF16),
        compiler_params=_cparams(("arbitrary",), VMEM_LIMIT),
        name="moe_ffn",
    )(tile_expert, tile_live, tile_first, xs, wi_all, wo_all)


def _combine_body(seg_ref, dst_ref, len_ref, x_ref, route_ref, pcol_ref, mods_ref, ys_ref, o_ref,
                  ybuf, sems):
    i = pl.program_id(0)
    last = pl.num_programs(0) - 1
    slot = i % 2
    tm, cap = x_ref.shape[0], ybuf.shape[1]

    def copies(tile, s):
        return _segment_copies(tile, seg_ref, dst_ref, len_ref, ybuf.at[s], ys_ref, sems.at[s], False)

    @pl.when(i == 0)
    def _():
        ybuf[...] = jnp.zeros(ybuf.shape, BF16)

    for pred, copy in copies(0, 0):
        pl.when(pred & (i == 0))(copy.start)
    for pred, copy in copies(jnp.minimum(i + 1, last), 1 - slot):
        pl.when(pred & (i < last))(copy.start)
    for pred, copy in copies(i, slot):
        pl.when(pred)(copy.wait)
    c = lax.broadcasted_iota(jnp.int32, (tm, cap), 1)
    pcol = pcol_ref[...]
    route = route_ref[...]
    sel = jnp.where(c == pcol[:, 0:1], route[:, 2:3], jnp.where(c == pcol[:, 1:2], route[:, 3:4], 0.0))
    o_ref[...] = x_ref[...] + mods_ref[3:4, :] * _dot(sel.astype(BF16), ybuf[slot])


def _combine_call(seg, dst, seg_len, x, route, pcol, mods, ys):
    n = x.shape[0]
    tm = min(ROUTE_TILE, n)
    return pl.pallas_call(
        _combine_body,
        grid_spec=pltpu.PrefetchScalarGridSpec(
            num_scalar_prefetch=3,
            grid=(n // tm,),
            in_specs=[pl.BlockSpec((tm, D_MODEL), lambda i, *_: (i, 0)),
                      pl.BlockSpec((tm, SUBLANES), lambda i, *_: (i, 0)),
                      pl.BlockSpec((tm, SUBLANES), lambda i, *_: (i, 0)),
                      pl.BlockSpec((SUBLANES, D_MODEL), lambda i, *_: (0, 0)),
                      pl.BlockSpec(memory_space=pl.ANY)],
            out_specs=pl.BlockSpec((tm, D_MODEL), lambda i, *_: (i, 0)),
            scratch_shapes=[pltpu.VMEM((2, _route_cap(tm), D_MODEL), BF16),
                            pltpu.SemaphoreType.DMA((2,))]),
        out_shape=jax.ShapeDtypeStruct(x.shape, F32),
        compiler_params=_cparams(("arbitrary",), VMEM_LIMIT),
        name="moe_combine",
    )(seg, dst, seg_len, x, route, pcol, mods, ys)


def _routing_tables(route, counts, tm, n_tiles):
    nt = counts.shape[0]
    seg_len = counts[:, 0, :N_EXPERTS].astype(jnp.int32)
    seg = counts[:, 1, :N_EXPERTS].astype(jnp.int32)
    tiles = (jnp.sum(seg_len, axis=0) + MOE_TILE - 1) // MOE_TILE
    tile_end = jnp.cumsum(tiles)
    start = (tile_end - tiles) * MOE_TILE
    dst = start[None, :] + jnp.cumsum(seg_len, axis=0) - seg_len
    t = jnp.arange(n_tiles, dtype=jnp.int32)
    expert = jnp.sum((t[:, None] >= tile_end[None, :]).astype(jnp.int32), axis=1)
    live = (t < tile_end[-1]).astype(jnp.int32)
    last_expert = jnp.sum((tile_end[-1] - 1 >= tile_end).astype(jnp.int32))
    expert = jnp.where(live > 0, expert, last_expert).astype(jnp.int32)
    prev = jnp.concatenate([jnp.full((1,), -1, jnp.int32), expert[:-1]])
    first = (live * (expert != prev)).astype(jnp.int32)
    p12 = route[:, 4:6].astype(jnp.int32)
    pcol = jnp.concatenate([p12, jnp.zeros((p12.shape[0], SUBLANES - 2), jnp.int32)], axis=1)
    prow = jnp.concatenate([p12.reshape(nt, tm, 2).transpose(0, 2, 1),
                            jnp.full((nt, SUBLANES - 2, tm), -1, jnp.int32)], axis=1)
    return (seg.reshape(-1), dst.reshape(-1).astype(jnp.int32), seg_len.reshape(-1),
            expert, live, first, pcol, prow.reshape(nt * SUBLANES, tm))


def _moe_layer(x, mods, router, wi_all, wo_all, layer):
    n = x.shape[0]
    tm = min(ROUTE_TILE, n)
    router_pad = jnp.zeros((D_MODEL, LANES), F32).at[:, :N_EXPERTS].set(router)
    h, route, counts = _router_call(x, mods, router_pad)
    max_rows = 2 * n + (n // tm) * N_EXPERTS * (SEG_ALIGN - 1)
    n_tiles = -(-max_rows // MOE_TILE) + N_EXPERTS
    seg, dst, seg_len, tile_expert, tile_live, tile_first, pcol, prow = _routing_tables(
        route, counts, tm, n_tiles)
    xs = _scatter_call(seg, dst, seg_len, h, prow, n_tiles * MOE_TILE)
    ys = _moe_call(tile_expert + layer * N_EXPERTS, tile_live, tile_first, xs, wi_all, wo_all)
    return _combine_call(seg, dst, seg_len, x, route, pcol, mods, ys)


def _mods(gain, shift, scale, gate, extra=None):
    rows = [gain, shift, scale, gate, extra if extra is not None else jnp.zeros_like(gain)]
    m = jnp.stack(rows, axis=0)
    return jnp.concatenate([m, jnp.zeros((SUBLANES - m.shape[0], m.shape[1]), F32)], axis=0)


def kernel(x, c, ctx, c_ctx, ada_w, ada_b, norm_mix, norm_ffn, fnet_w, attn_wqkv, attn_q_gain,
           attn_k_gain, attn_sink, attn_wo, pool_w, pool_scale, ffn_wi, ffn_wo, moe_router,
           moe_wi, moe_wo):
    assert x.shape[0] == 1 and x.shape[2] == D_MODEL
    depth = ada_w.shape[0]
    xs = x[0]
    cs = ctx[0]
    ada = _ada_call(jnp.stack([c[0], c_ctx]), ada_w, ada_b)

    attn_layers = [i for i in range(depth) if i % N_MIXERS == 1]
    last_ctx_read = attn_layers[-1] if attn_layers else -1
    n_side = math.isqrt(xs.shape[0])
    assert n_side * n_side == xs.shape[0] and n_side % SUBLANES == 0
    seq_tables = _sequence_tables(n_side)
    ch = _channel_table()
    moe_wi_all = moe_wi.reshape((-1,) + moe_wi.shape[2:])
    moe_wo_all = moe_wo.reshape((-1,) + moe_wo.shape[2:])

    for i in range(depth):
        mixer = i % N_MIXERS
        j = i // N_MIXERS
        f = i // 2
        ctx_full = i < last_ctx_read
        ctx_live = i <= last_ctx_read
        sh1, sc1, g1, sh2, sc2, g2 = [ada[i, 0, k * D_MODEL:(k + 1) * D_MODEL] for k in range(6)]
        csh1, csc1, cg1, csh2, csc2, cg2 = [ada[i, 1, k * D_MODEL:(k + 1) * D_MODEL] for k in range(6)]
        extra = pool_scale[j] if mixer == 2 else None
        m1 = _mods(norm_mix[i], sh1, sc1, g1, extra)
        m2 = _mods(norm_ffn[i], sh2, sc2, g2)
        cm1 = _mods(norm_mix[i], csh1, csc1, cg1, extra)
        cm2 = _mods(norm_ffn[i], csh2, csc2, cg2)

        if mixer == 0:
            wb = fnet_w[j].astype(BF16)
            xs = _fourier_layer(xs, m1, wb, seq_tables, ch)
            if ctx_full:
                cs = _ctx_fourier_layer(cs, cm1, wb, ch)
        elif mixer == 1:
            assert ctx_live and not ctx_full
            xs = _attention_layer(xs, cs, m1, cm1, attn_wqkv[j], attn_wo[j], attn_q_gain[j],
                                  attn_k_gain[j], attn_sink[j])
        else:
            wb = pool_w[j].astype(BF16)
            xs = _pool_layer(xs, m1, wb)
            if ctx_full:
                cs = _pool_layer(cs, cm1, wb)

        if i % 2 == 0:
            xs = _ffn_layer(xs, m2, ffn_wi, ffn_wo, f)
            if ctx_full:
                cs = _ffn_layer(cs, cm2, ffn_wi, ffn_wo, f)
        else:
            xs = _moe_layer(xs, m2, moe_router[f], moe_wi_all, moe_wo_all, f)
            if ctx_full:
                cs = _moe_layer(cs, cm2, moe_router[f], moe_wi_all, moe_wo_all, f)
    return xs[None]
```

```python
import functools
import math

import numpy as np
import jax
import jax.numpy as jnp
from jax import lax
from jax.experimental import pallas as pl
from jax.experimental.pallas import tpu as pltpu

F32 = jnp.float32
BF16 = jnp.bfloat16

D_MODEL = 1024
GRID_W = 64
N_MIXERS = 3
FNET_GROUPS = 4
FNET_GROUP_DIM = D_MODEL // FNET_GROUPS
N_HEADS = 16
N_KV_HEADS = 4
HEAD_DIM = 64
Q_COLS = N_HEADS * HEAD_DIM
KV_COLS = N_KV_HEADS * HEAD_DIM
WINDOW = 128
ATTN_BLOCK = 128
ROPE_BASE = 10000.0
POOL_WINDOWS = (2, 4, 8, 16)
POOL_GROUP_DIM = D_MODEL // len(POOL_WINDOWS)
POOL_HALO = 8
D_FF = 3584
N_EXPERTS = 8
NORM_EPS = 1e-6
NEG_INF = -1e30
LOG2_E = 1.4426950408889634

LANES = 128
SUBLANES = 8
VMEM_LIMIT = 56 * 1024 * 1024

ROW_TILE = 512
FF_CHUNK = 512
MOE_TILE = 256
W_CHUNK = 512
ROUTE_TILE = 512
SEG_ALIGN = 16
N_COND = 2


def _cparams(sem, vmem=None):
    return pltpu.CompilerParams(dimension_semantics=sem, vmem_limit_bytes=vmem)


def _mod_norm(x, mods):
    ms = jnp.mean(x * x, axis=-1, keepdims=True)
    y = x * lax.rsqrt(ms + NORM_EPS) * mods[0:1, :]
    return y * (1.0 + mods[2:3, :]) + mods[1:2, :]


def _dot(a, b):
    return jnp.dot(a, b, preferred_element_type=F32)


def _ada_body(cc_ref, w_ref, b_ref, o_ref):
    n_tiles = w_ref.shape[-1] // LANES
    d = w_ref.shape[1]
    acc = [[jnp.zeros((SUBLANES, LANES), F32) for _ in range(N_COND)] for _ in range(n_tiles)]
    for k in range(d // LANES):
        rows = slice(k * LANES, (k + 1) * LANES)
        a = cc_ref[:, rows, :]
        a = a / (1.0 + jnp.exp(-a))
        for t in range(n_tiles):
            wt = w_ref[0, rows, t * LANES:(t + 1) * LANES]
            for r in range(N_COND):
                acc[t][r] = acc[t][r] + jnp.sum(
                    (wt * a[r]).reshape(LANES // SUBLANES, SUBLANES, LANES), axis=0)
    row = lax.broadcasted_iota(jnp.int32, (SUBLANES, LANES), 0)
    for t in range(n_tiles):
        sl = slice(t * LANES, (t + 1) * LANES)
        tile = jnp.zeros((SUBLANES, LANES), F32)
        for r in range(N_COND):
            s = jnp.sum(acc[t][r], axis=0, keepdims=True) + b_ref[0, :, sl]
            tile = jnp.where(row == r, s, tile)
        o_ref[0, :, sl] = tile


def _ada_call(cc, ada_w, ada_b):
    depth, d, n6 = ada_w.shape
    tn = n6 // 4
    cc_lanes = jnp.broadcast_to(cc[:, :, None], (N_COND, d, LANES))
    return pl.pallas_call(
        _ada_body,
        grid=(depth, n6 // tn),
        in_specs=[pl.BlockSpec((N_COND, d, LANES), lambda l, j: (0, 0, 0)),
                  pl.BlockSpec((1, d, tn), lambda l, j: (l, 0, j)),
                  pl.BlockSpec((1, 1, tn), lambda l, j: (l, 0, j))],
        out_specs=pl.BlockSpec((1, SUBLANES, tn), lambda l, j: (l, 0, j)),
        out_shape=jax.ShapeDtypeStruct((depth, SUBLANES, n6), F32),
        compiler_params=_cparams(("arbitrary", "arbitrary"), VMEM_LIMIT),
        name="ada",
    )(cc_lanes, ada_w, ada_b.reshape(depth, 1, n6))


def _cos_sin(n, period):
    k = np.arange(n, dtype=np.float64)
    ang = 2.0 * np.pi * np.outer(k, k) / period
    return np.cos(ang).astype(np.float32), np.sin(ang).astype(np.float32)


def _channel_table():
    c, s = _cos_sin(FNET_GROUP_DIM, FNET_GROUP_DIM)
    return (jnp.concatenate([jnp.asarray(c), jnp.asarray(s)], axis=0)
            * (FNET_GROUP_DIM ** -0.5)).astype(BF16)


def _sequence_tables(n):
    c, s = _cos_sin(n, n)
    eye = np.eye(SUBLANES, dtype=np.float32)
    rows = n * SUBLANES
    f = np.stack([c, -s]) * np.float32(1.0 / n)
    ka = (f[:, None, :, :, None] * eye[None, :, None, None, :]).reshape(2 * rows, rows)
    ck = (c[:, None, :, None] * eye[None, :, None, :]).reshape(rows, rows)
    sk = (s[:, None, :, None] * eye[None, :, None, :]).reshape(rows, rows)
    cs = np.concatenate([ck, sk], axis=1)
    tc, ts = _cos_sin(n, n * n)
    oct_ = n // SUBLANES
    tc = np.ascontiguousarray(np.broadcast_to(tc.reshape(oct_, rows, 1), (oct_, rows, LANES)))
    ts = np.ascontiguousarray(np.broadcast_to(ts.reshape(oct_, rows, 1), (oct_, rows, LANES)))
    return jnp.asarray(ka).astype(BF16), jnp.asarray(cs).astype(BF16), jnp.asarray(tc), jnp.asarray(ts)


def _fft_a_body(x_ref, mods_ref, ka_ref, tc_ref, ts_ref, zr_ref, zi_ref):
    rows = ka_ref.shape[1]
    x = x_ref[...].reshape(rows, D_MODEL)
    h = _mod_norm(x, mods_ref[...]).astype(BF16)
    ka = ka_ref[...]
    tc, ts = tc_ref[0], ts_ref[0]
    gw = FNET_GROUP_DIM
    for g in range(D_MODEL // gw):
        z = _dot(ka, h[:, g * gw:(g + 1) * gw])
        zr, zi = z[:rows], z[rows:]
        for t in range(gw // LANES):
            a = zr[:, t * LANES:(t + 1) * LANES]
            b = zi[:, t * LANES:(t + 1) * LANES]
            sl = slice(g * gw + t * LANES, g * gw + (t + 1) * LANES)
            zr_ref[:, :, sl] = (a * tc + b * ts).reshape(SUBLANES, rows // SUBLANES, LANES)
            zi_ref[:, :, sl] = (b * tc - a * ts).reshape(SUBLANES, rows // SUBLANES, LANES)


def _fft_b_body(zr_ref, zi_ref, x_ref, mods_ref, cs_ref, ch_ref, w_ref, o_ref):
    rows = cs_ref.shape[0]
    cs = cs_ref[...]
    gw = FNET_GROUP_DIM
    ys = []
    for g in range(D_MODEL // gw):
        zr = zr_ref[:, :, g * gw:(g + 1) * gw].reshape(rows, gw).astype(BF16)
        zi = zi_ref[:, :, g * gw:(g + 1) * gw].reshape(rows, gw).astype(BF16)
        xr = _dot(cs, jnp.concatenate([zr, zi], axis=0))
        xi = _dot(cs, jnp.concatenate([zi, -zr], axis=0))
        ys.append(_dot(jnp.concatenate([xr, xi], axis=1).astype(BF16), ch_ref[...]).astype(BF16))
    y = _dot(jnp.concatenate(ys, axis=1), w_ref[...])
    x = x_ref[...].reshape(rows, D_MODEL)
    o_ref[...] = (x + mods_ref[3:4, :] * y).reshape(o_ref.shape)


def _fourier_layer(x, mods, w_bf16, seq_tables, ch):
    L = x.shape[0]
    n = math.isqrt(L)
    rows = n * SUBLANES
    oct_ = n // SUBLANES
    ka, cs, tc, ts = seq_tables
    x3 = x.reshape(n, n, D_MODEL)
    gw = FNET_GROUP_DIM
    zshape = jax.ShapeDtypeStruct((n, n, D_MODEL), F32)
    once = dict(pipeline_mode=pl.Buffered(1))
    zr, zi = pl.pallas_call(
        _fft_a_body,
        grid=(oct_,),
        in_specs=[pl.BlockSpec((n, SUBLANES, D_MODEL), lambda o: (0, o, 0)),
                  pl.BlockSpec((SUBLANES, D_MODEL), lambda o: (0, 0)),
                  pl.BlockSpec((2 * rows, rows), lambda o: (0, 0), **once),
                  pl.BlockSpec((1, rows, LANES), lambda o: (o, 0, 0)),
                  pl.BlockSpec((1, rows, LANES), lambda o: (o, 0, 0))],
        out_specs=[pl.BlockSpec((SUBLANES, n, D_MODEL), lambda o: (o, 0, 0)),
                   pl.BlockSpec((SUBLANES, n, D_MODEL), lambda o: (o, 0, 0))],
        out_shape=[zshape, zshape],
        compiler_params=_cparams(("arbitrary",), VMEM_LIMIT),
        name="fft_a",
    )(x3, mods, ka, tc, ts)
    out = pl.pallas_call(
        _fft_b_body,
        grid=(oct_,),
        in_specs=[pl.BlockSpec((n, SUBLANES, D_MODEL), lambda p: (0, p, 0)),
                  pl.BlockSpec((n, SUBLANES, D_MODEL), lambda p: (0, p, 0)),
                  pl.BlockSpec((n, SUBLANES, D_MODEL), lambda p: (0, p, 0)),
                  pl.BlockSpec((SUBLANES, D_MODEL), lambda p: (0, 0)),
                  pl.BlockSpec((rows, 2 * rows), lambda p: (0, 0), **once),
                  pl.BlockSpec((2 * gw, gw), lambda p: (0, 0), **once),
                  pl.BlockSpec((D_MODEL, D_MODEL), lambda p: (0, 0), **once)],
        out_specs=pl.BlockSpec((n, SUBLANES, D_MODEL), lambda p: (0, p, 0)),
        out_shape=jax.ShapeDtypeStruct((n, n, D_MODEL), F32),
        compiler_params=_cparams(("arbitrary",), VMEM_LIMIT),
        name="fft_b",
    )(zr, zi, x3, mods, cs, ch, w_bf16)
    return out.reshape(L, D_MODEL)


def _ctx_fourier_body(x_ref, mods_ref, f_ref, ch_ref, w_ref, o_ref):
    x = x_ref[...]
    n = x.shape[0]
    h = _mod_norm(x, mods_ref[...]).astype(BF16)
    g = _dot(f_ref[...], h)
    gr, gi = g[:n], g[n:]
    gw = FNET_GROUP_DIM
    ys = []
    for k in range(D_MODEL // gw):
        sl = slice(k * gw, (k + 1) * gw)
        ys.append(_dot(jnp.concatenate([gr[:, sl], gi[:, sl]], axis=1).astype(BF16), ch_ref[...]))
    y = jnp.concatenate(ys, axis=1).astype(BF16)
    o_ref[...] = x + mods_ref[3:4, :] * _dot(y, w_ref[...])


def _ctx_fourier_layer(ctx, mods, w_bf16, ch):
    n = ctx.shape[0]
    c, s = _cos_sin(n, n)
    f = (jnp.concatenate([jnp.asarray(c), -jnp.asarray(s)], axis=0) * (n ** -0.5)).astype(BF16)
    return pl.pallas_call(
        _ctx_fourier_body,
        out_shape=jax.ShapeDtypeStruct(ctx.shape, F32),
        compiler_params=pltpu.CompilerParams(vmem_limit_bytes=VMEM_LIMIT),
        name="ctx_fourier",
    )(ctx, mods, f, ch, w_bf16)


def _load_swiglu_weights(e, wi_hbm, wo_hbm, wi_res, wo_res, stage_i, stage_o, sems):
    n_i = wi_res.shape[1] // W_CHUNK
    total = n_i + wo_res.shape[0] // W_CHUNK

    def copy(c):
        slot = c % 2
        if c < n_i:
            src = wi_hbm.at[e, :, pl.ds(c * W_CHUNK, W_CHUNK)]
            return pltpu.make_async_copy(src, stage_i.at[slot], sems.at[slot])
        src = wo_hbm.at[e, pl.ds((c - n_i) * W_CHUNK, W_CHUNK), :]
        return pltpu.make_async_copy(src, stage_o.at[slot], sems.at[slot])

    copy(0).start()
    for c in range(total):
        if c + 1 < total:
            copy(c + 1).start()
        copy(c).wait()
        if c < n_i:
            wi_res[:, c * W_CHUNK:(c + 1) * W_CHUNK] = stage_i[c % 2].astype(BF16)
        else:
            k = c - n_i
            wo_res[k * W_CHUNK:(k + 1) * W_CHUNK, :] = stage_o[c % 2].astype(BF16)


def _swiglu_tile(h, wi_res, wo_res, act):
    for k in range(D_FF // FF_CHUNK):
        gate = _dot(h, wi_res[:, k * FF_CHUNK:(k + 1) * FF_CHUNK])
        up = _dot(h, wi_res[:, D_FF + k * FF_CHUNK:D_FF + (k + 1) * FF_CHUNK])
        act[:, k * FF_CHUNK:(k + 1) * FF_CHUNK] = (gate / (1.0 + jnp.exp(-gate)) * up).astype(BF16)
    return _dot(act[...], wo_res[...])


def _swiglu_scratch(tm):
    return [pltpu.VMEM((D_MODEL, 2 * D_FF), BF16), pltpu.VMEM((D_FF, D_MODEL), BF16),
            pltpu.VMEM((2, D_MODEL, W_CHUNK), F32), pltpu.VMEM((2, W_CHUNK, D_MODEL), F32),
            pltpu.VMEM((tm, D_FF), BF16), pltpu.SemaphoreType.DMA((2,))]


def _ffn_body(layer, x_ref, mods_ref, wi_hbm, wo_hbm, o_ref, wi_res, wo_res, stage_i, stage_o,
              act, sems):
    @pl.when(pl.program_id(0) == 0)
    def _():
        _load_swiglu_weights(layer, wi_hbm, wo_hbm, wi_res, wo_res, stage_i, stage_o, sems)

    x = x_ref[...]
    h = _mod_norm(x, mods_ref[...]).astype(BF16)
    o_ref[...] = x + mods_ref[3:4, :] * _swiglu_tile(h, wi_res, wo_res, act)


def _ffn_layer(x, mods, wi_all, wo_all, layer):
    n = x.shape[0]
    tm = min(ROW_TILE, n)
    return pl.pallas_call(
        functools.partial(_ffn_body, layer),
        grid=(n // tm,),
        in_specs=[pl.BlockSpec((tm, D_MODEL), lambda i: (i, 0)),
                  pl.BlockSpec((SUBLANES, D_MODEL), lambda i: (0, 0)),
                  pl.BlockSpec(memory_space=pl.ANY),
                  pl.BlockSpec(memory_space=pl.ANY)],
        out_specs=pl.BlockSpec((tm, D_MODEL), lambda i: (i, 0)),
        out_shape=jax.ShapeDtypeStruct(x.shape, F32),
        scratch_shapes=_swiglu_scratch(tm),
        compiler_params=_cparams(("arbitrary",), VMEM_LIMIT),
        name="ffn",
    )(x, mods, wi_all, wo_all)


def _qkv_body(x_ref, mods_ref, w_ref, bd_ref, gains_ref, cos_ref, sin_ref, q_ref, k_ref, v_ref):
    h = _mod_norm(x_ref[...], mods_ref[...]).astype(BF16)
    qkv = _dot(h, w_ref[...])
    bd = bd_ref[...]
    cos, sin = cos_ref[...], sin_ref[...]
    lane = lax.broadcasted_iota(jnp.int32, (1, LANES), 1)
    first_half = (lane % (HEAD_DIM // 2)) < (HEAD_DIM // 4)
    low_head = lane < HEAD_DIM

    def norm_rope(a, gain):
        sq = a * a
        hi = sq.astype(BF16)
        lo = (sq - hi.astype(F32)).astype(BF16)
        ms = _dot(hi, bd) + _dot(lo, bd)
        an = a * lax.rsqrt(ms + NORM_EPS) * gain
        partner = jnp.where(first_half,
                            pltpu.roll(an, LANES - HEAD_DIM // 4, 1),
                            pltpu.roll(an, HEAD_DIM // 4, 1))
        return an * cos + partner * sin

    def dup_heads(a):
        r = pltpu.roll(a, HEAD_DIM, 1)
        return jnp.where(low_head, a, r), jnp.where(low_head, r, a)

    nq = Q_COLS // LANES
    for t in range(nq):
        a = norm_rope(qkv[:, t * LANES:(t + 1) * LANES], gains_ref[0:1, :])
        q_ref[:, t * LANES:(t + 1) * LANES] = a.astype(BF16)
    for t in range(KV_COLS // LANES):
        kt = norm_rope(qkv[:, Q_COLS + t * LANES:Q_COLS + (t + 1) * LANES], gains_ref[1:2, :])
        k0, k1 = dup_heads(kt)
        k_ref[2 * t] = k0.astype(BF16)
        k_ref[2 * t + 1] = k1.astype(BF16)
        v0, v1 = dup_heads(qkv[:, Q_COLS + KV_COLS + t * LANES:Q_COLS + KV_COLS + (t + 1) * LANES])
        for hk, vv in ((2 * t, v0), (2 * t + 1, v1)):
            v_ref[2 * hk] = jnp.where(low_head, vv, 1.0).astype(BF16)
            v_ref[2 * hk + 1] = jnp.where(low_head, 1.0, vv).astype(BF16)


def _qkv_call(x, mods, wqkv_bf16, bd, gains, cos_t, sin_t):
    n = x.shape[0]
    tm = min(ROW_TILE, n)
    ncol = Q_COLS + 2 * KV_COLS
    return pl.pallas_call(
        _qkv_body,
        grid=(n // tm,),
        in_specs=[pl.BlockSpec((tm, D_MODEL), lambda i: (i, 0)),
                  pl.BlockSpec((SUBLANES, D_MODEL), lambda i: (0, 0)),
                  pl.BlockSpec((D_MODEL, ncol), lambda i: (0, 0)),
                  pl.BlockSpec((LANES, LANES), lambda i: (0, 0)),
                  pl.BlockSpec((SUBLANES, LANES), lambda i: (0, 0)),
                  pl.BlockSpec((tm, LANES), lambda i: (i, 0)),
                  pl.BlockSpec((tm, LANES), lambda i: (i, 0))],
        out_specs=[pl.BlockSpec((tm, Q_COLS), lambda i: (i, 0)),
                   pl.BlockSpec((N_KV_HEADS, tm, LANES), lambda i: (0, i, 0)),
                   pl.BlockSpec((2 * N_KV_HEADS, tm, LANES), lambda i: (0, i, 0))],
        out_shape=[jax.ShapeDtypeStruct((n, Q_COLS), BF16),
                   jax.ShapeDtypeStruct((N_KV_HEADS, n, LANES), BF16),
                   jax.ShapeDtypeStruct((2 * N_KV_HEADS, n, LANES), BF16)],
        compiler_params=_cparams(("arbitrary",), VMEM_LIMIT),
        name="qkv",
    )(x, mods, wqkv_bf16, bd, gains, cos_t, sin_t)


def _attn_body(sink_ref, q_ref, kp_ref, kc_ref, kn_ref, vp_ref, vc_ref, vn_ref,
               kx_ref, vx_ref, o_ref):
    b = pl.program_id(0)
    nb = pl.num_programs(0)
    blk = ATTN_BLOCK
    lane = lax.broadcasted_iota(jnp.int32, (1, LANES), 1)
    low_head = lane < HEAD_DIM
    qi = lax.broadcasted_iota(jnp.int32, (blk, 3 * blk), 0)
    kj = lax.broadcasted_iota(jnp.int32, (blk, 3 * blk), 1)
    valid = (kj >= qi) & (kj <= qi + 2 * WINDOW)
    valid = valid & ((kj >= blk) | (b > 0)) & ((kj < 2 * blk) | (b < nb - 1))
    per_kv = N_HEADS // N_KV_HEADS
    for g in range(N_KV_HEADS):
        kb = jnp.concatenate([kp_ref[g], kc_ref[g], kn_ref[g], kx_ref[g]], axis=0)
        vbs = [jnp.concatenate([vp_ref[2 * g + a], vc_ref[2 * g + a], vn_ref[2 * g + a],
                                vx_ref[2 * g + a]], axis=0) for a in range(2)]
        parts = []
        for p in range(per_kv // 2):
            t = g * (per_kv // 2) + p
            qt = q_ref[:, t * LANES:(t + 1) * LANES]
            zero = jnp.zeros_like(qt)
            parts += [jnp.where(low_head, qt, zero), jnp.where(low_head, zero, qt)]
        lhs = jnp.concatenate(parts, axis=0)
        s = lax.dot_general(lhs, kb, (((1,), (1,)), ((), ())), preferred_element_type=F32)
        outs = []
        for hh in range(per_kv):
            sink = sink_ref[g * per_kv + hh] * LOG2_E
            sh = s[hh * blk:(hh + 1) * blk]
            s_loc = jnp.where(valid, sh[:, :3 * blk], NEG_INF)
            s_ctx = sh[:, 3 * blk:]
            m = jnp.maximum(jnp.maximum(jnp.max(s_loc, axis=-1, keepdims=True),
                                        jnp.max(s_ctx, axis=-1, keepdims=True)), sink)
            pr = jnp.concatenate([jnp.exp2(s_loc - m), jnp.exp2(s_ctx - m)], axis=1).astype(BF16)
            pv = _dot(pr, vbs[hh % 2])
            den = pltpu.roll(pv, HEAD_DIM, 1) + jnp.exp2(sink - m)
            outs.append(pv / den)
        for p in range(per_kv // 2):
            t = g * (per_kv // 2) + p
            o_ref[:, t * LANES:(t + 1) * LANES] = jnp.where(
                low_head, outs[2 * p], outs[2 * p + 1]).astype(BF16)


def _attn_call(sink, q, kd, vd, kx, vx):
    n = q.shape[0]
    blk = ATTN_BLOCK
    nb = n // blk
    nctx = kx.shape[1]
    k_spec = lambda f: pl.BlockSpec((N_KV_HEADS, blk, LANES), f)
    v_spec = lambda f: pl.BlockSpec((2 * N_KV_HEADS, blk, LANES), f)
    prev = lambda b: (0, jnp.maximum(b - 1, 0), 0)
    cur = lambda b: (0, b, 0)
    nxt = lambda b: (0, jnp.minimum(b + 1, nb - 1), 0)
    return pl.pallas_call(
        _attn_body,
        grid=(nb,),
        in_specs=[pl.BlockSpec(memory_space=pltpu.SMEM),
                  pl.BlockSpec((blk, Q_COLS), lambda b: (b, 0)),
                  k_spec(prev), k_spec(cur), k_spec(nxt),
                  v_spec(prev), v_spec(cur), v_spec(nxt),
                  pl.BlockSpec((N_KV_HEADS, nctx, LANES), lambda b: (0, 0, 0)),
                  pl.BlockSpec((2 * N_KV_HEADS, nctx, LANES), lambda b: (0, 0, 0))],
        out_specs=pl.BlockSpec((blk, Q_COLS), lambda b: (b, 0)),
        out_shape=jax.ShapeDtypeStruct((n, Q_COLS), BF16),
        compiler_params=_cparams(("arbitrary",), VMEM_LIMIT),
        name="attn",
    )(sink, q, kd, kd, kd, vd, vd, vd, kx, vx)


def _proj_body(a_ref, x_ref, mods_ref, w_ref, o_ref):
    o_ref[...] = x_ref[...] + mods_ref[3:4, :] * _dot(a_ref[...], w_ref[...])


def _proj_call(a, x, mods, w_bf16):
    n = x.shape[0]
    tm = min(ROW_TILE, n)
    return pl.pallas_call(
        _proj_body,
        grid=(n // tm,),
        in_specs=[pl.BlockSpec((tm, a.shape[1]), lambda i: (i, 0)),
                  pl.BlockSpec((tm, D_MODEL), lambda i: (i, 0)),
                  pl.BlockSpec((SUBLANES, D_MODEL), lambda i: (0, 0)),
                  pl.BlockSpec(w_bf16.shape, lambda i: (0, 0))],
        out_specs=pl.BlockSpec((tm, D_MODEL), lambda i: (i, 0)),
        out_shape=jax.ShapeDtypeStruct(x.shape, F32),
        compiler_params=_cparams(("arbitrary",), VMEM_LIMIT),
        name="proj",
    )(a, x, mods, w_bf16)


def _rope_lane_tables(length):
    rows = length // GRID_W
    row_pos = jnp.repeat(jnp.arange(rows, dtype=F32), GRID_W)
    col_pos = jnp.tile(jnp.arange(GRID_W, dtype=F32), rows)
    axis_dim = HEAD_DIM // 2
    inv_freq = ROPE_BASE ** (-jnp.arange(0, axis_dim, 2, dtype=F32) / axis_dim)
    ang_r = row_pos[:, None] * inv_freq[None, :]
    ang_c = col_pos[:, None] * inv_freq[None, :]
    cos_h = jnp.concatenate([jnp.cos(ang_r), jnp.cos(ang_r), jnp.cos(ang_c), jnp.cos(ang_c)], axis=1)
    sin_h = jnp.concatenate([-jnp.sin(ang_r), jnp.sin(ang_r), -jnp.sin(ang_c), jnp.sin(ang_c)], axis=1)
    reps = LANES // HEAD_DIM
    return jnp.tile(cos_h, (1, reps)), jnp.tile(sin_h, (1, reps))


def _attention_layer(x, ctx, mods_x, mods_c, wqkv, wo, q_gain, k_gain, sink):
    L = x.shape[0]
    nctx = ctx.shape[0]
    wqkv_b = wqkv.astype(BF16)
    head = np.arange(LANES) // HEAD_DIM
    bd = jnp.asarray((head[:, None] == head[None, :]).astype(np.float32) / HEAD_DIM).astype(BF16)
    reps = LANES // HEAD_DIM
    gains = jnp.zeros((SUBLANES, LANES), F32)
    gains = gains.at[0].set(jnp.tile(q_gain, reps) * (HEAD_DIM ** -0.5 * LOG2_E))
    gains = gains.at[1].set(jnp.tile(k_gain, reps))
    cos_t, sin_t = _rope_lane_tables(L)
    q, kd, vd = _qkv_call(x, mods_x, wqkv_b, bd, gains, cos_t, sin_t)
    ones = jnp.ones((nctx, LANES), F32)
    _, kx, vx = _qkv_call(ctx, mods_c, wqkv_b, bd, gains, ones, jnp.zeros_like(ones))
    o = _attn_call(sink, q, kd, vd, kx, vx)
    return _proj_call(o, x, mods_x, wo.astype(BF16))


def _pool_body(xp_ref, xc_ref, xn_ref, mods_ref, w_ref, o_ref, h_scr):
    i = pl.program_id(0)
    tm = xc_ref.shape[0]
    total = tm * pl.num_programs(0)
    x = xc_ref[...]
    mods = mods_ref[...]
    h_scr[0:POOL_HALO, :] = jnp.where(i > 0, _mod_norm(xp_ref[...], mods), 0.0)
    h_scr[POOL_HALO:POOL_HALO + tm, :] = _mod_norm(x, mods)
    h_scr[POOL_HALO + tm:, :] = jnp.where(i < pl.num_programs(0) - 1, _mod_norm(xn_ref[...], mods), 0.0)
    t = i * tm + lax.broadcasted_iota(jnp.int32, (tm, 1), 0)
    gd = POOL_GROUP_DIM
    for g, win in enumerate(POOL_WINDOWS):
        sl = slice(g * gd, (g + 1) * gd)
        half = win // 2
        tot = h_scr[POOL_HALO - half:POOL_HALO - half + tm, sl]
        for s in range(-half + 1, half):
            tot = tot + h_scr[POOL_HALO + s:POOL_HALO + s + tm, sl]
        lo = jnp.maximum(t - half, 0)
        hi = jnp.minimum(t + half - 1, total - 1)
        cnt = (hi - lo + 1).astype(F32)
        pooled = (tot / cnt - h_scr[POOL_HALO:POOL_HALO + tm, sl]).astype(BF16)
        y = _dot(pooled, w_ref[g]) * mods[4:5, sl]
        o_ref[:, sl] = x[:, sl] + mods[3:4, sl] * y


def _pool_layer(x, mods, w_bf16):
    n = x.shape[0]
    tm = min(ROW_TILE, n)
    r = tm // POOL_HALO
    last = n // POOL_HALO - 1
    return pl.pallas_call(
        _pool_body,
        grid=(n // tm,),
        in_specs=[pl.BlockSpec((POOL_HALO, D_MODEL), lambda i: (jnp.maximum(i * r - 1, 0), 0)),
                  pl.BlockSpec((tm, D_MODEL), lambda i: (i, 0)),
                  pl.BlockSpec((POOL_HALO, D_MODEL), lambda i: (jnp.minimum((i + 1) * r, last), 0)),
                  pl.BlockSpec((SUBLANES, D_MODEL), lambda i: (0, 0)),
                  pl.BlockSpec(w_bf16.shape, lambda i: (0, 0, 0))],
        out_specs=pl.BlockSpec((tm, D_MODEL), lambda i: (i, 0)),
        out_shape=jax.ShapeDtypeStruct(x.shape, F32),
        scratch_shapes=[pltpu.VMEM((tm + 2 * POOL_HALO, D_MODEL), F32)],
        compiler_params=_cparams(("arbitrary",), VMEM_LIMIT),
        name="pool",
    )(x, x, x, mods, w_bf16)


def _router_body(x_ref, mods_ref, r_ref, h_ref, route_ref, cnt_ref):
    h = _mod_norm(x_ref[...], mods_ref[...])
    h_hi = h.astype(BF16)
    h_ref[...] = h_hi
    h_lo = (h - h_hi.astype(F32)).astype(BF16)
    r = r_ref[...]
    r_hi = r.astype(BF16)
    r_lo = (r - r_hi.astype(F32)).astype(BF16)
    logits = _dot(h_hi, r_hi) + (_dot(h_lo, r_hi) + _dot(h_hi, r_lo))
    lane = lax.broadcasted_iota(jnp.int32, logits.shape, 1)
    logits = jnp.where(lane < N_EXPERTS, logits, -jnp.inf)
    m1 = jnp.max(logits, axis=-1, keepdims=True)
    i1 = jnp.min(jnp.where(logits == m1, lane, LANES), axis=-1, keepdims=True)
    rest = jnp.where(lane == i1, -jnp.inf, logits)
    m2 = jnp.max(rest, axis=-1, keepdims=True)
    i2 = jnp.min(jnp.where(rest == m2, lane, LANES), axis=-1, keepdims=True)
    e = jnp.exp(m2 - m1)
    w1 = 1.0 / (1.0 + e)
    w2 = e / (1.0 + e)
    tm = h.shape[0]
    oh = jnp.where((lane == i1) | (lane == i2 + N_EXPERTS), 1.0, 0.0)
    before = (lax.broadcasted_iota(jnp.int32, (tm, tm), 1)
              < lax.broadcasted_iota(jnp.int32, (tm, tm), 0))
    prior = _dot(jnp.where(before, 1.0, 0.0).astype(BF16), oh.astype(BF16))
    cnt = jnp.sum(oh, axis=0, keepdims=True)
    r1 = jnp.sum(jnp.where(lane == i1, prior, 0.0), axis=-1, keepdims=True)
    r2 = (jnp.sum(jnp.where(lane == i2 + N_EXPERTS, prior, 0.0), axis=-1, keepdims=True)
          + jnp.sum(jnp.where(lane == i2, cnt, 0.0), axis=-1, keepdims=True))
    lane8 = lane[0:SUBLANES, :]
    cnt8 = jnp.broadcast_to(cnt, (SUBLANES, LANES))
    both = cnt8 + pltpu.roll(cnt8, LANES - N_EXPERTS, 1)
    seg_len = jnp.where(lane8 < N_EXPERTS, jnp.ceil(both * (1.0 / SEG_ALIGN)) * SEG_ALIGN, 0.0)
    incl = seg_len
    shift = 1
    while shift < N_EXPERTS:
        incl = incl + jnp.where(lane8 >= shift, pltpu.roll(incl, shift, 1), 0.0)
        shift *= 2
    seg = incl - seg_len
    p1 = r1 + jnp.sum(jnp.where(lane == i1, seg[0:1, :], 0.0), axis=-1, keepdims=True)
    p2 = r2 + jnp.sum(jnp.where(lane == i2, seg[0:1, :], 0.0), axis=-1, keepdims=True)
    col = lax.broadcasted_iota(jnp.int32, route_ref.shape, 1)
    vals = (i1.astype(F32), i2.astype(F32), w1, w2, p1, p2)
    out = jnp.zeros(route_ref.shape, F32)
    for k, v in enumerate(vals):
        out = jnp.where(col == k, v, out)
    route_ref[...] = out
    row = lax.broadcasted_iota(jnp.int32, cnt_ref.shape[1:], 0)
    cnt_ref[0] = jnp.where(row == 0, seg_len, jnp.where(row == 1, seg, 0.0))


def _router_call(x, mods, router_pad):
    n = x.shape[0]
    tm = min(ROUTE_TILE, n)
    return pl.pallas_call(
        _router_body,
        grid=(n // tm,),
        in_specs=[pl.BlockSpec((tm, D_MODEL), lambda i: (i, 0)),
                  pl.BlockSpec((SUBLANES, D_MODEL), lambda i: (0, 0)),
                  pl.BlockSpec((D_MODEL, LANES), lambda i: (0, 0))],
        out_specs=[pl.BlockSpec((tm, D_MODEL), lambda i: (i, 0)),
                   pl.BlockSpec((tm, SUBLANES), lambda i: (i, 0)),
                   pl.BlockSpec((1, SUBLANES, LANES), lambda i: (i, 0, 0))],
        out_shape=[jax.ShapeDtypeStruct((n, D_MODEL), BF16),
                   jax.ShapeDtypeStruct((n, SUBLANES), F32),
                   jax.ShapeDtypeStruct((n // tm, SUBLANES, LANES), F32)],
        compiler_params=_cparams(("arbitrary",), VMEM_LIMIT),
        name="router",
    )(x, mods, router_pad)


def _segment_copies(i, seg_ref, dst_ref, len_ref, tile_buf, sorted_hbm, sem, to_hbm):
    for e in range(N_EXPERTS):
        length = len_ref[i * N_EXPERTS + e]
        seg = seg_ref[i * N_EXPERTS + e]
        dst = dst_ref[i * N_EXPERTS + e]
        size = SEG_ALIGN
        while size <= tile_buf.shape[0] // 2:
            done = length & ~(2 * size - 1)
            a = tile_buf.at[pl.ds(pl.multiple_of(seg + done, SEG_ALIGN), size)]
            b = sorted_hbm.at[pl.ds(pl.multiple_of(dst + done, SEG_ALIGN), size)]
            copy = pltpu.make_async_copy(a, b, sem) if to_hbm else pltpu.make_async_copy(b, a, sem)
            yield (length & size) != 0, copy
            size *= 2


def _scatter_body(seg_ref, dst_ref, len_ref, h_ref, prow_ref, xs_in_ref, xs_ref, sbuf, sems):
    del xs_in_ref
    i = pl.program_id(0)
    last = pl.num_programs(0) - 1
    slot = i % 2
    cap, tm = sbuf.shape[1], h_ref.shape[0]
    r = lax.broadcasted_iota(jnp.int32, (cap, tm), 0)
    hit = (r == prow_ref[0:1, :]) | (r == prow_ref[1:2, :])
    sel = jnp.where(hit, 1.0, 0.0).astype(BF16)
    sbuf[slot] = _dot(sel, h_ref[...]).astype(BF16)

    def copies(tile, s):
        return _segment_copies(tile, seg_ref, dst_ref, len_ref, sbuf.at[s], xs_ref, sems.at[s], True)

    for pred, copy in copies(i, slot):
        pl.when(pred)(copy.start)
    for pred, copy in copies(jnp.maximum(i - 1, 0), 1 - slot):
        pl.when(pred & (i > 0))(copy.wait)
    for pred, copy in copies(i, slot):
        pl.when(pred & (i == last))(copy.wait)


def _route_cap(tm):
    return 2 * tm + N_EXPERTS * SEG_ALIGN


def _scatter_call(seg, dst, seg_len, h, prow, n_rows):
    n = h.shape[0]
    tm = min(ROUTE_TILE, n)
    zeros = jnp.zeros((n_rows, D_MODEL), BF16)
    return pl.pallas_call(
        _scatter_body,
        grid_spec=pltpu.PrefetchScalarGridSpec(
            num_scalar_prefetch=3,
            grid=(n // tm,),
            in_specs=[pl.BlockSpec((tm, D_MODEL), lambda i, *_: (i, 0)),
                      pl.BlockSpec((SUBLANES, tm), lambda i, *_: (i, 0)),
                      pl.BlockSpec(memory_space=pl.ANY)],
            out_specs=pl.BlockSpec(memory_space=pl.ANY),
            scratch_shapes=[pltpu.VMEM((2, _route_cap(tm), D_MODEL), BF16),
                            pltpu.SemaphoreType.DMA((2,))]),
        out_shape=jax.ShapeDtypeStruct((n_rows, D_MODEL), BF16),
        input_output_aliases={5: 0},
        compiler_params=_cparams(("arbitrary",), VMEM_LIMIT),
        name="moe_scatter",
    )(seg, dst, seg_len, h, prow, zeros)


def _moe_body(te_ref, tv_ref, tf_ref, xs_ref, wi_hbm, wo_hbm, ys_ref, wi_res, wo_res, stage_i,
              stage_o, act, sems):
    i = pl.program_id(0)

    @pl.when(tf_ref[i] > 0)
    def _():
        _load_swiglu_weights(te_ref[i], wi_hbm, wo_hbm, wi_res, wo_res, stage_i, stage_o, sems)

    @pl.when(tv_ref[i] > 0)
    def _():
        ys_ref[...] = _swiglu_tile(xs_ref[...], wi_res, wo_res, act).astype(BF16)

    @pl.when(tv_ref[i] == 0)
    def _():
        ys_ref[...] = jnp.zeros(ys_ref.shape, BF16)


def _moe_call(tile_expert, tile_live, tile_first, xs, wi_all, wo_all):
    n_rows = xs.shape[0]
    return pl.pallas_call(
        _moe_body,
        grid_spec=pltpu.PrefetchScalarGridSpec(
            num_scalar_prefetch=3,
            grid=(n_rows // MOE_TILE,),
            in_specs=[pl.BlockSpec((MOE_TILE, D_MODEL), lambda i, *_: (i, 0)),
                      pl.BlockSpec(memory_space=pl.ANY),
                      pl.BlockSpec(memory_space=pl.ANY)],
            out_specs=pl.BlockSpec((MOE_TILE, D_MODEL), lambda i, *_: (i, 0)),
            scratch_shapes=_swiglu_scratch(MOE_TILE)),
        out_shape=jax.ShapeDtypeStruct((n_rows, D_MODEL), BF16),
        compiler_params=_cparams(("arbitrary",), VMEM_LIMIT),
        name="moe_ffn",
    )(tile_expert, tile_live, tile_first, xs, wi_all, wo_all)


def _combine_body(seg_ref, dst_ref, len_ref, x_ref, route_ref, pcol_ref, mods_ref, ys_ref, o_ref,
                  ybuf, sems):
    i = pl.program_id(0)
    last = pl.num_programs(0) - 1
    slot = i % 2
    tm, cap = x_ref.shape[0], ybuf.shape[1]

    def copies(tile, s):
        return _segment_copies(tile, seg_ref, dst_ref, len_ref, ybuf.at[s], ys_ref, sems.at[s], False)

    @pl.when(i == 0)
    def _():
        ybuf[...] = jnp.zeros(ybuf.shape, BF16)

    for pred, copy in copies(0, 0):
        pl.when(pred & (i == 0))(copy.start)
    for pred, copy in copies(jnp.minimum(i + 1, last), 1 - slot):
        pl.when(pred & (i < last))(copy.start)
    for pred, copy in copies(i, slot):
        pl.when(pred)(copy.wait)
    c = lax.broadcasted_iota(jnp.int32, (tm, cap), 1)
    pcol = pcol_ref[...]
    route = route_ref[...]
    sel = jnp.where(c == pcol[:, 0:1], route[:, 2:3], jnp.where(c == pcol[:, 1:2], route[:, 3:4], 0.0))
    o_ref[...] = x_ref[...] + mods_ref[3:4, :] * _dot(sel.astype(BF16), ybuf[slot])


def _combine_call(seg, dst, seg_len, x, route, pcol, mods, ys):
    n = x.shape[0]
    tm = min(ROUTE_TILE, n)
    return pl.pallas_call(
        _combine_body,
        grid_spec=pltpu.PrefetchScalarGridSpec(
            num_scalar_prefetch=3,
            grid=(n // tm,),
            in_specs=[pl.BlockSpec((tm, D_MODEL), lambda i, *_: (i, 0)),
                      pl.BlockSpec((tm, SUBLANES), lambda i, *_: (i, 0)),
                      pl.BlockSpec((tm, SUBLANES), lambda i, *_: (i, 0)),
                      pl.BlockSpec((SUBLANES, D_MODEL), lambda i, *_: (0, 0)),
                      pl.BlockSpec(memory_space=pl.ANY)],
            out_specs=pl.BlockSpec((tm, D_MODEL), lambda i, *_: (i, 0)),
            scratch_shapes=[pltpu.VMEM((2, _route_cap(tm), D_MODEL), BF16),
                            pltpu.SemaphoreType.DMA((2,))]),
        out_shape=jax.ShapeDtypeStruct(x.shape, F32),
        compiler_params=_cparams(("arbitrary",), VMEM_LIMIT),
        name="moe_combine",
    )(seg, dst, seg_len, x, route, pcol, mods, ys)


def _routing_tables(route, counts, tm, n_tiles):
    nt = counts.shape[0]
    seg_len = counts[:, 0, :N_EXPERTS].astype(jnp.int32)
    seg = counts[:, 1, :N_EXPERTS].astype(jnp.int32)
    tiles = (jnp.sum(seg_len, axis=0) + MOE_TILE - 1) // MOE_TILE
    tile_end = jnp.cumsum(tiles)
    start = (tile_end - tiles) * MOE_TILE
    dst = start[None, :] + jnp.cumsum(seg_len, axis=0) - seg_len
    t = jnp.arange(n_tiles, dtype=jnp.int32)
    expert = jnp.sum((t[:, None] >= tile_end[None, :]).astype(jnp.int32), axis=1)
    live = (t < tile_end[-1]).astype(jnp.int32)
    last_expert = jnp.sum((tile_end[-1] - 1 >= tile_end).astype(jnp.int32))
    expert = jnp.where(live > 0, expert, last_expert).astype(jnp.int32)
    prev = jnp.concatenate([jnp.full((1,), -1, jnp.int32), expert[:-1]])
    first = (live * (expert != prev)).astype(jnp.int32)
    p12 = route[:, 4:6].astype(jnp.int32)
    pcol = jnp.concatenate([p12, jnp.zeros((p12.shape[0], SUBLANES - 2), jnp.int32)], axis=1)
    prow = jnp.concatenate([p12.reshape(nt, tm, 2).transpose(0, 2, 1),
                            jnp.full((nt, SUBLANES - 2, tm), -1, jnp.int32)], axis=1)
    return (seg.reshape(-1), dst.reshape(-1).astype(jnp.int32), seg_len.reshape(-1),
            expert, live, first, pcol, prow.reshape(nt * SUBLANES, tm))


def _moe_layer(x, mods, router, wi_all, wo_all, layer):
    n = x.shape[0]
    tm = min(ROUTE_TILE, n)
    router_pad = jnp.zeros((D_MODEL, LANES), F32).at[:, :N_EXPERTS].set(router)
    h, route, counts = _router_call(x, mods, router_pad)
    max_rows = 2 * n + (n // tm) * N_EXPERTS * (SEG_ALIGN - 1)
    n_tiles = -(-max_rows // MOE_TILE) + N_EXPERTS
    seg, dst, seg_len, tile_expert, tile_live, tile_first, pcol, prow = _routing_tables(
        route, counts, tm, n_tiles)
    xs = _scatter_call(seg, dst, seg_len, h, prow, n_tiles * MOE_TILE)
    ys = _moe_call(tile_expert + layer * N_EXPERTS, tile_live, tile_first, xs, wi_all, wo_all)
    return _combine_call(seg, dst, seg_len, x, route, pcol, mods, ys)


def _mods(gain, shift, scale, gate, extra=None):
    rows = [gain, shift, scale, gate, extra if extra is not None else jnp.zeros_like(gain)]
    m = jnp.stack(rows, axis=0)
    return jnp.concatenate([m, jnp.zeros((SUBLANES - m.shape[0], m.shape[1]), F32)], axis=0)


def kernel(x, c, ctx, c_ctx, ada_w, ada_b, norm_mix, norm_ffn, fnet_w, attn_wqkv, attn_q_gain,
           attn_k_gain, attn_sink, attn_wo, pool_w, pool_scale, ffn_wi, ffn_wo, moe_router,
           moe_wi, moe_wo):
    assert x.shape[0] == 1 and x.shape[2] == D_MODEL
    depth = ada_w.shape[0]
    xs = x[0]
    cs = ctx[0]
    ada = _ada_call(jnp.stack([c[0], c_ctx]), ada_w, ada_b)

    attn_layers = [i for i in range(depth) if i % N_MIXERS == 1]
    last_ctx_read = attn_layers[-1] if attn_layers else -1
    n_side = math.isqrt(xs.shape[0])
    assert n_side * n_side == xs.shape[0] and n_side % SUBLANES == 0
    seq_tables = _sequence_tables(n_side)
    ch = _channel_table()
    moe_wi_all = moe_wi.reshape((-1,) + moe_wi.shape[2:])
    moe_wo_all = moe_wo.reshape((-1,) + moe_wo.shape[2:])

    for i in range(depth):
        mixer = i % N_MIXERS
        j = i // N_MIXERS
        f = i // 2
        ctx_full = i < last_ctx_read
        ctx_live = i <= last_ctx_read
        sh1, sc1, g1, sh2, sc2, g2 = [ada[i, 0, k * D_MODEL:(k + 1) * D_MODEL] for k in range(6)]
        csh1, csc1, cg1, csh2, csc2, cg2 = [ada[i, 1, k * D_MODEL:(k + 1) * D_MODEL] for k in range(6)]
        extra = pool_scale[j] if mixer == 2 else None
        m1 = _mods(norm_mix[i], sh1, sc1, g1, extra)
        m2 = _mods(norm_ffn[i], sh2, sc2, g2)
        cm1 = _mods(norm_mix[i], csh1, csc1, cg1, extra)
        cm2 = _mods(norm_ffn[i], csh2, csc2, cg2)

        if mixer == 0:
            wb = fnet_w[j].astype(BF16)
            xs = _fourier_layer(xs, m1, wb, seq_tables, ch)
            if ctx_full:
                cs = _ctx_fourier_layer(cs, cm1, wb, ch)
        elif mixer == 1:
            assert ctx_live and not ctx_full
            xs = _attention_layer(xs, cs, m1, cm1, attn_wqkv[j], attn_wo[j], attn_q_gain[j],
                                  attn_k_gain[j], attn_sink[j])
        else:
            wb = pool_w[j].astype(BF16)
            xs = _pool_layer(xs, m1, wb)
            if ctx_full:
                cs = _pool_layer(cs, cm1, wb)

        if i % 2 == 0:
            xs = _ffn_layer(xs, m2, ffn_wi, ffn_wo, f)
            if ctx_full:
                cs = _ffn_layer(cs, cm2, ffn_wi, ffn_wo, f)
        else:
            xs = _moe_layer(xs, m2, moe_router[f], moe_wi_all, moe_wo_all, f)
            if ctx_full:
                cs = _moe_layer(cs, cm2, moe_router[f], moe_wi_all, moe_wo_all, f)
    return xs[None]
```

```python
import functools
import math

import numpy as np
import jax
import jax.numpy as jnp
from jax import lax
from jax.experimental import pallas as pl
from jax.experimental.pallas import tpu as pltpu

F32 = jnp.float32
BF16 = jnp.bfloat16

D_MODEL = 1024
GRID_W = 64
N_MIXERS = 3
FNET_GROUPS = 4
FNET_GROUP_DIM = D_MODEL // FNET_GROUPS
N_HEADS = 16
N_KV_HEADS = 4
HEAD_DIM = 64
Q_COLS = N_HEADS * HEAD_DIM
KV_COLS = N_KV_HEADS * HEAD_DIM
WINDOW = 128
ATTN_BLOCK = 128
ROPE_BASE = 10000.0
POOL_WINDOWS = (2, 4, 8, 16)
POOL_GROUP_DIM = D_MODEL // len(POOL_WINDOWS)
POOL_HALO = 8
D_FF = 3584
N_EXPERTS = 8
NORM_EPS = 1e-6
NEG_INF = -1e30
LOG2_E = 1.4426950408889634

LANES = 128
SUBLANES = 8
VMEM_LIMIT = 56 * 1024 * 1024

ROW_TILE = 512
FF_CHUNK = 512
MOE_TILE = 256
W_CHUNK = 512
ROUTE_TILE = 512
SEG_ALIGN = 16
N_COND = 2


def _cparams(sem, vmem=None):
    return pltpu.CompilerParams(dimension_semantics=sem, vmem_limit_bytes=vmem)


def _mod_norm(x, mods):
    ms = jnp.mean(x * x, axis=-1, keepdims=True)
    y = x * lax.rsqrt(ms + NORM_EPS) * mods[0:1, :]
    return y * (1.0 + mods[2:3, :]) + mods[1:2, :]


def _dot(a, b):
    return jnp.dot(a, b, preferred_element_type=F32)


def _ada_body(cc_ref, w_ref, b_ref, o_ref):
    n_tiles = w_ref.shape[-1] // LANES
    d = w_ref.shape[1]
    acc = [[jnp.zeros((SUBLANES, LANES), F32) for _ in range(N_COND)] for _ in range(n_tiles)]
    for k in range(d // LANES):
        rows = slice(k * LANES, (k + 1) * LANES)
        a = cc_ref[:, rows, :]
        a = a / (1.0 + jnp.exp(-a))
        for t in range(n_tiles):
            wt = w_ref[0, rows, t * LANES:(t + 1) * LANES]
            for r in range(N_COND):
                acc[t][r] = acc[t][r] + jnp.sum(
                    (wt * a[r]).reshape(LANES // SUBLANES, SUBLANES, LANES), axis=0)
    row = lax.broadcasted_iota(jnp.int32, (SUBLANES, LANES), 0)
    for t in range(n_tiles):
        sl = slice(t * LANES, (t + 1) * LANES)
        tile = jnp.zeros((SUBLANES, LANES), F32)
        for r in range(N_COND):
            s = jnp.sum(acc[t][r], axis=0, keepdims=True) + b_ref[0, :, sl]
            tile = jnp.where(row == r, s, tile)
        o_ref[0, :, sl] = tile


def _ada_call(cc, ada_w, ada_b):
    depth, d, n6 = ada_w.shape
    tn = n6 // 4
    cc_lanes = jnp.broadcast_to(cc[:, :, None], (N_COND, d, LANES))
    return pl.pallas_call(
        _ada_body,
        grid=(depth, n6 // tn),
        in_specs=[pl.BlockSpec((N_COND, d, LANES), lambda l, j: (0, 0, 0)),
                  pl.BlockSpec((1, d, tn), lambda l, j: (l, 0, j)),
                  pl.BlockSpec((1, 1, tn), lambda l, j: (l, 0, j))],
        out_specs=pl.BlockSpec((1, SUBLANES, tn), lambda l, j: (l, 0, j)),
        out_shape=jax.ShapeDtypeStruct((depth, SUBLANES, n6), F32),
        compiler_params=_cparams(("arbitrary", "arbitrary"), VMEM_LIMIT),
        name="ada",
    )(cc_lanes, ada_w, ada_b.reshape(depth, 1, n6))


def _cos_sin(n, period):
    k = np.arange(n, dtype=np.float64)
    ang = 2.0 * np.pi * np.outer(k, k) / period
    return np.cos(ang).astype(np.float32), np.sin(ang).astype(np.float32)


def _channel_table():
    c, s = _cos_sin(FNET_GROUP_DIM, FNET_GROUP_DIM)
    return (jnp.concatenate([jnp.asarray(c), jnp.asarray(s)], axis=0)
            * (FNET_GROUP_DIM ** -0.5)).astype(BF16)


def _sequence_tables(n):
    c, s = _cos_sin(n, n)
    eye = np.eye(SUBLANES, dtype=np.float32)
    rows = n * SUBLANES
    f = np.stack([c, -s]) * np.float32(1.0 / n)
    ka = (f[:, None, :, :, None] * eye[None, :, None, None, :]).reshape(2 * rows, rows)
    ck = (c[:, None, :, None] * eye[None, :, None, :]).reshape(rows, rows)
    sk = (s[:, None, :, None] * eye[None, :, None, :]).reshape(rows, rows)
    cs = np.concatenate([ck, sk], axis=1)
    tc, ts = _cos_sin(n, n * n)
    oct_ = n // SUBLANES
    tc = np.ascontiguousarray(np.broadcast_to(tc.reshape(oct_, rows, 1), (oct_, rows, LANES)))
    ts = np.ascontiguousarray(np.broadcast_to(ts.reshape(oct_, rows, 1), (oct_, rows, LANES)))
    return jnp.asarray(ka).astype(BF16), jnp.asarray(cs).astype(BF16), jnp.asarray(tc), jnp.asarray(ts)


def _fft_a_body(x_ref, mods_ref, ka_ref, tc_ref, ts_ref, zr_ref, zi_ref):
    rows = ka_ref.shape[1]
    x = x_ref[...].reshape(rows, D_MODEL)
    h = _mod_norm(x, mods_ref[...]).astype(BF16)
    ka = ka_ref[...]
    tc, ts = tc_ref[0], ts_ref[0]
    gw = FNET_GROUP_DIM
    for g in range(D_MODEL // gw):
        z = _dot(ka, h[:, g * gw:(g + 1) * gw])
        zr, zi = z[:rows], z[rows:]
        for t in range(gw // LANES):
            a = zr[:, t * LANES:(t + 1) * LANES]
            b = zi[:, t * LANES:(t + 1) * LANES]
            sl = slice(g * gw + t * LANES, g * gw + (t + 1) * LANES)
            zr_ref[:, :, sl] = (a * tc + b * ts).reshape(SUBLANES, rows // SUBLANES, LANES)
            zi_ref[:, :, sl] = (b * tc - a * ts).reshape(SUBLANES, rows // SUBLANES, LANES)


def _fft_b_body(zr_ref, zi_ref, x_ref, mods_ref, cs_ref, ch_ref, w_ref, o_ref):
    rows = cs_ref.shape[0]
    cs = cs_ref[...]
    gw = FNET_GROUP_DIM
    ys = []
    for g in range(D_MODEL // gw):
        zr = zr_ref[:, :, g * gw:(g + 1) * gw].reshape(rows, gw).astype(BF16)
        zi = zi_ref[:, :, g * gw:(g + 1) * gw].reshape(rows, gw).astype(BF16)
        xr = _dot(cs, jnp.concatenate([zr, zi], axis=0))
        xi = _dot(cs, jnp.concatenate([zi, -zr], axis=0))
        ys.append(_dot(jnp.concatenate([xr, xi], axis=1).astype(BF16), ch_ref[...]).astype(BF16))
    y = _dot(jnp.concatenate(ys, axis=1), w_ref[...])
    x = x_ref[...].reshape(rows, D_MODEL)
    o_ref[...] = (x + mods_ref[3:4, :] * y).reshape(o_ref.shape)


def _fourier_layer(x, mods, w_bf16, seq_tables, ch):
    L = x.shape[0]
    n = math.isqrt(L)
    rows = n * SUBLANES
    oct_ = n // SUBLANES
    ka, cs, tc, ts = seq_tables
    x3 = x.reshape(n, n, D_MODEL)
    gw = FNET_GROUP_DIM
    zshape = jax.ShapeDtypeStruct((n, n, D_MODEL), F32)
    once = dict(pipeline_mode=pl.Buffered(1))
    zr, zi = pl.pallas_call(
        _fft_a_body,
        grid=(oct_,),
        in_specs=[pl.BlockSpec((n, SUBLANES, D_MODEL), lambda o: (0, o, 0)),
                  pl.BlockSpec((SUBLANES, D_MODEL), lambda o: (0, 0)),
                  pl.BlockSpec((2 * rows, rows), lambda o: (0, 0), **once),
                  pl.BlockSpec((1, rows, LANES), lambda o: (o, 0, 0)),
                  pl.BlockSpec((1, rows, LANES), lambda o: (o, 0, 0))],
        out_specs=[pl.BlockSpec((SUBLANES, n, D_MODEL), lambda o: (o, 0, 0)),
                   pl.BlockSpec((SUBLANES, n, D_MODEL), lambda o: (o, 0, 0))],
        out_shape=[zshape, zshape],
        compiler_params=_cparams(("arbitrary",), VMEM_LIMIT),
        name="fft_a",
    )(x3, mods, ka, tc, ts)
    out = pl.pallas_call(
        _fft_b_body,
        grid=(oct_,),
        in_specs=[pl.BlockSpec((n, SUBLANES, D_MODEL), lambda p: (0, p, 0)),
                  pl.BlockSpec((n, SUBLANES, D_MODEL), lambda p: (0, p, 0)),
                  pl.BlockSpec((n, SUBLANES, D_MODEL), lambda p: (0, p, 0)),
                  pl.BlockSpec((SUBLANES, D_MODEL), lambda p: (0, 0)),
                  pl.BlockSpec((rows, 2 * rows), lambda p: (0, 0), **once),
                  pl.BlockSpec((2 * gw, gw), lambda p: (0, 0), **once),
                  pl.BlockSpec((D_MODEL, D_MODEL), lambda p: (0, 0), **once)],
        out_specs=pl.BlockSpec((n, SUBLANES, D_MODEL), lambda p: (0, p, 0)),
        out_shape=jax.ShapeDtypeStruct((n, n, D_MODEL), F32),
        compiler_params=_cparams(("arbitrary",), VMEM_LIMIT),
        name="fft_b",
    )(zr, zi, x3, mods, cs, ch, w_bf16)
    return out.reshape(L, D_MODEL)


def _ctx_fourier_body(x_ref, mods_ref, f_ref, ch_ref, w_ref, o_ref):
    x = x_ref[...]
    n = x.shape[0]
    h = _mod_norm(x, mods_ref[...]).astype(BF16)
    g = _dot(f_ref[...], h)
    gr, gi = g[:n], g[n:]
    gw = FNET_GROUP_DIM
    ys = []
    for k in range(D_MODEL // gw):
        sl = slice(k * gw, (k + 1) * gw)
        ys.append(_dot(jnp.concatenate([gr[:, sl], gi[:, sl]], axis=1).astype(BF16), ch_ref[...]))
    y = jnp.concatenate(ys, axis=1).astype(BF16)
    o_ref[...] = x + mods_ref[3:4, :] * _dot(y, w_ref[...])


def _ctx_fourier_layer(ctx, mods, w_bf16, ch):
    n = ctx.shape[0]
    c, s = _cos_sin(n, n)
    f = (jnp.concatenate([jnp.asarray(c), -jnp.asarray(s)], axis=0) * (n ** -0.5)).astype(BF16)
    return pl.pallas_call(
        _ctx_fourier_body,
        out_shape=jax.ShapeDtypeStruct(ctx.shape, F32),
        compiler_params=pltpu.CompilerParams(vmem_limit_bytes=VMEM_LIMIT),
        name="ctx_fourier",
    )(ctx, mods, f, ch, w_bf16)


def _load_swiglu_weights(e, wi_hbm, wo_hbm, wi_res, wo_res, stage_i, stage_o, sems):
    n_i = wi_res.shape[1] // W_CHUNK
    total = n_i + wo_res.shape[0] // W_CHUNK

    def copy(c):
        slot = c % 2
        if c < n_i:
            src = wi_hbm.at[e, :, pl.ds(c * W_CHUNK, W_CHUNK)]
            return pltpu.make_async_copy(src, stage_i.at[slot], sems.at[slot])
        src = wo_hbm.at[e, pl.ds((c - n_i) * W_CHUNK, W_CHUNK), :]
        return pltpu.make_async_copy(src, stage_o.at[slot], sems.at[slot])

    copy(0).start()
    for c in range(total):
        if c + 1 < total:
            copy(c + 1).start()
        copy(c).wait()
        if c < n_i:
            wi_res[:, c * W_CHUNK:(c + 1) * W_CHUNK] = stage_i[c % 2].astype(BF16)
        else:
            k = c - n_i
            wo_res[k * W_CHUNK:(k + 1) * W_CHUNK, :] = stage_o[c % 2].astype(BF16)


def _swiglu_tile(h, wi_res, wo_res, act):
    for k in range(D_FF // FF_CHUNK):
        gate = _dot(h, wi_res[:, k * FF_CHUNK:(k + 1) * FF_CHUNK])
        up = _dot(h, wi_res[:, D_FF + k * FF_CHUNK:D_FF + (k + 1) * FF_CHUNK])
        act[:, k * FF_CHUNK:(k + 1) * FF_CHUNK] = (gate / (1.0 + jnp.exp(-gate)) * up).astype(BF16)
    return _dot(act[...], wo_res[...])


def _swiglu_scratch(tm):
    return [pltpu.VMEM((D_MODEL, 2 * D_FF), BF16), pltpu.VMEM((D_FF, D_MODEL), BF16),
            pltpu.VMEM((2, D_MODEL, W_CHUNK), F32), pltpu.VMEM((2, W_CHUNK, D_MODEL), F32),
            pltpu.VMEM((tm, D_FF), BF16), pltpu.SemaphoreType.DMA((2,))]


def _ffn_body(layer, x_ref, mods_ref, wi_hbm, wo_hbm, o_ref, wi_res, wo_res, stage_i, stage_o,
              act, sems):
    @pl.when(pl.program_id(0) == 0)
    def _():
        _load_swiglu_weights(layer, wi_hbm, wo_hbm, wi_res, wo_res, stage_i, stage_o, sems)

    x = x_ref[...]
    h = _mod_norm(x, mods_ref[...]).astype(BF16)
    o_ref[...] = x + mods_ref[3:4, :] * _swiglu_tile(h, wi_res, wo_res, act)


def _ffn_layer(x, mods, wi_all, wo_all, layer):
    n = x.shape[0]
    tm = min(ROW_TILE, n)
    return pl.pallas_call(
        functools.partial(_ffn_body, layer),
        grid=(n // tm,),
        in_specs=[pl.BlockSpec((tm, D_MODEL), lambda i: (i, 0)),
                  pl.BlockSpec((SUBLANES, D_MODEL), lambda i: (0, 0)),
                  pl.BlockSpec(memory_space=pl.ANY),
                  pl.BlockSpec(memory_space=pl.ANY)],
        out_specs=pl.BlockSpec((tm, D_MODEL), lambda i: (i, 0)),
        out_shape=jax.ShapeDtypeStruct(x.shape, F32),
        scratch_shapes=_swiglu_scratch(tm),
        compiler_params=_cparams(("arbitrary",), VMEM_LIMIT),
        name="ffn",
    )(x, mods, wi_all, wo_all)


def _qkv_body(x_ref, mods_ref, w_ref, bd_ref, gains_ref, cos_ref, sin_ref, q_ref, k_ref, v_ref):
    h = _mod_norm(x_ref[...], mods_ref[...]).astype(BF16)
    qkv = _dot(h, w_ref[...])
    bd = bd_ref[...]
    cos, sin = cos_ref[...], sin_ref[...]
    lane = lax.broadcasted_iota(jnp.int32, (1, LANES), 1)
    first_half = (lane % (HEAD_DIM // 2)) < (HEAD_DIM // 4)
    low_head = lane < HEAD_DIM

    def norm_rope(a, gain):
        sq = a * a
        hi = sq.astype(BF16)
        lo = (sq - hi.astype(F32)).astype(BF16)
        ms = _dot(hi, bd) + _dot(lo, bd)
        an = a * lax.rsqrt(ms + NORM_EPS) * gain
        partner = jnp.where(first_half,
                            pltpu.roll(an, LANES - HEAD_DIM // 4, 1),
                            pltpu.roll(an, HEAD_DIM // 4, 1))
        return an * cos + partner * sin

    def dup_heads(a):
        r = pltpu.roll(a, HEAD_DIM, 1)
        return jnp.where(low_head, a, r), jnp.where(low_head, r, a)

    nq = Q_COLS // LANES
    for t in range(nq):
        a = norm_rope(qkv[:, t * LANES:(t + 1) * LANES], gains_ref[0:1, :])
        q_ref[:, t * LANES:(t + 1) * LANES] = a.astype(BF16)
    for t in range(KV_COLS // LANES):
        kt = norm_rope(qkv[:, Q_COLS + t * LANES:Q_COLS + (t + 1) * LANES], gains_ref[1:2, :])
        k0, k1 = dup_heads(kt)
        k_ref[2 * t] = k0.astype(BF16)
        k_ref[2 * t + 1] = k1.astype(BF16)
        v0, v1 = dup_heads(qkv[:, Q_COLS + KV_COLS + t * LANES:Q_COLS + KV_COLS + (t + 1) * LANES])
        for hk, vv in ((2 * t, v0), (2 * t + 1, v1)):
            v_ref[2 * hk] = jnp.where(low_head, vv, 1.0).astype(BF16)
            v_ref[2 * hk + 1] = jnp.where(low_head, 1.0, vv).astype(BF16)


def _qkv_call(x, mods, wqkv_bf16, bd, gains, cos_t, sin_t):
    n = x.shape[0]
    tm = min(ROW_TILE, n)
    ncol = Q_COLS + 2 * KV_COLS
    return pl.pallas_call(
        _qkv_body,
        grid=(n // tm,),
        in_specs=[pl.BlockSpec((tm, D_MODEL), lambda i: (i, 0)),
                  pl.BlockSpec((SUBLANES, D_MODEL), lambda i: (0, 0)),
                  pl.BlockSpec((D_MODEL, ncol), lambda i: (0, 0)),
                  pl.BlockSpec((LANES, LANES), lambda i: (0, 0)),
                  pl.BlockSpec((SUBLANES, LANES), lambda i: (0, 0)),
                  pl.BlockSpec((tm, LANES), lambda i: (i, 0)),
                  pl.BlockSpec((tm, LANES), lambda i: (i, 0))],
        out_specs=[pl.BlockSpec((tm, Q_COLS), lambda i: (i, 0)),
                   pl.BlockSpec((N_KV_HEADS, tm, LANES), lambda i: (0, i, 0)),
                   pl.BlockSpec((2 * N_KV_HEADS, tm, LANES), lambda i: (0, i, 0))],
        out_shape=[jax.ShapeDtypeStruct((n, Q_COLS), BF16),
                   jax.ShapeDtypeStruct((N_KV_HEADS, n, LANES), BF16),
                   jax.ShapeDtypeStruct((2 * N_KV_HEADS, n, LANES), BF16)],
        compiler_params=_cparams(("arbitrary",), VMEM_LIMIT),
        name="qkv",
    )(x, mods, wqkv_bf16, bd, gains, cos_t, sin_t)


def _attn_body(sink_ref, q_ref, kp_ref, kc_ref, kn_ref, vp_ref, vc_ref, vn_ref,
               kx_ref, vx_ref, o_ref):
    b = pl.program_id(0)
    nb = pl.num_programs(0)
    blk = ATTN_BLOCK
    lane = lax.broadcasted_iota(jnp.int32, (1, LANES), 1)
    low_head = lane < HEAD_DIM
    qi = lax.broadcasted_iota(jnp.int32, (blk, 3 * blk), 0)
    kj = lax.broadcasted_iota(jnp.int32, (blk, 3 * blk), 1)
    valid = (kj >= qi) & (kj <= qi + 2 * WINDOW)
    valid = valid & ((kj >= blk) | (b > 0)) & ((kj < 2 * blk) | (b < nb - 1))
    per_kv = N_HEADS // N_KV_HEADS
    for g in range(N_KV_HEADS):
        kb = jnp.concatenate([kp_ref[g], kc_ref[g], kn_ref[g], kx_ref[g]], axis=0)
        vbs = [jnp.concatenate([vp_ref[2 * g + a], vc_ref[2 * g + a], vn_ref[2 * g + a],
                                vx_ref[2 * g + a]], axis=0) for a in range(2)]
        parts = []
        for p in range(per_kv // 2):
            t = g * (per_kv // 2) + p
            qt = q_ref[:, t * LANES:(t + 1) * LANES]
            zero = jnp.zeros_like(qt)
            parts += [jnp.where(low_head, qt, zero), jnp.where(low_head, zero, qt)]
        lhs = jnp.concatenate(parts, axis=0)
        s = lax.dot_general(lhs, kb, (((1,), (1,)), ((), ())), preferred_element_type=F32)
        outs = []
        for hh in range(per_kv):
            sink = sink_ref[g * per_kv + hh] * LOG2_E
            sh = s[hh * blk:(hh + 1) * blk]
            s_loc = jnp.where(valid, sh[:, :3 * blk], NEG_INF)
            s_ctx = sh[:, 3 * blk:]
            m = jnp.maximum(jnp.maximum(jnp.max(s_loc, axis=-1, keepdims=True),
                                        jnp.max(s_ctx, axis=-1, keepdims=True)), sink)
            pr = jnp.concatenate([jnp.exp2(s_loc - m), jnp.exp2(s_ctx - m)], axis=1).astype(BF16)
            pv = _dot(pr, vbs[hh % 2])
            den = pltpu.roll(pv, HEAD_DIM, 1) + jnp.exp2(sink - m)
            outs.append(pv / den)
        for p in range(per_kv // 2):
            t = g * (per_kv // 2) + p
            o_ref[:, t * LANES:(t + 1) * LANES] = jnp.where(
                low_head, outs[2 * p], outs[2 * p + 1]).astype(BF16)


def _attn_call(sink, q, kd, vd, kx, vx):
    n = q.shape[0]
    blk = ATTN_BLOCK
    nb = n // blk
    nctx = kx.shape[1]
    k_spec = lambda f: pl.BlockSpec((N_KV_HEADS, blk, LANES), f)
    v_spec = lambda f: pl.BlockSpec((2 * N_KV_HEADS, blk, LANES), f)
    prev = lambda b: (0, jnp.maximum(b - 1, 0), 0)
    cur = lambda b: (0, b, 0)
    nxt = lambda b: (0, jnp.minimum(b + 1, nb - 1), 0)
    return pl.pallas_call(
        _attn_body,
        grid=(nb,),
        in_specs=[pl.BlockSpec(memory_space=pltpu.SMEM),
                  pl.BlockSpec((blk, Q_COLS), lambda b: (b, 0)),
                  k_spec(prev), k_spec(cur), k_spec(nxt),
                  v_spec(prev), v_spec(cur), v_spec(nxt),
                  pl.BlockSpec((N_KV_HEADS, nctx, LANES), lambda b: (0, 0, 0)),
                  pl.BlockSpec((2 * N_KV_HEADS, nctx, LANES), lambda b: (0, 0, 0))],
        out_specs=pl.BlockSpec((blk, Q_COLS), lambda b: (b, 0)),
        out_shape=jax.ShapeDtypeStruct((n, Q_COLS), BF16),
        compiler_params=_cparams(("arbitrary",), VMEM_LIMIT),
        name="attn",
    )(sink, q, kd, kd, kd, vd, vd, vd, kx, vx)


def _proj_body(a_ref, x_ref, mods_ref, w_ref, o_ref):
    o_ref[...] = x_ref[...] + mods_ref[3:4, :] * _dot(a_ref[...], w_ref[...])


def _proj_call(a, x, mods, w_bf16):
    n = x.shape[0]
    tm = min(ROW_TILE, n)
    return pl.pallas_call(
        _proj_body,
        grid=(n // tm,),
        in_specs=[pl.BlockSpec((tm, a.shape[1]), lambda i: (i, 0)),
                  pl.BlockSpec((tm, D_MODEL), lambda i: (i, 0)),
                  pl.BlockSpec((SUBLANES, D_MODEL), lambda i: (0, 0)),
                  pl.BlockSpec(w_bf16.shape, lambda i: (0, 0))],
        out_specs=pl.BlockSpec((tm, D_MODEL), lambda i: (i, 0)),
        out_shape=jax.ShapeDtypeStruct(x.shape, F32),
        compiler_params=_cparams(("arbitrary",), VMEM_LIMIT),
        name="proj",
    )(a, x, mods, w_bf16)


def _rope_lane_tables(length):
    rows = length // GRID_W
    row_pos = jnp.repeat(jnp.arange(rows, dtype=F32), GRID_W)
    col_pos = jnp.tile(jnp.arange(GRID_W, dtype=F32), rows)
    axis_dim = HEAD_DIM // 2
    inv_freq = ROPE_BASE ** (-jnp.arange(0, axis_dim, 2, dtype=F32) / axis_dim)
    ang_r = row_pos[:, None] * inv_freq[None, :]
    ang_c = col_pos[:, None] * inv_freq[None, :]
    cos_h = jnp.concatenate([jnp.cos(ang_r), jnp.cos(ang_r), jnp.cos(ang_c), jnp.cos(ang_c)], axis=1)
    sin_h = jnp.concatenate([-jnp.sin(ang_r), jnp.sin(ang_r), -jnp.sin(ang_c), jnp.sin(ang_c)], axis=1)
    reps = LANES // HEAD_DIM
    return jnp.tile(cos_h, (1, reps)), jnp.tile(sin_h, (1, reps))


def _attention_layer(x, ctx, mods_x, mods_c, wqkv, wo, q_gain, k_gain, sink):
    L = x.shape[0]
    nctx = ctx.shape[0]
    wqkv_b = wqkv.astype(BF16)
    head = np.arange(LANES) // HEAD_DIM
    bd = jnp.asarray((head[:, None] == head[None, :]).astype(np.float32) / HEAD_DIM).astype(BF16)
    reps = LANES // HEAD_DIM
    gains = jnp.zeros((SUBLANES, LANES), F32)
    gains = gains.at[0].set(jnp.tile(q_gain, reps) * (HEAD_DIM ** -0.5 * LOG2_E))
    gains = gains.at[1].set(jnp.tile(k_gain, reps))
    cos_t, sin_t = _rope_lane_tables(L)
    q, kd, vd = _qkv_call(x, mods_x, wqkv_b, bd, gains, cos_t, sin_t)
    ones = jnp.ones((nctx, LANES), F32)
    _, kx, vx = _qkv_call(ctx, mods_c, wqkv_b, bd, gains, ones, jnp.zeros_like(ones))
    o = _attn_call(sink, q, kd, vd, kx, vx)
    return _proj_call(o, x, mods_x, wo.astype(BF16))


def _pool_body(xp_ref, xc_ref, xn_ref, mods_ref, w_ref, o_ref, h_scr):
    i = pl.program_id(0)
    tm = xc_ref.shape[0]
    total = tm * pl.num_programs(0)
    x = xc_ref[...]
    mods = mods_ref[...]
    h_scr[0:POOL_HALO, :] = jnp.where(i > 0, _mod_norm(xp_ref[...], mods), 0.0)
    h_scr[POOL_HALO:POOL_HALO + tm, :] = _mod_norm(x, mods)
    h_scr[POOL_HALO + tm:, :] = jnp.where(i < pl.num_programs(0) - 1, _mod_norm(xn_ref[...], mods), 0.0)
    t = i * tm + lax.broadcasted_iota(jnp.int32, (tm, 1), 0)
    gd = POOL_GROUP_DIM
    for g, win in enumerate(POOL_WINDOWS):
        sl = slice(g * gd, (g + 1) * gd)
        half = win // 2
        tot = h_scr[POOL_HALO - half:POOL_HALO - half + tm, sl]
        for s in range(-half + 1, half):
            tot = tot + h_scr[POOL_HALO + s:POOL_HALO + s + tm, sl]
        lo = jnp.maximum(t - half, 0)
        hi = jnp.minimum(t + half - 1, total - 1)
        cnt = (hi - lo + 1).astype(F32)
        pooled = (tot / cnt - h_scr[POOL_HALO:POOL_HALO + tm, sl]).astype(BF16)
        y = _dot(pooled, w_ref[g]) * mods[4:5, sl]
        o_ref[:, sl] = x[:, sl] + mods[3:4, sl] * y


def _pool_layer(x, mods, w_bf16):
    n = x.shape[0]
    tm = min(ROW_TILE, n)
    r = tm // POOL_HALO
    last = n // POOL_HALO - 1
    return pl.pallas_call(
        _pool_body,
        grid=(n // tm,),
        in_specs=[pl.BlockSpec((POOL_HALO, D_MODEL), lambda i: (jnp.maximum(i * r - 1, 0), 0)),
                  pl.BlockSpec((tm, D_MODEL), lambda i: (i, 0)),
                  pl.BlockSpec((POOL_HALO, D_MODEL), lambda i: (jnp.minimum((i + 1) * r, last), 0)),
                  pl.BlockSpec((SUBLANES, D_MODEL), lambda i: (0, 0)),
                  pl.BlockSpec(w_bf16.shape, lambda i: (0, 0, 0))],
        out_specs=pl.BlockSpec((tm, D_MODEL), lambda i: (i, 0)),
        out_shape=jax.ShapeDtypeStruct(x.shape, F32),
        scratch_shapes=[pltpu.VMEM((tm + 2 * POOL_HALO, D_MODEL), F32)],
        compiler_params=_cparams(("arbitrary",), VMEM_LIMIT),
        name="pool",
    )(x, x, x, mods, w_bf16)


def _router_body(x_ref, mods_ref, r_ref, h_ref, route_ref, cnt_ref):
    h = _mod_norm(x_ref[...], mods_ref[...])
    h_hi = h.astype(BF16)
    h_ref[...] = h_hi
    h_lo = (h - h_hi.astype(F32)).astype(BF16)
    r = r_ref[...]
    r_hi = r.astype(BF16)
    r_lo = (r - r_hi.astype(F32)).astype(BF16)
    logits = _dot(h_hi, r_hi) + (_dot(h_lo, r_hi) + _dot(h_hi, r_lo))
    lane = lax.broadcasted_iota(jnp.int32, logits.shape, 1)
    logits = jnp.where(lane < N_EXPERTS, logits, -jnp.inf)
    m1 = jnp.max(logits, axis=-1, keepdims=True)
    i1 = jnp.min(jnp.where(logits == m1, lane, LANES), axis=-1, keepdims=True)
    rest = jnp.where(lane == i1, -jnp.inf, logits)
    m2 = jnp.max(rest, axis=-1, keepdims=True)
    i2 = jnp.min(jnp.where(rest == m2, lane, LANES), axis=-1, keepdims=True)
    e = jnp.exp(m2 - m1)
    w1 = 1.0 / (1.0 + e)
    w2 = e / (1.0 + e)
    tm = h.shape[0]
    oh = jnp.where((lane == i1) | (lane == i2 + N_EXPERTS), 1.0, 0.0)
    before = (lax.broadcasted_iota(jnp.int32, (tm, tm), 1)
              < lax.broadcasted_iota(jnp.int32, (tm, tm), 0))
    prior = _dot(jnp.where(before, 1.0, 0.0).astype(BF16), oh.astype(BF16))
    cnt = jnp.sum(oh, axis=0, keepdims=True)
    r1 = jnp.sum(jnp.where(lane == i1, prior, 0.0), axis=-1, keepdims=True)
    r2 = (jnp.sum(jnp.where(lane == i2 + N_EXPERTS, prior, 0.0), axis=-1, keepdims=True)
          + jnp.sum(jnp.where(lane == i2, cnt, 0.0), axis=-1, keepdims=True))
    lane8 = lane[0:SUBLANES, :]
    cnt8 = jnp.broadcast_to(cnt, (SUBLANES, LANES))
    both = cnt8 + pltpu.roll(cnt8, LANES - N_EXPERTS, 1)
    seg_len = jnp.where(lane8 < N_EXPERTS, jnp.ceil(both * (1.0 / SEG_ALIGN)) * SEG_ALIGN, 0.0)
    incl = seg_len
    shift = 1
    while shift < N_EXPERTS:
        incl = incl + jnp.where(lane8 >= shift, pltpu.roll(incl, shift, 1), 0.0)
        shift *= 2
    seg = incl - seg_len
    p1 = r1 + jnp.sum(jnp.where(lane == i1, seg[0:1, :], 0.0), axis=-1, keepdims=True)
    p2 = r2 + jnp.sum(jnp.where(lane == i2, seg[0:1, :], 0.0), axis=-1, keepdims=True)
    col = lax.broadcasted_iota(jnp.int32, route_ref.shape, 1)
    vals = (i1.astype(F32), i2.astype(F32), w1, w2, p1, p2)
    out = jnp.zeros(route_ref.shape, F32)
    for k, v in enumerate(vals):
        out = jnp.where(col == k, v, out)
    route_ref[...] = out
    row = lax.broadcasted_iota(jnp.int32, cnt_ref.shape[1:], 0)
    cnt_ref[0] = jnp.where(row == 0, seg_len, jnp.where(row == 1, seg, 0.0))


def _router_call(x, mods, router_pad):
    n = x.shape[0]
    tm = min(ROUTE_TILE, n)
    return pl.pallas_call(
        _router_body,
        grid=(n // tm,),
        in_specs=[pl.BlockSpec((tm, D_MODEL), lambda i: (i, 0)),
                  pl.BlockSpec((SUBLANES, D_MODEL), lambda i: (0, 0)),
                  pl.BlockSpec((D_MODEL, LANES), lambda i: (0, 0))],
        out_specs=[pl.BlockSpec((tm, D_MODEL), lambda i: (i, 0)),
                   pl.BlockSpec((tm, SUBLANES), lambda i: (i, 0)),
                   pl.BlockSpec((1, SUBLANES, LANES), lambda i: (i, 0, 0))],
        out_shape=[jax.ShapeDtypeStruct((n, D_MODEL), BF16),
                   jax.ShapeDtypeStruct((n, SUBLANES), F32),
                   jax.ShapeDtypeStruct((n // tm, SUBLANES, LANES), F32)],
        compiler_params=_cparams(("arbitrary",), VMEM_LIMIT),
        name="router",
    )(x, mods, router_pad)


def _segment_copies(i, seg_ref, dst_ref, len_ref, tile_buf, sorted_hbm, sem, to_hbm):
    for e in range(N_EXPERTS):
        length = len_ref[i * N_EXPERTS + e]
        seg = seg_ref[i * N_EXPERTS + e]
        dst = dst_ref[i * N_EXPERTS + e]
        size = SEG_ALIGN
        while size <= tile_buf.shape[0] // 2:
            done = length & ~(2 * size - 1)
            a = tile_buf.at[pl.ds(pl.multiple_of(seg + done, SEG_ALIGN), size)]
            b = sorted_hbm.at[pl.ds(pl.multiple_of(dst + done, SEG_ALIGN), size)]
            copy = pltpu.make_async_copy(a, b, sem) if to_hbm else pltpu.make_async_copy(b, a, sem)
            yield (length & size) != 0, copy
            size *= 2


def _scatter_body(seg_ref, dst_ref, len_ref, dead_ref, h_ref, prow_ref, xs_ref, sbuf, zbuf, sems):
    i = pl.program_id(0)
    n_route = pl.num_programs(0)
    last = n_route - 1
    slot = i % 2

    @pl.when(i == last)
    def _():
        zbuf[...] = jnp.zeros(zbuf.shape, BF16)
        fill_sem = sems.at[2]
        tails = list(_segment_copies(n_route, seg_ref, dst_ref, len_ref, zbuf, xs_ref, fill_sem, True))
        for pred, copy in tails:
            pl.when(pred)(copy.start)

        def dead_copy(k):
            row = pl.multiple_of(dead_ref[0] + k * MOE_TILE, MOE_TILE)
            return pltpu.make_async_copy(zbuf, xs_ref.at[pl.ds(row, MOE_TILE)], fill_sem)

        def start(k, c):
            dead_copy(k).start()
            return c

        def wait(k, c):
            dead_copy(k).wait()
            return c

        lax.fori_loop(0, dead_ref[1], start, 0)
        for pred, copy in tails:
            pl.when(pred)(copy.wait)
        lax.fori_loop(0, dead_ref[1], wait, 0)

    cap, tm = sbuf.shape[1], h_ref.shape[0]
    r = lax.broadcasted_iota(jnp.int32, (cap, tm), 0)
    hit = (r == prow_ref[0:1, :]) | (r == prow_ref[1:2, :])
    sel = jnp.where(hit, 1.0, 0.0).astype(BF16)
    sbuf[slot] = _dot(sel, h_ref[...]).astype(BF16)

    def copies(tile, s):
        return _segment_copies(tile, seg_ref, dst_ref, len_ref, sbuf.at[s], xs_ref, sems.at[s], True)

    for pred, copy in copies(i, slot):
        pl.when(pred)(copy.start)
    for pred, copy in copies(jnp.maximum(i - 1, 0), 1 - slot):
        pl.when(pred & (i > 0))(copy.wait)
    for pred, copy in copies(i, slot):
        pl.when(pred & (i == last))(copy.wait)


def _route_cap(tm):
    return 2 * tm + N_EXPERTS * SEG_ALIGN


def _scatter_call(seg, dst, seg_len, dead, h, prow, n_rows):
    n = h.shape[0]
    tm = min(ROUTE_TILE, n)
    return pl.pallas_call(
        _scatter_body,
        grid_spec=pltpu.PrefetchScalarGridSpec(
            num_scalar_prefetch=4,
            grid=(n // tm,),
            in_specs=[pl.BlockSpec((tm, D_MODEL), lambda i, *_: (i, 0)),
                      pl.BlockSpec((SUBLANES, tm), lambda i, *_: (i, 0))],
            out_specs=pl.BlockSpec(memory_space=pl.ANY),
            scratch_shapes=[pltpu.VMEM((2, _route_cap(tm), D_MODEL), BF16),
                            pltpu.VMEM((MOE_TILE, D_MODEL), BF16),
                            pltpu.SemaphoreType.DMA((3,))]),
        out_shape=jax.ShapeDtypeStruct((n_rows, D_MODEL), BF16),
        compiler_params=_cparams(("arbitrary",), VMEM_LIMIT),
        name="moe_scatter",
    )(seg, dst, seg_len, dead, h, prow)


def _moe_body(te_ref, tv_ref, tf_ref, xs_ref, wi_hbm, wo_hbm, ys_ref, wi_res, wo_res, stage_i,
              stage_o, act, sems):
    i = pl.program_id(0)

    @pl.when(tf_ref[i] > 0)
    def _():
        _load_swiglu_weights(te_ref[i], wi_hbm, wo_hbm, wi_res, wo_res, stage_i, stage_o, sems)

    @pl.when(tv_ref[i] > 0)
    def _():
        ys_ref[...] = _swiglu_tile(xs_ref[...], wi_res, wo_res, act).astype(BF16)

    @pl.when(tv_ref[i] == 0)
    def _():
        ys_ref[...] = jnp.zeros(ys_ref.shape, BF16)


def _moe_call(tile_expert, tile_live, tile_first, xs, wi_all, wo_all):
    n_rows = xs.shape[0]
    return pl.pallas_call(
        _moe_body,
        grid_spec=pltpu.PrefetchScalarGridSpec(
            num_scalar_prefetch=3,
            grid=(n_rows // MOE_TILE,),
            in_specs=[pl.BlockSpec((MOE_TILE, D_MODEL), lambda i, *_: (i, 0)),
                      pl.BlockSpec(memory_space=pl.ANY),
                      pl.BlockSpec(memory_space=pl.ANY)],
            out_specs=pl.BlockSpec((MOE_TILE, D_MODEL), lambda i, *_: (i, 0)),
            scratch_shapes=_swiglu_scratch(MOE_TILE)),
        out_shape=jax.ShapeDtypeStruct((n_rows, D_MODEL), BF16),
        compiler_params=_cparams(("arbitrary",), VMEM_LIMIT),
        name="moe_ffn",
    )(tile_expert, tile_live, tile_first, xs, wi_all, wo_all)


def _combine_body(seg_ref, dst_ref, len_ref, x_ref, route_ref, pcol_ref, mods_ref, ys_ref, o_ref,
                  ybuf, sems):
    i = pl.program_id(0)
    last = pl.num_programs(0) - 1
    slot = i % 2
    tm, cap = x_ref.shape[0], ybuf.shape[1]

    def copies(tile, s):
        return _segment_copies(tile, seg_ref, dst_ref, len_ref, ybuf.at[s], ys_ref, sems.at[s], False)

    @pl.when(i == 0)
    def _():
        ybuf[...] = jnp.zeros(ybuf.shape, BF16)

    for pred, copy in copies(0, 0):
        pl.when(pred & (i == 0))(copy.start)
    for pred, copy in copies(jnp.minimum(i + 1, last), 1 - slot):
        pl.when(pred & (i < last))(copy.start)
    for pred, copy in copies(i, slot):
        pl.when(pred)(copy.wait)
    c = lax.broadcasted_iota(jnp.int32, (tm, cap), 1)
    pcol = pcol_ref[...]
    route = route_ref[...]
    sel = jnp.where(c == pcol[:, 0:1], route[:, 2:3], jnp.where(c == pcol[:, 1:2], route[:, 3:4], 0.0))
    o_ref[...] = x_ref[...] + mods_ref[3:4, :] * _dot(sel.astype(BF16), ybuf[slot])


def _combine_call(seg, dst, seg_len, x, route, pcol, mods, ys):
    n = x.shape[0]
    tm = min(ROUTE_TILE, n)
    return pl.pallas_call(
        _combine_body,
        grid_spec=pltpu.PrefetchScalarGridSpec(
            num_scalar_prefetch=3,
            grid=(n // tm,),
            in_specs=[pl.BlockSpec((tm, D_MODEL), lambda i, *_: (i, 0)),
                      pl.BlockSpec((tm, SUBLANES), lambda i, *_: (i, 0)),
                      pl.BlockSpec((tm, SUBLANES), lambda i, *_: (i, 0)),
                      pl.BlockSpec((SUBLANES, D_MODEL), lambda i, *_: (0, 0)),
                      pl.BlockSpec(memory_space=pl.ANY)],
            out_specs=pl.BlockSpec((tm, D_MODEL), lambda i, *_: (i, 0)),
            scratch_shapes=[pltpu.VMEM((2, _route_cap(tm), D_MODEL), BF16),
                            pltpu.SemaphoreType.DMA((2,))]),
        out_shape=jax.ShapeDtypeStruct(x.shape, F32),
        compiler_params=_cparams(("arbitrary",), VMEM_LIMIT),
        name="moe_combine",
    )(seg, dst, seg_len, x, route, pcol, mods, ys)


def _routing_tables(route, counts, tm, n_tiles):
    nt = counts.shape[0]
    seg_len = counts[:, 0, :N_EXPERTS].astype(jnp.int32)
    seg = counts[:, 1, :N_EXPERTS].astype(jnp.int32)
    tiles = (jnp.sum(seg_len, axis=0) + MOE_TILE - 1) // MOE_TILE
    tile_end = jnp.cumsum(tiles)
    start = (tile_end - tiles) * MOE_TILE
    dst = start[None, :] + jnp.cumsum(seg_len, axis=0) - seg_len
    t = jnp.arange(n_tiles, dtype=jnp.int32)
    expert = jnp.sum((t[:, None] >= tile_end[None, :]).astype(jnp.int32), axis=1)
    live = (t < tile_end[-1]).astype(jnp.int32)
    last_expert = jnp.sum((tile_end[-1] - 1 >= tile_end).astype(jnp.int32))
    expert = jnp.where(live > 0, expert, last_expert).astype(jnp.int32)
    prev = jnp.concatenate([jnp.full((1,), -1, jnp.int32), expert[:-1]])
    first = (live * (expert != prev)).astype(jnp.int32)
    p12 = route[:, 4:6].astype(jnp.int32)
    pcol = jnp.concatenate([p12, jnp.zeros((p12.shape[0], SUBLANES - 2), jnp.int32)], axis=1)
    prow = jnp.concatenate([p12.reshape(nt, tm, 2).transpose(0, 2, 1),
                            jnp.full((nt, SUBLANES - 2, tm), -1, jnp.int32)], axis=1)
    rows_e = jnp.sum(seg_len, axis=0)
    seg = jnp.concatenate([seg, jnp.zeros((1, N_EXPERTS), jnp.int32)], axis=0)
    dst = jnp.concatenate([dst, (start + rows_e)[None, :]], axis=0)
    seg_len = jnp.concatenate([seg_len, (tiles * MOE_TILE - rows_e)[None, :]], axis=0)
    dead = jnp.stack([tile_end[-1] * MOE_TILE, n_tiles - tile_end[-1]]).astype(jnp.int32)
    return (seg.reshape(-1), dst.reshape(-1).astype(jnp.int32), seg_len.reshape(-1), dead,
            expert, live, first, pcol, prow.reshape(nt * SUBLANES, tm))


def _moe_layer(x, mods, router, wi_all, wo_all, layer):
    n = x.shape[0]
    tm = min(ROUTE_TILE, n)
    router_pad = jnp.zeros((D_MODEL, LANES), F32).at[:, :N_EXPERTS].set(router)
    h, route, counts = _router_call(x, mods, router_pad)
    max_rows = 2 * n + (n // tm) * N_EXPERTS * (SEG_ALIGN - 1)
    n_tiles = -(-max_rows // MOE_TILE) + N_EXPERTS
    seg, dst, seg_len, dead, tile_expert, tile_live, tile_first, pcol, prow = _routing_tables(
        route, counts, tm, n_tiles)
    xs = _scatter_call(seg, dst, seg_len, dead, h, prow, n_tiles * MOE_TILE)
    ys = _moe_call(tile_expert + layer * N_EXPERTS, tile_live, tile_first, xs, wi_all, wo_all)
    return _combine_call(seg, dst, seg_len, x, route, pcol, mods, ys)


def _mods(gain, shift, scale, gate, extra=None):
    rows = [gain, shift, scale, gate, extra if extra is not None else jnp.zeros_like(gain)]
    m = jnp.stack(rows, axis=0)
    return jnp.concatenate([m, jnp.zeros((SUBLANES - m.shape[0], m.shape[1]), F32)], axis=0)


def kernel(x, c, ctx, c_ctx, ada_w, ada_b, norm_mix, norm_ffn, fnet_w, attn_wqkv, attn_q_gain,
           attn_k_gain, attn_sink, attn_wo, pool_w, pool_scale, ffn_wi, ffn_wo, moe_router,
           moe_wi, moe_wo):
    assert x.shape[0] == 1 and x.shape[2] == D_MODEL
    depth = ada_w.shape[0]
    xs = x[0]
    cs = ctx[0]
    ada = _ada_call(jnp.stack([c[0], c_ctx]), ada_w, ada_b)

    attn_layers = [i for i in range(depth) if i % N_MIXERS == 1]
    last_ctx_read = attn_layers[-1] if attn_layers else -1
    n_side = math.isqrt(xs.shape[0])
    assert n_side * n_side == xs.shape[0] and n_side % SUBLANES == 0
    seq_tables = _sequence_tables(n_side)
    ch = _channel_table()
    moe_wi_all = moe_wi.reshape((-1,) + moe_wi.shape[2:])
    moe_wo_all = moe_wo.reshape((-1,) + moe_wo.shape[2:])

    for i in range(depth):
        mixer = i % N_MIXERS
        j = i // N_MIXERS
        f = i // 2
        ctx_full = i < last_ctx_read
        ctx_live = i <= last_ctx_read
        sh1, sc1, g1, sh2, sc2, g2 = [ada[i, 0, k * D_MODEL:(k + 1) * D_MODEL] for k in range(6)]
        csh1, csc1, cg1, csh2, csc2, cg2 = [ada[i, 1, k * D_MODEL:(k + 1) * D_MODEL] for k in range(6)]
        extra = pool_scale[j] if mixer == 2 else None
        m1 = _mods(norm_mix[i], sh1, sc1, g1, extra)
        m2 = _mods(norm_ffn[i], sh2, sc2, g2)
        cm1 = _mods(norm_mix[i], csh1, csc1, cg1, extra)
        cm2 = _mods(norm_ffn[i], csh2, csc2, cg2)

        if mixer == 0:
            wb = fnet_w[j].astype(BF16)
            xs = _fourier_layer(xs, m1, wb, seq_tables, ch)
            if ctx_full:
                cs = _ctx_fourier_layer(cs, cm1, wb, ch)
        elif mixer == 1:
            assert ctx_live and not ctx_full
            xs = _attention_layer(xs, cs, m1, cm1, attn_wqkv[j], attn_wo[j], attn_q_gain[j],
                                  attn_k_gain[j], attn_sink[j])
        else:
            wb = pool_w[j].astype(BF16)
            xs = _pool_layer(xs, m1, wb)
            if ctx_full:
                cs = _pool_layer(cs, cm1, wb)

        if i % 2 == 0:
            xs = _ffn_layer(xs, m2, ffn_wi, ffn_wo, f)
            if ctx_full:
                cs = _ffn_layer(cs, cm2, ffn_wi, ffn_wo, f)
        else:
            xs = _moe_layer(xs, m2, moe_router[f], moe_wi_all, moe_wo_all, f)
            if ctx_full:
                cs = _moe_layer(cs, cm2, moe_router[f], moe_wi_all, moe_wo_all, f)
    return xs[None]
```

```python
import functools
import math

import numpy as np
import jax
import jax.numpy as jnp
from jax import lax
from jax.experimental import pallas as pl
from jax.experimental.pallas import tpu as pltpu

F32 = jnp.float32
BF16 = jnp.bfloat16

D_MODEL = 1024
GRID_W = 64
N_MIXERS = 3
FNET_GROUPS = 4
FNET_GROUP_DIM = D_MODEL // FNET_GROUPS
N_HEADS = 16
N_KV_HEADS = 4
HEAD_DIM = 64
Q_COLS = N_HEADS * HEAD_DIM
KV_COLS = N_KV_HEADS * HEAD_DIM
WINDOW = 128
ATTN_BLOCK = 128
ROPE_BASE = 10000.0
POOL_WINDOWS = (2, 4, 8, 16)
POOL_GROUP_DIM = D_MODEL // len(POOL_WINDOWS)
POOL_HALO = 8
D_FF = 3584
N_EXPERTS = 8
NORM_EPS = 1e-6
NEG_INF = -1e30
LOG2_E = 1.4426950408889634

LANES = 128
SUBLANES = 8
VMEM_LIMIT = 56 * 1024 * 1024

ROW_TILE = 512
FF_CHUNK = 512
MOE_TILE = 256
W_CHUNK = 512
W_SLOTS = 3
ROUTE_TILE = 512
SEG_ALIGN = 16
N_COND = 2


def _cparams(sem, vmem=None):
    return pltpu.CompilerParams(dimension_semantics=sem, vmem_limit_bytes=vmem)


def _mod_norm(x, mods):
    ms = jnp.mean(x * x, axis=-1, keepdims=True)
    y = x * lax.rsqrt(ms + NORM_EPS) * mods[0:1, :]
    return y * (1.0 + mods[2:3, :]) + mods[1:2, :]


def _dot(a, b):
    return jnp.dot(a, b, preferred_element_type=F32)


def _ada_body(cc_ref, w_ref, b_ref, o_ref):
    n_tiles = w_ref.shape[-1] // LANES
    d = w_ref.shape[1]
    acc = [[jnp.zeros((SUBLANES, LANES), F32) for _ in range(N_COND)] for _ in range(n_tiles)]
    for k in range(d // LANES):
        rows = slice(k * LANES, (k + 1) * LANES)
        a = cc_ref[:, rows, :]
        a = a / (1.0 + jnp.exp(-a))
        for t in range(n_tiles):
            wt = w_ref[0, rows, t * LANES:(t + 1) * LANES]
            for r in range(N_COND):
                acc[t][r] = acc[t][r] + jnp.sum(
                    (wt * a[r]).reshape(LANES // SUBLANES, SUBLANES, LANES), axis=0)
    row = lax.broadcasted_iota(jnp.int32, (SUBLANES, LANES), 0)
    for t in range(n_tiles):
        sl = slice(t * LANES, (t + 1) * LANES)
        tile = jnp.zeros((SUBLANES, LANES), F32)
        for r in range(N_COND):
            s = jnp.sum(acc[t][r], axis=0, keepdims=True) + b_ref[0, :, sl]
            tile = jnp.where(row == r, s, tile)
        o_ref[0, :, sl] = tile


def _ada_call(cc, ada_w, ada_b):
    depth, d, n6 = ada_w.shape
    tn = n6 // 4
    cc_lanes = jnp.broadcast_to(cc[:, :, None], (N_COND, d, LANES))
    return pl.pallas_call(
        _ada_body,
        grid=(depth, n6 // tn),
        in_specs=[pl.BlockSpec((N_COND, d, LANES), lambda l, j: (0, 0, 0)),
                  pl.BlockSpec((1, d, tn), lambda l, j: (l, 0, j)),
                  pl.BlockSpec((1, 1, tn), lambda l, j: (l, 0, j))],
        out_specs=pl.BlockSpec((1, SUBLANES, tn), lambda l, j: (l, 0, j)),
        out_shape=jax.ShapeDtypeStruct((depth, SUBLANES, n6), F32),
        compiler_params=_cparams(("arbitrary", "arbitrary"), VMEM_LIMIT),
        name="ada",
    )(cc_lanes, ada_w, ada_b.reshape(depth, 1, n6))


def _cos_sin(n, period):
    k = np.arange(n, dtype=np.float64)
    ang = 2.0 * np.pi * np.outer(k, k) / period
    return np.cos(ang).astype(np.float32), np.sin(ang).astype(np.float32)


def _channel_table():
    c, s = _cos_sin(FNET_GROUP_DIM, FNET_GROUP_DIM)
    return (jnp.concatenate([jnp.asarray(c), jnp.asarray(s)], axis=0)
            * (FNET_GROUP_DIM ** -0.5)).astype(BF16)


def _sequence_tables(n):
    c, s = _cos_sin(n, n)
    eye = np.eye(SUBLANES, dtype=np.float32)
    rows = n * SUBLANES
    f = np.stack([c, -s]) * np.float32(1.0 / n)
    ka = (f[:, None, :, :, None] * eye[None, :, None, None, :]).reshape(2 * rows, rows)
    ck = (c[:, None, :, None] * eye[None, :, None, :]).reshape(rows, rows)
    sk = (s[:, None, :, None] * eye[None, :, None, :]).reshape(rows, rows)
    cs = np.concatenate([ck, sk], axis=1)
    tc, ts = _cos_sin(n, n * n)
    oct_ = n // SUBLANES
    tc = np.ascontiguousarray(np.broadcast_to(tc.reshape(oct_, rows, 1), (oct_, rows, LANES)))
    ts = np.ascontiguousarray(np.broadcast_to(ts.reshape(oct_, rows, 1), (oct_, rows, LANES)))
    return jnp.asarray(ka).astype(BF16), jnp.asarray(cs).astype(BF16), jnp.asarray(tc), jnp.asarray(ts)


def _fft_a_body(x_ref, mods_ref, ka_ref, tc_ref, ts_ref, zr_ref, zi_ref):
    rows = ka_ref.shape[1]
    x = x_ref[...].reshape(rows, D_MODEL)
    h = _mod_norm(x, mods_ref[...]).astype(BF16)
    ka = ka_ref[...]
    tc, ts = tc_ref[0], ts_ref[0]
    gw = FNET_GROUP_DIM
    for g in range(D_MODEL // gw):
        z = _dot(ka, h[:, g * gw:(g + 1) * gw])
        zr, zi = z[:rows], z[rows:]
        for t in range(gw // LANES):
            a = zr[:, t * LANES:(t + 1) * LANES]
            b = zi[:, t * LANES:(t + 1) * LANES]
            sl = slice(g * gw + t * LANES, g * gw + (t + 1) * LANES)
            zr_ref[:, :, sl] = (a * tc + b * ts).reshape(SUBLANES, rows // SUBLANES, LANES)
            zi_ref[:, :, sl] = (b * tc - a * ts).reshape(SUBLANES, rows // SUBLANES, LANES)


def _fft_b_body(zr_ref, zi_ref, x_ref, mods_ref, cs_ref, ch_ref, w_ref, o_ref):
    rows = cs_ref.shape[0]
    cs = cs_ref[...]
    gw = FNET_GROUP_DIM
    ys = []
    for g in range(D_MODEL // gw):
        zr = zr_ref[:, :, g * gw:(g + 1) * gw].reshape(rows, gw).astype(BF16)
        zi = zi_ref[:, :, g * gw:(g + 1) * gw].reshape(rows, gw).astype(BF16)
        xr = _dot(cs, jnp.concatenate([zr, zi], axis=0))
        xi = _dot(cs, jnp.concatenate([zi, -zr], axis=0))
        ys.append(_dot(jnp.concatenate([xr, xi], axis=1).astype(BF16), ch_ref[...]).astype(BF16))
    y = _dot(jnp.concatenate(ys, axis=1), w_ref[...])
    x = x_ref[...].reshape(rows, D_MODEL)
    o_ref[...] = (x + mods_ref[3:4, :] * y).reshape(o_ref.shape)


def _fourier_layer(x, mods, w_bf16, seq_tables, ch):
    L = x.shape[0]
    n = math.isqrt(L)
    rows = n * SUBLANES
    oct_ = n // SUBLANES
    ka, cs, tc, ts = seq_tables
    x3 = x.reshape(n, n, D_MODEL)
    gw = FNET_GROUP_DIM
    zshape = jax.ShapeDtypeStruct((n, n, D_MODEL), F32)
    once = dict(pipeline_mode=pl.Buffered(1))
    zr, zi = pl.pallas_call(
        _fft_a_body,
        grid=(oct_,),
        in_specs=[pl.BlockSpec((n, SUBLANES, D_MODEL), lambda o: (0, o, 0)),
                  pl.BlockSpec((SUBLANES, D_MODEL), lambda o: (0, 0)),
                  pl.BlockSpec((2 * rows, rows), lambda o: (0, 0), **once),
                  pl.BlockSpec((1, rows, LANES), lambda o: (o, 0, 0)),
                  pl.BlockSpec((1, rows, LANES), lambda o: (o, 0, 0))],
        out_specs=[pl.BlockSpec((SUBLANES, n, D_MODEL), lambda o: (o, 0, 0)),
                   pl.BlockSpec((SUBLANES, n, D_MODEL), lambda o: (o, 0, 0))],
        out_shape=[zshape, zshape],
        compiler_params=_cparams(("arbitrary",), VMEM_LIMIT),
        name="fft_a",
    )(x3, mods, ka, tc, ts)
    out = pl.pallas_call(
        _fft_b_body,
        grid=(oct_,),
        in_specs=[pl.BlockSpec((n, SUBLANES, D_MODEL), lambda p: (0, p, 0)),
                  pl.BlockSpec((n, SUBLANES, D_MODEL), lambda p: (0, p, 0)),
                  pl.BlockSpec((n, SUBLANES, D_MODEL), lambda p: (0, p, 0)),
                  pl.BlockSpec((SUBLANES, D_MODEL), lambda p: (0, 0)),
                  pl.BlockSpec((rows, 2 * rows), lambda p: (0, 0), **once),
                  pl.BlockSpec((2 * gw, gw), lambda p: (0, 0), **once),
                  pl.BlockSpec((D_MODEL, D_MODEL), lambda p: (0, 0), **once)],
        out_specs=pl.BlockSpec((n, SUBLANES, D_MODEL), lambda p: (0, p, 0)),
        out_shape=jax.ShapeDtypeStruct((n, n, D_MODEL), F32),
        compiler_params=_cparams(("arbitrary",), VMEM_LIMIT),
        name="fft_b",
    )(zr, zi, x3, mods, cs, ch, w_bf16)
    return out.reshape(L, D_MODEL)


def _ctx_fourier_body(x_ref, mods_ref, f_ref, ch_ref, w_ref, o_ref):
    x = x_ref[...]
    n = x.shape[0]
    h = _mod_norm(x, mods_ref[...]).astype(BF16)
    g = _dot(f_ref[...], h)
    gr, gi = g[:n], g[n:]
    gw = FNET_GROUP_DIM
    ys = []
    for k in range(D_MODEL // gw):
        sl = slice(k * gw, (k + 1) * gw)
        ys.append(_dot(jnp.concatenate([gr[:, sl], gi[:, sl]], axis=1).astype(BF16), ch_ref[...]))
    y = jnp.concatenate(ys, axis=1).astype(BF16)
    o_ref[...] = x + mods_ref[3:4, :] * _dot(y, w_ref[...])


def _ctx_fourier_layer(ctx, mods, w_bf16, ch):
    n = ctx.shape[0]
    c, s = _cos_sin(n, n)
    f = (jnp.concatenate([jnp.asarray(c), -jnp.asarray(s)], axis=0) * (n ** -0.5)).astype(BF16)
    return pl.pallas_call(
        _ctx_fourier_body,
        out_shape=jax.ShapeDtypeStruct(ctx.shape, F32),
        compiler_params=pltpu.CompilerParams(vmem_limit_bytes=VMEM_LIMIT),
        name="ctx_fourier",
    )(ctx, mods, f, ch, w_bf16)


def _load_swiglu_weights(e, wi_hbm, wo_hbm, wi_res, wo_res, stage_i, stage_o, sems):
    n_i = wi_res.shape[1] // W_CHUNK
    total = n_i + wo_res.shape[0] // W_CHUNK

    def copy(c):
        slot = c % W_SLOTS
        if c < n_i:
            src = wi_hbm.at[e, :, pl.ds(c * W_CHUNK, W_CHUNK)]
            return pltpu.make_async_copy(src, stage_i.at[slot], sems.at[slot])
        src = wo_hbm.at[e, pl.ds((c - n_i) * W_CHUNK, W_CHUNK), :]
        return pltpu.make_async_copy(src, stage_o.at[slot], sems.at[slot])

    for c in range(min(W_SLOTS - 1, total)):
        copy(c).start()
    for c in range(total):
        if c + W_SLOTS - 1 < total:
            copy(c + W_SLOTS - 1).start()
        copy(c).wait()
        if c < n_i:
            wi_res[:, c * W_CHUNK:(c + 1) * W_CHUNK] = stage_i[c % W_SLOTS].astype(BF16)
        else:
            k = c - n_i
            wo_res[k * W_CHUNK:(k + 1) * W_CHUNK, :] = stage_o[c % W_SLOTS].astype(BF16)


def _swiglu_tile(h, wi_res, wo_res, act):
    for k in range(D_FF // FF_CHUNK):
        gate = _dot(h, wi_res[:, k * FF_CHUNK:(k + 1) * FF_CHUNK])
        up = _dot(h, wi_res[:, D_FF + k * FF_CHUNK:D_FF + (k + 1) * FF_CHUNK])
        act[:, k * FF_CHUNK:(k + 1) * FF_CHUNK] = (gate / (1.0 + jnp.exp(-gate)) * up).astype(BF16)
    return _dot(act[...], wo_res[...])


def _swiglu_scratch(tm):
    return [pltpu.VMEM((D_MODEL, 2 * D_FF), BF16), pltpu.VMEM((D_FF, D_MODEL), BF16),
            pltpu.VMEM((W_SLOTS, D_MODEL, W_CHUNK), F32), pltpu.VMEM((W_SLOTS, W_CHUNK, D_MODEL), F32),
            pltpu.VMEM((tm, D_FF), BF16), pltpu.SemaphoreType.DMA((W_SLOTS,))]


def _ffn_body(layer, x_ref, mods_ref, wi_hbm, wo_hbm, o_ref, wi_res, wo_res, stage_i, stage_o,
              act, sems):
    @pl.when(pl.program_id(0) == 0)
    def _():
        _load_swiglu_weights(layer, wi_hbm, wo_hbm, wi_res, wo_res, stage_i, stage_o, sems)

    x = x_ref[...]
    h = _mod_norm(x, mods_ref[...]).astype(BF16)
    o_ref[...] = x + mods_ref[3:4, :] * _swiglu_tile(h, wi_res, wo_res, act)


def _ffn_layer(x, mods, wi_all, wo_all, layer):
    n = x.shape[0]
    tm = min(ROW_TILE, n)
    return pl.pallas_call(
        functools.partial(_ffn_body, layer),
        grid=(n // tm,),
        in_specs=[pl.BlockSpec((tm, D_MODEL), lambda i: (i, 0)),
                  pl.BlockSpec((SUBLANES, D_MODEL), lambda i: (0, 0)),
                  pl.BlockSpec(memory_space=pl.ANY),
                  pl.BlockSpec(memory_space=pl.ANY)],
        out_specs=pl.BlockSpec((tm, D_MODEL), lambda i: (i, 0)),
        out_shape=jax.ShapeDtypeStruct(x.shape, F32),
        scratch_shapes=_swiglu_scratch(tm),
        compiler_params=_cparams(("arbitrary",), VMEM_LIMIT),
        name="ffn",
    )(x, mods, wi_all, wo_all)


def _qkv_body(x_ref, mods_ref, w_ref, bd_ref, gains_ref, cos_ref, sin_ref, q_ref, k_ref, v_ref):
    h = _mod_norm(x_ref[...], mods_ref[...]).astype(BF16)
    qkv = _dot(h, w_ref[...])
    bd = bd_ref[...]
    cos, sin = cos_ref[...], sin_ref[...]
    lane = lax.broadcasted_iota(jnp.int32, (1, LANES), 1)
    first_half = (lane % (HEAD_DIM // 2)) < (HEAD_DIM // 4)
    low_head = lane < HEAD_DIM

    def norm_rope(a, gain):
        sq = a * a
        hi = sq.astype(BF16)
        lo = (sq - hi.astype(F32)).astype(BF16)
        ms = _dot(hi, bd) + _dot(lo, bd)
        an = a * lax.rsqrt(ms + NORM_EPS) * gain
        partner = jnp.where(first_half,
                            pltpu.roll(an, LANES - HEAD_DIM // 4, 1),
                            pltpu.roll(an, HEAD_DIM // 4, 1))
        return an * cos + partner * sin

    def dup_heads(a):
        r = pltpu.roll(a, HEAD_DIM, 1)
        return jnp.where(low_head, a, r), jnp.where(low_head, r, a)

    nq = Q_COLS // LANES
    for t in range(nq):
        a = norm_rope(qkv[:, t * LANES:(t + 1) * LANES], gains_ref[0:1, :])
        q_ref[:, t * LANES:(t + 1) * LANES] = a.astype(BF16)
    for t in range(KV_COLS // LANES):
        kt = norm_rope(qkv[:, Q_COLS + t * LANES:Q_COLS + (t + 1) * LANES], gains_ref[1:2, :])
        k0, k1 = dup_heads(kt)
        k_ref[2 * t] = k0.astype(BF16)
        k_ref[2 * t + 1] = k1.astype(BF16)
        v0, v1 = dup_heads(qkv[:, Q_COLS + KV_COLS + t * LANES:Q_COLS + KV_COLS + (t + 1) * LANES])
        for hk, vv in ((2 * t, v0), (2 * t + 1, v1)):
            v_ref[2 * hk] = jnp.where(low_head, vv, 1.0).astype(BF16)
            v_ref[2 * hk + 1] = jnp.where(low_head, 1.0, vv).astype(BF16)


def _qkv_call(x, mods, wqkv_bf16, bd, gains, cos_t, sin_t):
    n = x.shape[0]
    tm = min(ROW_TILE, n)
    ncol = Q_COLS + 2 * KV_COLS
    return pl.pallas_call(
        _qkv_body,
        grid=(n // tm,),
        in_specs=[pl.BlockSpec((tm, D_MODEL), lambda i: (i, 0)),
                  pl.BlockSpec((SUBLANES, D_MODEL), lambda i: (0, 0)),
                  pl.BlockSpec((D_MODEL, ncol), lambda i: (0, 0)),
                  pl.BlockSpec((LANES, LANES), lambda i: (0, 0)),
                  pl.BlockSpec((SUBLANES, LANES), lambda i: (0, 0)),
                  pl.BlockSpec((tm, LANES), lambda i: (i, 0)),
                  pl.BlockSpec((tm, LANES), lambda i: (i, 0))],
        out_specs=[pl.BlockSpec((tm, Q_COLS), lambda i: (i, 0)),
                   pl.BlockSpec((N_KV_HEADS, tm, LANES), lambda i: (0, i, 0)),
                   pl.BlockSpec((2 * N_KV_HEADS, tm, LANES), lambda i: (0, i, 0))],
        out_shape=[jax.ShapeDtypeStruct((n, Q_COLS), BF16),
                   jax.ShapeDtypeStruct((N_KV_HEADS, n, LANES), BF16),
                   jax.ShapeDtypeStruct((2 * N_KV_HEADS, n, LANES), BF16)],
        compiler_params=_cparams(("arbitrary",), VMEM_LIMIT),
        name="qkv",
    )(x, mods, wqkv_bf16, bd, gains, cos_t, sin_t)


def _attn_body(sink_ref, q_ref, kp_ref, kc_ref, kn_ref, vp_ref, vc_ref, vn_ref,
               kx_ref, vx_ref, o_ref):
    b = pl.program_id(0)
    nb = pl.num_programs(0)
    blk = ATTN_BLOCK
    lane = lax.broadcasted_iota(jnp.int32, (1, LANES), 1)
    low_head = lane < HEAD_DIM
    qi = lax.broadcasted_iota(jnp.int32, (blk, 3 * blk), 0)
    kj = lax.broadcasted_iota(jnp.int32, (blk, 3 * blk), 1)
    valid = (kj >= qi) & (kj <= qi + 2 * WINDOW)
    valid = valid & ((kj >= blk) | (b > 0)) & ((kj < 2 * blk) | (b < nb - 1))
    per_kv = N_HEADS // N_KV_HEADS
    for g in range(N_KV_HEADS):
        kb = jnp.concatenate([kp_ref[g], kc_ref[g], kn_ref[g], kx_ref[g]], axis=0)
        vbs = [jnp.concatenate([vp_ref[2 * g + a], vc_ref[2 * g + a], vn_ref[2 * g + a],
                                vx_ref[2 * g + a]], axis=0) for a in range(2)]
        parts = []
        for p in range(per_kv // 2):
            t = g * (per_kv // 2) + p
            qt = q_ref[:, t * LANES:(t + 1) * LANES]
            zero = jnp.zeros_like(qt)
            parts += [jnp.where(low_head, qt, zero), jnp.where(low_head, zero, qt)]
        lhs = jnp.concatenate(parts, axis=0)
        s = lax.dot_general(lhs, kb, (((1,), (1,)), ((), ())), preferred_element_type=F32)
        outs = []
        for hh in range(per_kv):
            sink = sink_ref[g * per_kv + hh] * LOG2_E
            sh = s[hh * blk:(hh + 1) * blk]
            s_loc = jnp.where(valid, sh[:, :3 * blk], NEG_INF)
            s_ctx = sh[:, 3 * blk:]
            m = jnp.maximum(jnp.maximum(jnp.max(s_loc, axis=-1, keepdims=True),
                                        jnp.max(s_ctx, axis=-1, keepdims=True)), sink)
            pr = jnp.concatenate([jnp.exp2(s_loc - m), jnp.exp2(s_ctx - m)], axis=1).astype(BF16)
            pv = _dot(pr, vbs[hh % 2])
            den = pltpu.roll(pv, HEAD_DIM, 1) + jnp.exp2(sink - m)
            outs.append(pv / den)
        for p in range(per_kv // 2):
            t = g * (per_kv // 2) + p
            o_ref[:, t * LANES:(t + 1) * LANES] = jnp.where(
                low_head, outs[2 * p], outs[2 * p + 1]).astype(BF16)


def _attn_call(sink, q, kd, vd, kx, vx):
    n = q.shape[0]
    blk = ATTN_BLOCK
    nb = n // blk
    nctx = kx.shape[1]
    k_spec = lambda f: pl.BlockSpec((N_KV_HEADS, blk, LANES), f)
    v_spec = lambda f: pl.BlockSpec((2 * N_KV_HEADS, blk, LANES), f)
    prev = lambda b: (0, jnp.maximum(b - 1, 0), 0)
    cur = lambda b: (0, b, 0)
    nxt = lambda b: (0, jnp.minimum(b + 1, nb - 1), 0)
    return pl.pallas_call(
        _attn_body,
        grid=(nb,),
        in_specs=[pl.BlockSpec(memory_space=pltpu.SMEM),
                  pl.BlockSpec((blk, Q_COLS), lambda b: (b, 0)),
                  k_spec(prev), k_spec(cur), k_spec(nxt),
                  v_spec(prev), v_spec(cur), v_spec(nxt),
                  pl.BlockSpec((N_KV_HEADS, nctx, LANES), lambda b: (0, 0, 0)),
                  pl.BlockSpec((2 * N_KV_HEADS, nctx, LANES), lambda b: (0, 0, 0))],
        out_specs=pl.BlockSpec((blk, Q_COLS), lambda b: (b, 0)),
        out_shape=jax.ShapeDtypeStruct((n, Q_COLS), BF16),
        compiler_params=_cparams(("arbitrary",), VMEM_LIMIT),
        name="attn",
    )(sink, q, kd, kd, kd, vd, vd, vd, kx, vx)


def _proj_body(a_ref, x_ref, mods_ref, w_ref, o_ref):
    o_ref[...] = x_ref[...] + mods_ref[3:4, :] * _dot(a_ref[...], w_ref[...])


def _proj_call(a, x, mods, w_bf16):
    n = x.shape[0]
    tm = min(ROW_TILE, n)
    return pl.pallas_call(
        _proj_body,
        grid=(n // tm,),
        in_specs=[pl.BlockSpec((tm, a.shape[1]), lambda i: (i, 0)),
                  pl.BlockSpec((tm, D_MODEL), lambda i: (i, 0)),
                  pl.BlockSpec((SUBLANES, D_MODEL), lambda i: (0, 0)),
                  pl.BlockSpec(w_bf16.shape, lambda i: (0, 0))],
        out_specs=pl.BlockSpec((tm, D_MODEL), lambda i: (i, 0)),
        out_shape=jax.ShapeDtypeStruct(x.shape, F32),
        compiler_params=_cparams(("arbitrary",), VMEM_LIMIT),
        name="proj",
    )(a, x, mods, w_bf16)


def _rope_lane_tables(length):
    rows = length // GRID_W
    row_pos = jnp.repeat(jnp.arange(rows, dtype=F32), GRID_W)
    col_pos = jnp.tile(jnp.arange(GRID_W, dtype=F32), rows)
    axis_dim = HEAD_DIM // 2
    inv_freq = ROPE_BASE ** (-jnp.arange(0, axis_dim, 2, dtype=F32) / axis_dim)
    ang_r = row_pos[:, None] * inv_freq[None, :]
    ang_c = col_pos[:, None] * inv_freq[None, :]
    cos_h = jnp.concatenate([jnp.cos(ang_r), jnp.cos(ang_r), jnp.cos(ang_c), jnp.cos(ang_c)], axis=1)
    sin_h = jnp.concatenate([-jnp.sin(ang_r), jnp.sin(ang_r), -jnp.sin(ang_c), jnp.sin(ang_c)], axis=1)
    reps = LANES // HEAD_DIM
    return jnp.tile(cos_h, (1, reps)), jnp.tile(sin_h, (1, reps))


def _attention_layer(x, ctx, mods_x, mods_c, wqkv, wo, q_gain, k_gain, sink):
    L = x.shape[0]
    nctx = ctx.shape[0]
    wqkv_b = wqkv.astype(BF16)
    head = np.arange(LANES) // HEAD_DIM
    bd = jnp.asarray((head[:, None] == head[None, :]).astype(np.float32) / HEAD_DIM).astype(BF16)
    reps = LANES // HEAD_DIM
    gains = jnp.zeros((SUBLANES, LANES), F32)
    gains = gains.at[0].set(jnp.tile(q_gain, reps) * (HEAD_DIM ** -0.5 * LOG2_E))
    gains = gains.at[1].set(jnp.tile(k_gain, reps))
    cos_t, sin_t = _rope_lane_tables(L)
    q, kd, vd = _qkv_call(x, mods_x, wqkv_b, bd, gains, cos_t, sin_t)
    ones = jnp.ones((nctx, LANES), F32)
    _, kx, vx = _qkv_call(ctx, mods_c, wqkv_b, bd, gains, ones, jnp.zeros_like(ones))
    o = _attn_call(sink, q, kd, vd, kx, vx)
    return _proj_call(o, x, mods_x, wo.astype(BF16))


def _pool_body(xp_ref, xc_ref, xn_ref, mods_ref, w_ref, o_ref, h_scr):
    i = pl.program_id(0)
    tm = xc_ref.shape[0]
    total = tm * pl.num_programs(0)
    x = xc_ref[...]
    mods = mods_ref[...]
    h_scr[0:POOL_HALO, :] = jnp.where(i > 0, _mod_norm(xp_ref[...], mods), 0.0)
    h_scr[POOL_HALO:POOL_HALO + tm, :] = _mod_norm(x, mods)
    h_scr[POOL_HALO + tm:, :] = jnp.where(i < pl.num_programs(0) - 1, _mod_norm(xn_ref[...], mods), 0.0)
    t = i * tm + lax.broadcasted_iota(jnp.int32, (tm, 1), 0)
    gd = POOL_GROUP_DIM
    for g, win in enumerate(POOL_WINDOWS):
        sl = slice(g * gd, (g + 1) * gd)
        half = win // 2
        tot = h_scr[POOL_HALO - half:POOL_HALO - half + tm, sl]
        for s in range(-half + 1, half):
            tot = tot + h_scr[POOL_HALO + s:POOL_HALO + s + tm, sl]
        lo = jnp.maximum(t - half, 0)
        hi = jnp.minimum(t + half - 1, total - 1)
        cnt = (hi - lo + 1).astype(F32)
        pooled = (tot / cnt - h_scr[POOL_HALO:POOL_HALO + tm, sl]).astype(BF16)
        y = _dot(pooled, w_ref[g]) * mods[4:5, sl]
        o_ref[:, sl] = x[:, sl] + mods[3:4, sl] * y


def _pool_layer(x, mods, w_bf16):
    n = x.shape[0]
    tm = min(ROW_TILE, n)
    r = tm // POOL_HALO
    last = n // POOL_HALO - 1
    return pl.pallas_call(
        _pool_body,
        grid=(n // tm,),
        in_specs=[pl.BlockSpec((POOL_HALO, D_MODEL), lambda i: (jnp.maximum(i * r - 1, 0), 0)),
                  pl.BlockSpec((tm, D_MODEL), lambda i: (i, 0)),
                  pl.BlockSpec((POOL_HALO, D_MODEL), lambda i: (jnp.minimum((i + 1) * r, last), 0)),
                  pl.BlockSpec((SUBLANES, D_MODEL), lambda i: (0, 0)),
                  pl.BlockSpec(w_bf16.shape, lambda i: (0, 0, 0))],
        out_specs=pl.BlockSpec((tm, D_MODEL), lambda i: (i, 0)),
        out_shape=jax.ShapeDtypeStruct(x.shape, F32),
        scratch_shapes=[pltpu.VMEM((tm + 2 * POOL_HALO, D_MODEL), F32)],
        compiler_params=_cparams(("arbitrary",), VMEM_LIMIT),
        name="pool",
    )(x, x, x, mods, w_bf16)


def _router_body(x_ref, mods_ref, r_ref, h_ref, route_ref, cnt_ref):
    h = _mod_norm(x_ref[...], mods_ref[...])
    h_hi = h.astype(BF16)
    h_ref[...] = h_hi
    h_lo = (h - h_hi.astype(F32)).astype(BF16)
    r = r_ref[...]
    r_hi = r.astype(BF16)
    r_lo = (r - r_hi.astype(F32)).astype(BF16)
    logits = _dot(h_hi, r_hi) + (_dot(h_lo, r_hi) + _dot(h_hi, r_lo))
    lane = lax.broadcasted_iota(jnp.int32, logits.shape, 1)
    logits = jnp.where(lane < N_EXPERTS, logits, -jnp.inf)
    m1 = jnp.max(logits, axis=-1, keepdims=True)
    i1 = jnp.min(jnp.where(logits == m1, lane, LANES), axis=-1, keepdims=True)
    rest = jnp.where(lane == i1, -jnp.inf, logits)
    m2 = jnp.max(rest, axis=-1, keepdims=True)
    i2 = jnp.min(jnp.where(rest == m2, lane, LANES), axis=-1, keepdims=True)
    e = jnp.exp(m2 - m1)
    w1 = 1.0 / (1.0 + e)
    w2 = e / (1.0 + e)
    tm = h.shape[0]
    oh = jnp.where((lane == i1) | (lane == i2 + N_EXPERTS), 1.0, 0.0)
    before = (lax.broadcasted_iota(jnp.int32, (tm, tm), 1)
              < lax.broadcasted_iota(jnp.int32, (tm, tm), 0))
    prior = _dot(jnp.where(before, 1.0, 0.0).astype(BF16), oh.astype(BF16))
    cnt = jnp.sum(oh, axis=0, keepdims=True)
    r1 = jnp.sum(jnp.where(lane == i1, prior, 0.0), axis=-1, keepdims=True)
    r2 = (jnp.sum(jnp.where(lane == i2 + N_EXPERTS, prior, 0.0), axis=-1, keepdims=True)
          + jnp.sum(jnp.where(lane == i2, cnt, 0.0), axis=-1, keepdims=True))
    lane8 = lane[0:SUBLANES, :]
    cnt8 = jnp.broadcast_to(cnt, (SUBLANES, LANES))
    both = cnt8 + pltpu.roll(cnt8, LANES - N_EXPERTS, 1)
    seg_len = jnp.where(lane8 < N_EXPERTS, jnp.ceil(both * (1.0 / SEG_ALIGN)) * SEG_ALIGN, 0.0)
    incl = seg_len
    shift = 1
    while shift < N_EXPERTS:
        incl = incl + jnp.where(lane8 >= shift, pltpu.roll(incl, shift, 1), 0.0)
        shift *= 2
    seg = incl - seg_len
    p1 = r1 + jnp.sum(jnp.where(lane == i1, seg[0:1, :], 0.0), axis=-1, keepdims=True)
    p2 = r2 + jnp.sum(jnp.where(lane == i2, seg[0:1, :], 0.0), axis=-1, keepdims=True)
    col = lax.broadcasted_iota(jnp.int32, route_ref.shape, 1)
    vals = (i1.astype(F32), i2.astype(F32), w1, w2, p1, p2)
    out = jnp.zeros(route_ref.shape, F32)
    for k, v in enumerate(vals):
        out = jnp.where(col == k, v, out)
    route_ref[...] = out
    row = lax.broadcasted_iota(jnp.int32, cnt_ref.shape[1:], 0)
    cnt_ref[0] = jnp.where(row == 0, seg_len, jnp.where(row == 1, seg, 0.0))


def _router_call(x, mods, router_pad):
    n = x.shape[0]
    tm = min(ROUTE_TILE, n)
    return pl.pallas_call(
        _router_body,
        grid=(n // tm,),
        in_specs=[pl.BlockSpec((tm, D_MODEL), lambda i: (i, 0)),
                  pl.BlockSpec((SUBLANES, D_MODEL), lambda i: (0, 0)),
                  pl.BlockSpec((D_MODEL, LANES), lambda i: (0, 0))],
        out_specs=[pl.BlockSpec((tm, D_MODEL), lambda i: (i, 0)),
                   pl.BlockSpec((tm, SUBLANES), lambda i: (i, 0)),
                   pl.BlockSpec((1, SUBLANES, LANES), lambda i: (i, 0, 0))],
        out_shape=[jax.ShapeDtypeStruct((n, D_MODEL), BF16),
                   jax.ShapeDtypeStruct((n, SUBLANES), F32),
                   jax.ShapeDtypeStruct((n // tm, SUBLANES, LANES), F32)],
        compiler_params=_cparams(("arbitrary",), VMEM_LIMIT),
        name="router",
    )(x, mods, router_pad)


def _segment_copies(i, seg_ref, dst_ref, len_ref, tile_buf, sorted_hbm, sem, to_hbm):
    for e in range(N_EXPERTS):
        length = len_ref[i * N_EXPERTS + e]
        seg = seg_ref[i * N_EXPERTS + e]
        dst = dst_ref[i * N_EXPERTS + e]
        size = SEG_ALIGN
        while size <= tile_buf.shape[0] // 2:
            done = length & ~(2 * size - 1)
            a = tile_buf.at[pl.ds(pl.multiple_of(seg + done, SEG_ALIGN), size)]
            b = sorted_hbm.at[pl.ds(pl.multiple_of(dst + done, SEG_ALIGN), size)]
            copy = pltpu.make_async_copy(a, b, sem) if to_hbm else pltpu.make_async_copy(b, a, sem)
            yield (length & size) != 0, copy
            size *= 2


def _scatter_body(seg_ref, dst_ref, len_ref, dead_ref, h_ref, prow_ref, xs_ref, sbuf, zbuf, sems):
    i = pl.program_id(0)
    n_route = pl.num_programs(0)
    last = n_route - 1
    slot = i % 2

    @pl.when(i == last)
    def _():
        zbuf[...] = jnp.zeros(zbuf.shape, BF16)
        fill_sem = sems.at[2]
        tails = list(_segment_copies(n_route, seg_ref, dst_ref, len_ref, zbuf, xs_ref, fill_sem, True))
        for pred, copy in tails:
            pl.when(pred)(copy.start)

        def dead_copy(k):
            row = pl.multiple_of(dead_ref[0] + k * MOE_TILE, MOE_TILE)
            return pltpu.make_async_copy(zbuf, xs_ref.at[pl.ds(row, MOE_TILE)], fill_sem)

        def start(k, c):
            dead_copy(k).start()
            return c

        def wait(k, c):
            dead_copy(k).wait()
            return c

        lax.fori_loop(0, dead_ref[1], start, 0)
        for pred, copy in tails:
            pl.when(pred)(copy.wait)
        lax.fori_loop(0, dead_ref[1], wait, 0)

    cap, tm = sbuf.shape[1], h_ref.shape[0]
    r = lax.broadcasted_iota(jnp.int32, (cap, tm), 0)
    hit = (r == prow_ref[0:1, :]) | (r == prow_ref[1:2, :])
    sel = jnp.where(hit, 1.0, 0.0).astype(BF16)
    sbuf[slot] = _dot(sel, h_ref[...]).astype(BF16)

    def copies(tile, s):
        return _segment_copies(tile, seg_ref, dst_ref, len_ref, sbuf.at[s], xs_ref, sems.at[s], True)

    for pred, copy in copies(i, slot):
        pl.when(pred)(copy.start)
    for pred, copy in copies(jnp.maximum(i - 1, 0), 1 - slot):
        pl.when(pred & (i > 0))(copy.wait)
    for pred, copy in copies(i, slot):
        pl.when(pred & (i == last))(copy.wait)


def _route_cap(tm):
    return 2 * tm + N_EXPERTS * SEG_ALIGN


def _scatter_call(seg, dst, seg_len, dead, h, prow, n_rows):
    n = h.shape[0]
    tm = min(ROUTE_TILE, n)
    return pl.pallas_call(
        _scatter_body,
        grid_spec=pltpu.PrefetchScalarGridSpec(
            num_scalar_prefetch=4,
            grid=(n // tm,),
            in_specs=[pl.BlockSpec((tm, D_MODEL), lambda i, *_: (i, 0)),
                      pl.BlockSpec((SUBLANES, tm), lambda i, *_: (i, 0))],
            out_specs=pl.BlockSpec(memory_space=pl.ANY),
            scratch_shapes=[pltpu.VMEM((2, _route_cap(tm), D_MODEL), BF16),
                            pltpu.VMEM((MOE_TILE, D_MODEL), BF16),
                            pltpu.SemaphoreType.DMA((3,))]),
        out_shape=jax.ShapeDtypeStruct((n_rows, D_MODEL), BF16),
        compiler_params=_cparams(("arbitrary",), VMEM_LIMIT),
        name="moe_scatter",
    )(seg, dst, seg_len, dead, h, prow)


def _moe_body(te_ref, tv_ref, tf_ref, xs_ref, wi_hbm, wo_hbm, ys_ref, wi_res, wo_res, stage_i,
              stage_o, act, sems):
    i = pl.program_id(0)

    @pl.when(tf_ref[i] > 0)
    def _():
        _load_swiglu_weights(te_ref[i], wi_hbm, wo_hbm, wi_res, wo_res, stage_i, stage_o, sems)

    @pl.when(tv_ref[i] > 0)
    def _():
        ys_ref[...] = _swiglu_tile(xs_ref[...], wi_res, wo_res, act).astype(BF16)

    @pl.when(tv_ref[i] == 0)
    def _():
        ys_ref[...] = jnp.zeros(ys_ref.shape, BF16)


def _moe_call(tile_expert, tile_live, tile_first, xs, wi_all, wo_all):
    n_rows = xs.shape[0]
    return pl.pallas_call(
        _moe_body,
        grid_spec=pltpu.PrefetchScalarGridSpec(
            num_scalar_prefetch=3,
            grid=(n_rows // MOE_TILE,),
            in_specs=[pl.BlockSpec((MOE_TILE, D_MODEL), lambda i, *_: (i, 0)),
                      pl.BlockSpec(memory_space=pl.ANY),
                      pl.BlockSpec(memory_space=pl.ANY)],
            out_specs=pl.BlockSpec((MOE_TILE, D_MODEL), lambda i, *_: (i, 0)),
            scratch_shapes=_swiglu_scratch(MOE_TILE)),
        out_shape=jax.ShapeDtypeStruct((n_rows, D_MODEL), BF16),
        compiler_params=_cparams(("arbitrary",), VMEM_LIMIT),
        name="moe_ffn",
    )(tile_expert, tile_live, tile_first, xs, wi_all, wo_all)


def _combine_body(seg_ref, dst_ref, len_ref, x_ref, route_ref, pcol_ref, mods_ref, ys_ref, o_ref,
                  ybuf, sems):
    i = pl.program_id(0)
    last = pl.num_programs(0) - 1
    slot = i % 2
    tm, cap = x_ref.shape[0], ybuf.shape[1]

    def copies(tile, s):
        return _segment_copies(tile, seg_ref, dst_ref, len_ref, ybuf.at[s], ys_ref, sems.at[s], False)

    @pl.when(i == 0)
    def _():
        ybuf[...] = jnp.zeros(ybuf.shape, BF16)

    for pred, copy in copies(0, 0):
        pl.when(pred & (i == 0))(copy.start)
    for pred, copy in copies(jnp.minimum(i + 1, last), 1 - slot):
        pl.when(pred & (i < last))(copy.start)
    for pred, copy in copies(i, slot):
        pl.when(pred)(copy.wait)
    c = lax.broadcasted_iota(jnp.int32, (tm, cap), 1)
    pcol = pcol_ref[...]
    route = route_ref[...]
    sel = jnp.where(c == pcol[:, 0:1], route[:, 2:3], jnp.where(c == pcol[:, 1:2], route[:, 3:4], 0.0))
    o_ref[...] = x_ref[...] + mods_ref[3:4, :] * _dot(sel.astype(BF16), ybuf[slot])


def _combine_call(seg, dst, seg_len, x, route, pcol, mods, ys):
    n = x.shape[0]
    tm = min(ROUTE_TILE, n)
    return pl.pallas_call(
        _combine_body,
        grid_spec=pltpu.PrefetchScalarGridSpec(
            num_scalar_prefetch=3,
            grid=(n // tm,),
            in_specs=[pl.BlockSpec((tm, D_MODEL), lambda i, *_: (i, 0)),
                      pl.BlockSpec((tm, SUBLANES), lambda i, *_: (i, 0)),
                      pl.BlockSpec((tm, SUBLANES), lambda i, *_: (i, 0)),
                      pl.BlockSpec((SUBLANES, D_MODEL), lambda i, *_: (0, 0)),
                      pl.BlockSpec(memory_space=pl.ANY)],
            out_specs=pl.BlockSpec((tm, D_MODEL), lambda i, *_: (i, 0)),
            scratch_shapes=[pltpu.VMEM((2, _route_cap(tm), D_MODEL), BF16),
                            pltpu.SemaphoreType.DMA((2,))]),
        out_shape=jax.ShapeDtypeStruct(x.shape, F32),
        compiler_params=_cparams(("arbitrary",), VMEM_LIMIT),
        name="moe_combine",
    )(seg, dst, seg_len, x, route, pcol, mods, ys)


def _routing_tables(route, counts, tm, n_tiles):
    nt = counts.shape[0]
    seg_len = counts[:, 0, :N_EXPERTS].astype(jnp.int32)
    seg = counts[:, 1, :N_EXPERTS].astype(jnp.int32)
    tiles = (jnp.sum(seg_len, axis=0) + MOE_TILE - 1) // MOE_TILE
    tile_end = jnp.cumsum(tiles)
    start = (tile_end - tiles) * MOE_TILE
    dst = start[None, :] + jnp.cumsum(seg_len, axis=0) - seg_len
    t = jnp.arange(n_tiles, dtype=jnp.int32)
    expert = jnp.sum((t[:, None] >= tile_end[None, :]).astype(jnp.int32), axis=1)
    live = (t < tile_end[-1]).astype(jnp.int32)
    last_expert = jnp.sum((tile_end[-1] - 1 >= tile_end).astype(jnp.int32))
    expert = jnp.where(live > 0, expert, last_expert).astype(jnp.int32)
    prev = jnp.concatenate([jnp.full((1,), -1, jnp.int32), expert[:-1]])
    first = (live * (expert != prev)).astype(jnp.int32)
    p12 = route[:, 4:6].astype(jnp.int32)
    pcol = jnp.concatenate([p12, jnp.zeros((p12.shape[0], SUBLANES - 2), jnp.int32)], axis=1)
    prow = jnp.concatenate([p12.reshape(nt, tm, 2).transpose(0, 2, 1),
                            jnp.full((nt, SUBLANES - 2, tm), -1, jnp.int32)], axis=1)
    rows_e = jnp.sum(seg_len, axis=0)
    seg = jnp.concatenate([seg, jnp.zeros((1, N_EXPERTS), jnp.int32)], axis=0)
    dst = jnp.concatenate([dst, (start + rows_e)[None, :]], axis=0)
    seg_len = jnp.concatenate([seg_len, (tiles * MOE_TILE - rows_e)[None, :]], axis=0)
    dead = jnp.stack([tile_end[-1] * MOE_TILE, n_tiles - tile_end[-1]]).astype(jnp.int32)
    return (seg.reshape(-1), dst.reshape(-1).astype(jnp.int32), seg_len.reshape(-1), dead,
            expert, live, first, pcol, prow.reshape(nt * SUBLANES, tm))


def _moe_layer(x, mods, router, wi_all, wo_all, layer):
    n = x.shape[0]
    tm = min(ROUTE_TILE, n)
    router_pad = jnp.zeros((D_MODEL, LANES), F32).at[:, :N_EXPERTS].set(router)
    h, route, counts = _router_call(x, mods, router_pad)
    max_rows = 2 * n + (n // tm) * N_EXPERTS * (SEG_ALIGN - 1)
    n_tiles = -(-max_rows // MOE_TILE) + N_EXPERTS
    seg, dst, seg_len, dead, tile_expert, tile_live, tile_first, pcol, prow = _routing_tables(
        route, counts, tm, n_tiles)
    xs = _scatter_call(seg, dst, seg_len, dead, h, prow, n_tiles * MOE_TILE)
    ys = _moe_call(tile_expert + layer * N_EXPERTS, tile_live, tile_first, xs, wi_all, wo_all)
    return _combine_call(seg, dst, seg_len, x, route, pcol, mods, ys)


def _mods(gain, shift, scale, gate, extra=None):
    rows = [gain, shift, scale, gate, extra if extra is not None else jnp.zeros_like(gain)]
    m = jnp.stack(rows, axis=0)
    return jnp.concatenate([m, jnp.zeros((SUBLANES - m.shape[0], m.shape[1]), F32)], axis=0)


def kernel(x, c, ctx, c_ctx, ada_w, ada_b, norm_mix, norm_ffn, fnet_w, attn_wqkv, attn_q_gain,
           attn_k_gain, attn_sink, attn_wo, pool_w, pool_scale, ffn_wi, ffn_wo, moe_router,
           moe_wi, moe_wo):
    assert x.shape[0] == 1 and x.shape[2] == D_MODEL
    depth = ada_w.shape[0]
    xs = x[0]
    cs = ctx[0]
    ada = _ada_call(jnp.stack([c[0], c_ctx]), ada_w, ada_b)

    attn_layers = [i for i in range(depth) if i % N_MIXERS == 1]
    last_ctx_read = attn_layers[-1] if attn_layers else -1
    n_side = math.isqrt(xs.shape[0])
    assert n_side * n_side == xs.shape[0] and n_side % SUBLANES == 0
    seq_tables = _sequence_tables(n_side)
    ch = _channel_table()
    moe_wi_all = moe_wi.reshape((-1,) + moe_wi.shape[2:])
    moe_wo_all = moe_wo.reshape((-1,) + moe_wo.shape[2:])

    for i in range(depth):
        mixer = i % N_MIXERS
        j = i // N_MIXERS
        f = i // 2
        ctx_full = i < last_ctx_read
        ctx_live = i <= last_ctx_read
        sh1, sc1, g1, sh2, sc2, g2 = [ada[i, 0, k * D_MODEL:(k + 1) * D_MODEL] for k in range(6)]
        csh1, csc1, cg1, csh2, csc2, cg2 = [ada[i, 1, k * D_MODEL:(k + 1) * D_MODEL] for k in range(6)]
        extra = pool_scale[j] if mixer == 2 else None
        m1 = _mods(norm_mix[i], sh1, sc1, g1, extra)
        m2 = _mods(norm_ffn[i], sh2, sc2, g2)
        cm1 = _mods(norm_mix[i], csh1, csc1, cg1, extra)
        cm2 = _mods(norm_ffn[i], csh2, csc2, cg2)

        if mixer == 0:
            wb = fnet_w[j].astype(BF16)
            xs = _fourier_layer(xs, m1, wb, seq_tables, ch)
            if ctx_full:
                cs = _ctx_fourier_layer(cs, cm1, wb, ch)
        elif mixer == 1:
            assert ctx_live and not ctx_full
            xs = _attention_layer(xs, cs, m1, cm1, attn_wqkv[j], attn_wo[j], attn_q_gain[j],
                                  attn_k_gain[j], attn_sink[j])
        else:
            wb = pool_w[j].astype(BF16)
            xs = _pool_layer(xs, m1, wb)
            if ctx_full:
                cs = _pool_layer(cs, cm1, wb)

        if i % 2 == 0:
            xs = _ffn_layer(xs, m2, ffn_wi, ffn_wo, f)
            if ctx_full:
                cs = _ffn_layer(cs, cm2, ffn_wi, ffn_wo, f)
        else:
            xs = _moe_layer(xs, m2, moe_router[f], moe_wi_all, moe_wo_all, f)
            if ctx_full:
                cs = _moe_layer(cs, cm2, moe_router[f], moe_wi_all, moe_wo_all, f)
    return xs[None]
```
